```python
import jax
import jax.numpy as jnp
from jax import lax
import numpy as np

D_MODEL = 1024
BATCH = 16
SEQ = 256
DEPTH = 2
DEC_BATCH = 4
DEC_SEQ = 4096
PAST_LEN = 512

GRID_W = 64
NORM_EPS = 1e-6

RWKV_HEADS = 8
RWKV_HEAD_DIM = 64
RWKV_WIDTH = RWKV_HEADS * RWKV_HEAD_DIM
DECAY_LORA = 64
ICLR_LORA = 64
GATE_LORA = 128
RWKV_GN_EPS = 6.4e-4

N_HEADS = 8
KV_HEADS = 2
HEAD_DIM = 64
ATT_WIDTH = N_HEADS * HEAD_DIM
KV_WIDTH = KV_HEADS * HEAD_DIM
ROPE_THETA = 10000.0
Q_BLOCK = 128

GLA_HEADS = 4
GLA_DK = 128
GLA_DV = 256
GLA_K_WIDTH = GLA_HEADS * GLA_DK
GLA_V_WIDTH = GLA_HEADS * GLA_DV
GLA_LORA = 16
GLA_GATE_NORMALIZER = 16.0
GLA_CHUNK = 64

N_EXPERTS = 32
TOP_K = 4
D_EXPERT = 1024
SWIGLU_ALPHA = 1.702
SWIGLU_LIMIT = 7.0
EXPERT_BLOCK = 256

RWKV_SPLITS = (RWKV_WIDTH, RWKV_WIDTH, RWKV_WIDTH, 2 * DECAY_LORA, 2 * ICLR_LORA, GATE_LORA)
REST_SPLITS = (ATT_WIDTH, KV_WIDTH, KV_WIDTH,
               GLA_K_WIDTH, GLA_K_WIDTH, GLA_V_WIDTH, 2 * GLA_LORA, GLA_V_WIDTH,
               D_MODEL, D_MODEL, D_MODEL)
RWKV_COLS = sum(RWKV_SPLITS)
IN_COLS = RWKV_COLS + sum(REST_SPLITS)

kernel_name = "hybrid_rwkv7_gqa_gla_moe_diffusion_step"


def split_cols(z, sizes):
    idx, acc = [], 0
    for s in sizes[:-1]:
        acc += s
        idx.append(acc)
    return jnp.split(z, idx, axis=-1)


def rmsnorm(x, g, eps=NORM_EPS):
    xf = x.astype(jnp.float32)
    y = xf * lax.rsqrt(jnp.mean(xf * xf, axis=-1, keepdims=True) + eps)
    return (y * g.astype(jnp.float32)).astype(x.dtype)


def adaln(cond, w, b):
    m = jax.nn.silu(cond) @ w + b
    return jnp.split(m[..., None, :], 6, axis=-1)


def centred_shift(z):
    zero = jnp.zeros_like(z[:, :1])
    prev = jnp.concatenate([zero, z[:, :-1]], axis=1)
    nxt = jnp.concatenate([z[:, 1:], zero], axis=1)
    return 0.5 * (prev + nxt)


def axial_rope(x):
    B, T, H, Dh = x.shape
    rows = T // GRID_W
    quarter = Dh // 4
    inv_freq = ROPE_THETA ** (-jnp.arange(quarter, dtype=jnp.float32) / quarter)
    row = jnp.repeat(jnp.arange(rows, dtype=jnp.float32), GRID_W)
    col = (jnp.arange(rows * GRID_W) % GRID_W).astype(jnp.float32)
    ang = jnp.stack([row[:, None] * inv_freq, col[:, None] * inv_freq], axis=1)
    cos = jnp.cos(ang)[None, :, None]
    sin = jnp.sin(ang)[None, :, None]
    xr = x.astype(jnp.float32).reshape(B, T, H, 2, 2, quarter)
    x0, x1 = xr[..., 0, :], xr[..., 1, :]
    out = jnp.stack([x0 * cos - x1 * sin, x0 * sin + x1 * cos], axis=-2)
    return out.reshape(B, T, H, Dh).astype(x.dtype)


def block_attention(q, k, v):
    B, T, H, Dh = q.shape
    KVH = k.shape[2]
    G = H // KVH
    nb = T // Q_BLOCK
    kf, vf = k.astype(jnp.float32), v.astype(jnp.float32)
    qb = jnp.moveaxis(q.astype(jnp.float32).reshape(B, nb, Q_BLOCK, KVH, G, Dh), 1, 0)

    def one_block(qblk):
        s = jnp.einsum("bqkgd,bskd->bkgqs", qblk, kf) * (Dh ** -0.5)
        p = jax.nn.softmax(s, axis=-1)
        return jnp.einsum("bkgqs,bskd->bqkgd", p, vf)

    o = lax.map(one_block, qb)
    return jnp.moveaxis(o, 0, 1).reshape(B, T, H * Dh).astype(q.dtype)


def attn_mixer(q, k, v, lp, cache):
    B, T, _ = q.shape
    qh = rmsnorm(q.reshape(B, T, N_HEADS, HEAD_DIM), lp["attn_qn"])
    kh = rmsnorm(k.reshape(B, T, KV_HEADS, HEAD_DIM), lp["attn_kn"])
    vh = v.reshape(B, T, KV_HEADS, HEAD_DIM)
    if cache is None:
        return block_attention(qh, kh, vh), kh, vh
    ck, cv = cache
    keys = jnp.concatenate([axial_rope(kh), ck.astype(kh.dtype)], axis=1)
    vals = jnp.concatenate([vh, cv.astype(vh.dtype)], axis=1)
    return block_attention(axial_rope(qh), keys, vals), None, None


def rwkv_scan(r, w, k, v, kk, a, s0, reverse):
    xs = tuple(jnp.moveaxis(t, 1, 0) for t in (r, w, k, v, kk, a))

    def step(S, inp):
        r_t, w_t, k_t, v_t, kk_t, a_t = inp
        sa = jnp.einsum("bhvk,bhk->bhv", S, -kk_t)
        S = (S * w_t[:, :, None, :] + sa[..., None] * (kk_t * a_t)[:, :, None, :]
             + v_t[..., None] * k_t[:, :, None, :])
        return S, jnp.einsum("bhvk,bhk->bhv", S, r_t)

    s, ys = lax.scan(step, s0, xs, reverse=reverse)
    return jnp.moveaxis(ys, 0, 1), s


def rwkv_mixer(r, k, v, lw, la, lg, lp, init):
    B, T, _ = r.shape
    f32 = jnp.float32

    def heads(t):
        return t.astype(f32).reshape(B, T, RWKV_HEADS, RWKV_HEAD_DIM)

    r_h, v_h = heads(r), heads(v)
    kk = heads(k * lp["rwkv_kk"])
    kk = kk / jnp.maximum(jnp.sqrt(jnp.sum(kk * kk, axis=-1, keepdims=True)), 1e-12)
    lw_d = jnp.split(jnp.tanh(lw), 2, axis=-1)
    la_d = jnp.split(la, 2, axis=-1)
    ys, bonuses, finals = [], [], []
    for d in range(2):
        w_raw = (lp["rwkv_w0"][d] + lw_d[d] @ lp["rwkv_w2"][d]).astype(f32)
        decay = heads(jnp.exp(-jnp.exp(-jax.nn.softplus(-w_raw) - 0.5)))
        a = jax.nn.sigmoid((lp["rwkv_a0"][d] + la_d[d] @ lp["rwkv_a2"][d]).astype(f32))
        k_d = heads(k * (1 + (a - 1) * lp["rwkv_ka"]))
        s0 = (jnp.zeros((B, RWKV_HEADS, RWKV_HEAD_DIM, RWKV_HEAD_DIM), f32)
              if init is None else init[:, d].astype(f32))
        y_d, s_d = rwkv_scan(r_h, decay, k_d, v_h, kk, heads(a), s0, reverse=(d == 1))
        ys.append(y_d)
        bonuses.append(jnp.sum(r_h * k_d * lp["rwkv_rk"].astype(f32), axis=-1, keepdims=True) * v_h)
        finals.append(s_d)
    y = ys[0] + ys[1]
    mu = jnp.mean(y, axis=-1, keepdims=True)
    var = jnp.mean(jnp.square(y - mu), axis=-1, keepdims=True)
    y = (y - mu) * lax.rsqrt(var + RWKV_GN_EPS)
    y = (y.reshape(B, T, RWKV_WIDTH) * lp["rwkv_ln_g"] + lp["rwkv_ln_b"]
         + (bonuses[0] + bonuses[1]).reshape(B, T, RWKV_WIDTH))
    g = jax.nn.sigmoid(lg) @ lp["rwkv_g2"]
    out = (y * g).astype(r.dtype)
    return out, (jnp.stack(finals, axis=1) if init is None else None)


def gla_chunked(q, k, v, log_a, s0):
    B, T, H, DK = q.shape
    DV = v.shape[-1]
    n = T // GLA_CHUNK

    def chunks(t):
        return t.reshape(B, n, GLA_CHUNK, H, t.shape[-1]).transpose(1, 0, 3, 2, 4)

    q, k, v, log_a = chunks(q), chunks(k), chunks(v), chunks(log_a)
    b = jnp.cumsum(log_a, axis=-2)
    b_last = b[..., -1, :]
    q_in = q * jnp.exp(b)
    k_in = k * jnp.exp(-b)
    k_end = k * jnp.exp(b_last[..., None, :] - b)
    lower = jnp.tril(jnp.ones((GLA_CHUNK, GLA_CHUNK), dtype=bool))

    def step(S, inp):
        qc, kc, ke, vc, bl = inp
        att = jnp.where(lower, jnp.einsum("bhcd,bhsd->bhcs", qc, kc), 0.0)
        o = jnp.einsum("bhcs,bhsv->bhcv", att, vc) + jnp.einsum("bhcd,bhdv->bhcv", qc, S)
        S = S * jnp.exp(bl)[..., None] + jnp.einsum("bhsd,bhsv->bhdv", ke, vc)
        return S, o

    s, o = lax.scan(step, s0, (q_in, k_in, k_end, v, b_last))
    return o.transpose(1, 0, 3, 2, 4).reshape(B, T, H, DV), s


def gla_mixer(q, k, v, la, rg, lp, init):
    B, T, _ = q.shape
    f32 = jnp.float32
    qh = q.astype(f32).reshape(B, T, GLA_HEADS, GLA_DK) * (GLA_DK ** -0.5)
    kh = k.astype(f32).reshape(B, T, GLA_HEADS, GLA_DK)
    vh = v.astype(f32).reshape(B, T, GLA_HEADS, GLA_DV)
    la_d = jnp.split(la, 2, axis=-1)
    outs, finals = [], []
    for d in range(2):
        log_a = jax.nn.log_sigmoid((la_d[d] @ lp["gla_a2"][d] + lp["gla_ab"][d]).astype(f32)) / GLA_GATE_NORMALIZER
        log_a = log_a.reshape(B, T, GLA_HEADS, GLA_DK)
        s0 = (jnp.zeros((B, GLA_HEADS, GLA_DK, GLA_DV), f32)
              if init is None else init[:, d].astype(f32))
        if d == 0:
            o, s = gla_chunked(qh, kh, vh, log_a, s0)
        else:
            o, s = gla_chunked(jnp.flip(qh, 1), jnp.flip(kh, 1), jnp.flip(vh, 1), jnp.flip(log_a, 1), s0)
            o = jnp.flip(o, 1)
        outs.append(o)
        finals.append(s)
    o = outs[0] + outs[1]
    o = o * lax.rsqrt(jnp.mean(o * o, axis=-1, keepdims=True) + NORM_EPS) * lp["gla_norm"].astype(f32)
    out = o.reshape(B, T, GLA_V_WIDTH) * jax.nn.silu(rg.astype(f32))
    return out.astype(q.dtype), (jnp.stack(finals, axis=1) if init is None else None)


def moe_ffn(h, router_w, router_b, w1, b1, w2, b2):
    B, T, D = h.shape
    xf = h.reshape(B * T, D)
    n_tok = B * T
    logits = (xf @ router_w + router_b).astype(jnp.float32)
    top_v, top_i = lax.top_k(logits, TOP_K)
    gates = jax.nn.softmax(top_v, axis=-1)
    n_as = n_tok * TOP_K
    n_blocks = -(-n_as // EXPERT_BLOCK) + N_EXPERTS
    flat_e = top_i.reshape(n_as)
    order = jnp.argsort(flat_e)
    e_sorted = flat_e[order]
    counts = jnp.bincount(flat_e, length=N_EXPERTS)
    starts = jnp.cumsum(counts) - counts
    padded = (counts + EXPERT_BLOCK - 1) // EXPERT_BLOCK * EXPERT_BLOCK
    pends = jnp.cumsum(padded)
    pstarts = pends - padded
    dest = pstarts[e_sorted] + jnp.arange(n_as) - starts[e_sorted]
    row_tok = jnp.zeros((n_blocks * EXPERT_BLOCK,), jnp.int32).at[dest].set((order // TOP_K).astype(jnp.int32))
    row_gate = jnp.zeros((n_blocks * EXPERT_BLOCK,), jnp.float32).at[dest].set(gates.reshape(n_as)[order])
    block_exp = jnp.minimum(jnp.searchsorted(pends, jnp.arange(n_blocks) * EXPERT_BLOCK, side="right"),
                            N_EXPERTS - 1)

    def run_block(inp):
        tok, gate, e = inp
        z = xf[tok] @ w1[e] + b1[e]
        glu = jnp.minimum(z[:, :D_EXPERT], SWIGLU_LIMIT)
        lin = jnp.clip(z[:, D_EXPERT:], -SWIGLU_LIMIT, SWIGLU_LIMIT)
        act = glu * jax.nn.sigmoid(SWIGLU_ALPHA * glu) * (lin + 1)
        return (act @ w2[e] + b2[e]) * gate[:, None].astype(act.dtype)

    out = lax.map(run_block, (row_tok.reshape(n_blocks, EXPERT_BLOCK),
                              row_gate.reshape(n_blocks, EXPERT_BLOCK), block_exp))
    y = jnp.zeros_like(xf).at[row_tok].add(out.reshape(-1, D).astype(xf.dtype))
    return y.reshape(B, T, D)


def trunk_layer(x, cond, lp, ctx):
    sh1, sc1, g1, sh2, sc2, g2 = adaln(cond, lp["ada_w"], lp["ada_b"])
    h = rmsnorm(x, lp["norm_mix"]) * (1 + sc1) + sh1
    z = h @ lp["w_in"]
    zr = z[..., :RWKV_COLS]
    zr = zr + lp["rwkv_mu"] * (centred_shift(zr) - zr)
    r, k, v, lw, la, lg = split_cols(zr, RWKV_SPLITS)
    aq, ak, av, gq, gk, gv, gl, gr, mr, ma, mg = split_cols(z[..., RWKV_COLS:], REST_SPLITS)
    o_r, s_r = rwkv_mixer(r, k, v, lw, la, lg, lp, None if ctx is None else ctx["rwkv"])
    o_a, kh, vh = attn_mixer(aq, ak, av, lp, None if ctx is None else (ctx["k"], ctx["v"]))
    o_g, s_g = gla_mixer(gq, gk, gv, gl, gr, lp, None if ctx is None else ctx["gla"])
    merged = (jax.nn.sigmoid(mr) * (o_r @ lp["p_rwkv"])
              + jax.nn.sigmoid(ma) * (o_a @ lp["p_attn"])
              + jax.nn.sigmoid(mg) * (o_g @ lp["p_gla"]))
    x = x + g1 * (merged @ lp["w_out"])
    h2 = rmsnorm(x, lp["norm_ffn"]) * (1 + sc2) + sh2
    x = x + g2 * moe_ffn(h2, lp["router_w"], lp["router_b"], lp["moe_w1"], lp["moe_b1"],
                         lp["moe_w2"], lp["moe_b2"])
    return x, (kh, vh, s_r, s_g)


def setup_inputs(seed: int = 0) -> dict:
    key = jax.random.key(seed)
    keys = iter(jax.random.split(key, 48))

    def nrm(shape, scale=1.0, offset=0.0):
        return offset + scale * jax.random.normal(next(keys), shape, jnp.float32)

    L, D, E = DEPTH, D_MODEL, N_EXPERTS
    return {
        "x_prompt": nrm((BATCH, SEQ, D)),
        "x_sample": nrm((DEC_BATCH, DEC_SEQ, D)),
        "c": nrm((DEC_BATCH, D)),
        "cache_k": nrm((DEC_BATCH, L, PAST_LEN, KV_HEADS, HEAD_DIM)),
        "cache_v": nrm((DEC_BATCH, L, PAST_LEN, KV_HEADS, HEAD_DIM)),
        "state_rwkv": nrm((DEC_BATCH, L, 2, RWKV_HEADS, RWKV_HEAD_DIM, RWKV_HEAD_DIM), 0.3),
        "state_gla": nrm((DEC_BATCH, L, 2, GLA_HEADS, GLA_DK, GLA_DV), 0.3),
        "c_ctx": nrm((D,)),
        "ada_w": nrm((L, D, 6 * D), 0.5 * D ** -0.5),
        "ada_b": nrm((L, 6 * D), 0.02),
        "norm_mix": nrm((L, D), 0.02, 1.0),
        "norm_ffn": nrm((L, D), 0.02, 1.0),
        "w_in": nrm((L, D, IN_COLS), D ** -0.5),
        "rwkv_mu": nrm((L, RWKV_COLS), 0.1, 0.5),
        "rwkv_w0": nrm((L, 2, RWKV_WIDTH), 1.0, -2.0),
        "rwkv_w2": nrm((L, 2, DECAY_LORA, RWKV_WIDTH), DECAY_LORA ** -0.5),
        "rwkv_a0": nrm((L, 2, RWKV_WIDTH), 0.3),
        "rwkv_a2": nrm((L, 2, ICLR_LORA, RWKV_WIDTH), 0.5 * ICLR_LORA ** -0.5),
        "rwkv_g2": nrm((L, GATE_LORA, RWKV_WIDTH), GATE_LORA ** -0.5),
        "rwkv_kk": nrm((L, RWKV_WIDTH), 0.05, 0.85),
        "rwkv_ka": nrm((L, RWKV_WIDTH), 0.05, 1.0),
        "rwkv_rk": nrm((L, RWKV_HEADS, RWKV_HEAD_DIM), 0.1),
        "rwkv_ln_g": nrm((L, RWKV_WIDTH), 0.02, 1.0),
        "rwkv_ln_b": nrm((L, RWKV_WIDTH), 0.02),
        "attn_qn": nrm((L, HEAD_DIM), 0.02, 1.0),
        "attn_kn": nrm((L, HEAD_DIM), 0.02, 1.0),
        "gla_a2": nrm((L, 2, GLA_LORA, GLA_K_WIDTH), GLA_LORA ** -0.5),
        "gla_ab": nrm((L, 2, GLA_K_WIDTH), 0.5, 1.0),
        "gla_norm": nrm((L, GLA_DV), 0.02, 1.0),
        "p_rwkv": nrm((L, RWKV_WIDTH, D), RWKV_WIDTH ** -0.5),
        "p_attn": nrm((L, ATT_WIDTH, D), ATT_WIDTH ** -0.5),
        "p_gla": nrm((L, GLA_V_WIDTH, D), GLA_V_WIDTH ** -0.5),
        "w_out": nrm((L, D, D), D ** -0.5),
        "router_w": nrm((L, D, E), D ** -0.5),
        "router_b": nrm((L, E), 0.01),
        "moe_w1": nrm((L, E, D, 2 * D_EXPERT), D ** -0.5),
        "moe_b1": nrm((L, E, 2 * D_EXPERT), 0.01),
        "moe_w2": nrm((L, E, D_EXPERT, D), D_EXPERT ** -0.5),
        "moe_b2": nrm((L, E, D), 0.01),
    }


def reference(x_prompt, x_sample, c, cache_k, cache_v, state_rwkv, state_gla, c_ctx,
              ada_w, ada_b, norm_mix, norm_ffn, w_in, rwkv_mu, rwkv_w0, rwkv_w2, rwkv_a0, rwkv_a2,
              rwkv_g2, rwkv_kk, rwkv_ka, rwkv_rk, rwkv_ln_g, rwkv_ln_b, attn_qn, attn_kn,
              gla_a2, gla_ab, gla_norm, p_rwkv, p_attn, p_gla, w_out, router_w, router_b,
              moe_w1, moe_b1, moe_w2, moe_b2):
    layer_params = {
        "ada_w": ada_w, "ada_b": ada_b, "norm_mix": norm_mix, "norm_ffn": norm_ffn, "w_in": w_in,
        "rwkv_mu": rwkv_mu, "rwkv_w0": rwkv_w0, "rwkv_w2": rwkv_w2, "rwkv_a0": rwkv_a0,
        "rwkv_a2": rwkv_a2, "rwkv_g2": rwkv_g2, "rwkv_kk": rwkv_kk, "rwkv_ka": rwkv_ka,
        "rwkv_rk": rwkv_rk, "rwkv_ln_g": rwkv_ln_g, "rwkv_ln_b": rwkv_ln_b,
        "attn_qn": attn_qn, "attn_kn": attn_kn, "gla_a2": gla_a2, "gla_ab": gla_ab,
        "gla_norm": gla_norm, "p_rwkv": p_rwkv, "p_attn": p_attn, "p_gla": p_gla, "w_out": w_out,
        "router_w": router_w, "router_b": router_b, "moe_w1": moe_w1, "moe_b1": moe_b1,
        "moe_w2": moe_w2, "moe_b2": moe_b2,
    }

    y_prompt = x_prompt
    ks, vs, srs, sgs = [], [], [], []
    for l in range(DEPTH):
        lp = {name: arr[l] for name, arr in layer_params.items()}
        y_prompt, (kh, vh, s_r, s_g) = trunk_layer(y_prompt, c_ctx, lp, None)
        ks.append(kh)
        vs.append(vh)
        srs.append(s_r)
        sgs.append(s_g)

    y_sample = x_sample
    for l in range(DEPTH):
        lp = {name: arr[l] for name, arr in layer_params.items()}
        ctx = {"k": cache_k[:, l], "v": cache_v[:, l], "rwkv": state_rwkv[:, l], "gla": state_gla[:, l]}
        y_sample, _ = trunk_layer(y_sample, c, lp, ctx)

    dt = x_prompt.dtype
    new_cache_k = jnp.stack(ks, axis=1).astype(dt)
    new_cache_v = jnp.stack(vs, axis=1).astype(dt)
    new_state_rwkv = jnp.stack(srs, axis=1).astype(dt)
    new_state_gla = jnp.stack(sgs, axis=1).astype(dt)
    return (y_prompt, y_sample, new_cache_k, new_cache_v, new_state_rwkv, new_state_gla)
```

```python
import functools

import jax
import jax.numpy as jnp
from jax import lax
from jax.experimental import pallas as pl
from jax.experimental.pallas import tpu as pltpu

D_MODEL = 1024
DEPTH = 2
GRID_W = 64
NORM_EPS = 1e-6

RWKV_HEADS = 8
RWKV_HEAD_DIM = 64
RWKV_WIDTH = RWKV_HEADS * RWKV_HEAD_DIM
DECAY_LORA = 64
ICLR_LORA = 64
GATE_LORA = 128
RWKV_GN_EPS = 6.4e-4
RWKV_COLS = 3 * RWKV_WIDTH + 2 * DECAY_LORA + 2 * ICLR_LORA + GATE_LORA

N_HEADS = 8
KV_HEADS = 2
HEAD_DIM = 64
ATT_WIDTH = N_HEADS * HEAD_DIM
KV_WIDTH = KV_HEADS * HEAD_DIM
ROPE_THETA = 10000.0

GLA_HEADS = 4
GLA_DK = 128
GLA_DV = 256
GLA_K_WIDTH = GLA_HEADS * GLA_DK
GLA_V_WIDTH = GLA_HEADS * GLA_DV
GLA_LORA = 16
GLA_GATE_NORMALIZER = 16.0

N_EXPERTS = 32
TOP_K = 4
D_EXPERT = 1024
SWIGLU_ALPHA = 1.702
SWIGLU_LIMIT = 7.0
EXPERT_BLOCK = 256

LANES = 128
CHUNK = 64
ROW_TILE = 256
Q_TILE = 128
VMEM_LIMIT = 52 * 1024 * 1024

F32 = jnp.float32
BF16 = jnp.bfloat16
HI = lax.Precision.HIGHEST


def _dot(a, b, prec=None):
    return jnp.dot(a, b, preferred_element_type=F32, precision=prec)


def _dot_nt(a, b, prec=None):
    return lax.dot_general(a, b, (((1,), (1,)), ((), ())), preferred_element_type=F32, precision=prec)


def _dot_tn(a, b, prec=None):
    return lax.dot_general(a, b, (((0,), (0,)), ((), ())), preferred_element_type=F32, precision=prec)


def _seg_sum(x, seg_bf16):
    xh = x.astype(BF16)
    xl = (x - xh.astype(F32)).astype(BF16)
    return _dot(xh, seg_bf16) + _dot(xl, seg_bf16)


def _sigmoid(x):
    return 1.0 / (1.0 + jnp.exp(-x))


def _params(sem):
    return pltpu.CompilerParams(dimension_semantics=sem, vmem_limit_bytes=VMEM_LIMIT)


def _const_spec(shape):
    nd = len(shape)
    return pl.BlockSpec(shape, lambda *_: (0,) * nd)


def _ada_kernel(c_ref, w_ref, b_ref, o_ref):
    c = c_ref[...]
    s = c * _sigmoid(c)
    o_ref[...] = _dot(s, w_ref[...], HI) + b_ref[...]


def _ada_call(cond8, ada_w, ada_b):
    L, D, W = ada_w.shape
    tn = 1536
    return pl.pallas_call(
        _ada_kernel,
        grid=(L, W // tn),
        in_specs=[pl.BlockSpec((8, D), lambda l, j: (0, 0)),
                  pl.BlockSpec((None, D, tn), lambda l, j: (l, 0, j)),
                  pl.BlockSpec((None, 1, tn), lambda l, j: (l, 0, j))],
        out_specs=pl.BlockSpec((None, 8, tn), lambda l, j: (l, 0, j)),
        out_shape=jax.ShapeDtypeStruct((L, 8, W), F32),
        compiler_params=_params(("parallel", "parallel")),
        name="ada",
    )(cond8, ada_w, ada_b.reshape(L, 1, W))


def _inproj_kernel(x_ref, m_ref, g_ref, w_ref, o_ref, h_scr):
    @pl.when(pl.program_id(1) == 0)
    def _():
        x = x_ref[...]
        y = x * lax.rsqrt(jnp.mean(x * x, axis=-1, keepdims=True) + NORM_EPS) * g_ref[...]
        h = y * (1.0 + m_ref[1:2, :]) + m_ref[0:1, :]
        h_scr[...] = h.astype(BF16)

    o_ref[...] = _dot(h_scr[...], w_ref[...])


def _inproj_call(x, mod, gain, w, rows_per_cond, tn):
    N, D = x.shape
    W = w.shape[1]
    tm = ROW_TILE
    return pl.pallas_call(
        _inproj_kernel,
        grid=(N // tm, W // tn),
        in_specs=[pl.BlockSpec((tm, D), lambda i, j: (i, 0)),
                  pl.BlockSpec((None, 6, D), lambda i, j: ((i * tm) // rows_per_cond, 0, 0)),
                  pl.BlockSpec((1, D), lambda i, j: (0, 0)),
                  pl.BlockSpec((D, tn), lambda i, j: (0, j))],
        out_specs=pl.BlockSpec((tm, tn), lambda i, j: (i, j)),
        out_shape=jax.ShapeDtypeStruct((N, W), F32),
        scratch_shapes=[pltpu.VMEM((tm, D), BF16)],
        compiler_params=_params(("parallel", "arbitrary")),
        name="inproj",
    )(x, mod, gain, w)


def _rwkv_prep_kernel(z_ref, zp_ref, zn_ref, mu_ref, kkp_ref, ka_ref, rk_ref, w0_ref, a0_ref,
                      w2_ref, a2_ref, g2_ref, seg_ref,
                      r_o, v_o, kk_o, g_o, bon_o, lw0_o, lw1_o, k0_o, k1_o, b0_o, b1_o,
                      *, tm, blocks_per_seq):
    i = pl.program_id(0)
    z = z_ref[...]
    row = lax.broadcasted_iota(jnp.int32, (tm, 1), 0)
    first = (i % blocks_per_seq) == 0
    last = (i % blocks_per_seq) == blocks_per_seq - 1
    pz = jnp.where(first, 0.0, zp_ref[7:8, :])
    nz = jnp.where(last, 0.0, zn_ref[0:1, :])
    prev = jnp.where(row == 0, pz, pltpu.roll(z, 1, 0))
    nxt = jnp.where(row == tm - 1, nz, pltpu.roll(z, tm - 1, 0))
    zm = z + mu_ref[...] * (0.5 * (prev + nxt) - z)

    Wd = RWKV_WIDTH
    r = zm[:, 0:Wd]
    k = zm[:, Wd:2 * Wd]
    v = zm[:, 2 * Wd:3 * Wd]
    lw = zm[:, 3 * Wd:3 * Wd + 128]
    la = zm[:, 3 * Wd + 128:3 * Wd + 256]
    lg = zm[:, 3 * Wd + 256:3 * Wd + 384]
    seg = seg_ref[...]

    kk = k * kkp_ref[...]
    kk = kk / jnp.maximum(jnp.sqrt(_seg_sum(kk * kk, seg)), 1e-12)
    w_raw = _dot(jnp.tanh(lw), w2_ref[...], HI) + w0_ref[...]
    logw = -_sigmoid(w_raw) * 0.6065306597126334
    a = _sigmoid(_dot(la, a2_ref[...], HI) + a0_ref[...])
    ka = ka_ref[...]
    rk = rk_ref[...]
    bonus = jnp.zeros_like(r)
    for d, (lw_o, k_o, b_o) in enumerate(((lw0_o, k0_o, b0_o), (lw1_o, k1_o, b1_o))):
        a_d = a[:, d * Wd:(d + 1) * Wd]
        k_d = k * (1.0 + (a_d - 1.0) * ka)
        lw_o[...] = logw[:, d * Wd:(d + 1) * Wd]
        k_o[...] = k_d
        b_o[...] = kk * a_d
        bonus = bonus + _seg_sum(r * k_d * rk, seg) * v
    r_o[...] = r
    v_o[...] = v
    kk_o[...] = kk
    bon_o[...] = bonus
    g_o[...] = _dot(_sigmoid(lg), g2_ref[...], HI)


def _rwkv_prep_call(zr, seq_len, lw):
    N, Wz = zr.shape
    tm = ROW_TILE
    bps = seq_len // tm
    Wd = RWKV_WIDTH
    nb8 = N // 8
    row_spec = pl.BlockSpec((tm, Wd), lambda i: (i, 0))
    out = jax.ShapeDtypeStruct((N, Wd), F32)
    return pl.pallas_call(
        functools.partial(_rwkv_prep_kernel, tm=tm, blocks_per_seq=bps),
        grid=(N // tm,),
        in_specs=[pl.BlockSpec((tm, Wz), lambda i: (i, 0)),
                  pl.BlockSpec((8, Wz), lambda i: (jnp.maximum(i * (tm // 8) - 1, 0), 0)),
                  pl.BlockSpec((8, Wz), lambda i: (jnp.minimum((i + 1) * (tm // 8), nb8 - 1), 0)),
                  _const_spec((1, Wz)), _const_spec((1, Wd)), _const_spec((1, Wd)), _const_spec((1, Wd)),
                  _const_spec((1, 2 * Wd)), _const_spec((1, 2 * Wd)),
                  _const_spec((128, 2 * Wd)), _const_spec((128, 2 * Wd)), _const_spec((128, Wd)),
                  _const_spec((Wd, Wd))],
        out_specs=[row_spec] * 11,
        out_shape=[out] * 11,
        compiler_params=_params(("parallel",)),
        name="rwkv_prep",
    )(zr, zr, zr, lw["mu"], lw["kk"], lw["ka"], lw["rk"], lw["w0"], lw["a0"],
      lw["w2"], lw["a2"], lw["g2"], lw["seg64"])


def _tri_masks(d):
    ti = lax.broadcasted_iota(jnp.int32, (CHUNK, CHUNK), 0)
    si = lax.broadcasted_iota(jnp.int32, (CHUNK, CHUNK), 1)
    if d == 0:
        return si <= ti, si < ti
    return si >= ti, si > ti


def _rwkv_scan_kernel(rf, vf, kkf, lwf, kf, bf, rb, vb, kkb, lwb, kb, bb, s0_ref,
                      yf_o, yb_o, st_o, s_scr, *, nc):
    i = pl.program_id(1)

    @pl.when(i == 0)
    def _():
        s_scr[...] = s0_ref[...]

    eye = (lax.broadcasted_iota(jnp.int32, (CHUNK, CHUNK), 0)
           == lax.broadcasted_iota(jnp.int32, (CHUNK, CHUNK), 1)).astype(F32)
    H, Dh = RWKV_HEADS, RWKV_HEAD_DIM
    dirs = ((rf, vf, kkf, lwf, kf, bf, yf_o), (rb, vb, kkb, lwb, kb, bb, yb_o))
    for d, (r_ref, v_ref, kk_ref, lw_ref, k_ref, b_ref, y_o) in enumerate(dirs):
        incl, strict = _tri_masks(d)
        lw = lw_ref[...]
        b = _dot(incl.astype(F32), lw, HI)
        btot = jnp.sum(lw, axis=0, keepdims=True)
        r, v, kk, k, beta = r_ref[...], v_ref[...], kk_ref[...], k_ref[...], b_ref[...]
        at = -kk * jnp.exp(b - lw)
        rt = r * jnp.exp(b)
        nb = jnp.exp(-b)
        bbar = beta * nb
        kbar = k * nb
        eb = jnp.exp(btot - b)
        bhat = beta * eb
        khat = k * eb
        gtot = jnp.exp(btot)
        for h in range(H):
            sl = slice(h * Dh, (h + 1) * Dh)
            at_h, rt_h, v_h = at[:, sl], rt[:, sl], v[:, sl]
            a_ab = jnp.where(strict, _dot_nt(at_h, bbar[:, sl], HI), 0.0)
            a_ak = jnp.where(strict, _dot_nt(at_h, kbar[:, sl], HI), 0.0)
            a_rb = jnp.where(incl, _dot_nt(rt_h, bbar[:, sl], HI), 0.0)
            a_rk = jnp.where(incl, _dot_nt(rt_h, kbar[:, sl], HI), 0.0)
            p = a_ab
            t = eye + a_ab
            for _ in range(5):
                p = _dot(p, p, HI)
                t = t + _dot(t, p, HI)
            w = _dot(t, at_h, HI)
            u0 = _dot(t, _dot(a_ak, v_h, HI), HI)
            s = s_scr[d, h]
            u = _dot_nt(w, s, HI) + u0
            y = _dot_nt(rt_h, s, HI) + _dot(a_rb, u, HI) + _dot(a_rk, v_h, HI)
            s_scr[d, h] = s * gtot[:, sl] + _dot_tn(u, bhat[:, sl], HI) + _dot_tn(v_h, khat[:, sl], HI)
            y_o[:, sl] = y

    @pl.when(i == nc - 1)
    def _():
        st_o[...] = s_scr[...]


def _rwkv_scan_call(prep, s0, B, T):
    r, v, kk, lw0, lw1, k0, k1, b0, b1 = prep
    N, Wd = r.shape
    nc = T // CHUNK
    fwd = pl.BlockSpec((CHUNK, Wd), lambda b, i: (b * nc + i, 0))
    bwd = pl.BlockSpec((CHUNK, Wd), lambda b, i: (b * nc + nc - 1 - i, 0))
    st = pl.BlockSpec((None, 2, RWKV_HEADS, RWKV_HEAD_DIM, RWKV_HEAD_DIM), lambda b, i: (b, 0, 0, 0, 0))
    y = jax.ShapeDtypeStruct((N, Wd), F32)
    return pl.pallas_call(
        functools.partial(_rwkv_scan_kernel, nc=nc),
        grid=(B, nc),
        in_specs=[fwd] * 6 + [bwd] * 6 + [st],
        out_specs=[fwd, bwd, st],
        out_shape=[y, y, jax.ShapeDtypeStruct(s0.shape, F32)],
        scratch_shapes=[pltpu.VMEM((2, RWKV_HEADS, RWKV_HEAD_DIM, RWKV_HEAD_DIM), F32)],
        compiler_params=_params(("parallel", "arbitrary")),
        name="rwkv_scan",
    )(r, v, kk, lw0, k0, b0, r, v, kk, lw1, k1, b1, s0)


def _gla_scan_kernel(xf, glf, xb, glb, a2_ref, ab_ref, s0_ref, of_o, ob_o, st_o, s_scr, *, nc):
    i = pl.program_id(1)

    @pl.when(i == 0)
    def _():
        s_scr[...] = s0_ref[...]

    Kw = GLA_K_WIDTH
    for d, (x_ref, gl_ref, o_o) in enumerate(((xf, glf, of_o), (xb, glb, ob_o))):
        incl, _ = _tri_masks(d)
        xa = _dot(gl_ref[...], a2_ref[:, d * Kw:(d + 1) * Kw], HI) + ab_ref[:, d * Kw:(d + 1) * Kw]
        log_a = (jnp.minimum(xa, 0.0) - jnp.log1p(jnp.exp(-jnp.abs(xa)))) * (1.0 / GLA_GATE_NORMALIZER)
        b = _dot(incl.astype(F32), log_a, HI)
        bl = jnp.sum(log_a, axis=0, keepdims=True)
        q = x_ref[:, 0:Kw] * (GLA_DK ** -0.5)
        k = x_ref[:, Kw:2 * Kw]
        q_in = q * jnp.exp(b)
        k_in = k * jnp.exp(-b)
        k_end = k * jnp.exp(bl - b)
        gtot = jnp.exp(bl)
        for h in range(GLA_HEADS):
            ks = slice(h * GLA_DK, (h + 1) * GLA_DK)
            v_h = x_ref[:, 2 * Kw + h * GLA_DV:2 * Kw + (h + 1) * GLA_DV]
            att = jnp.where(incl, _dot_nt(q_in[:, ks], k_in[:, ks], HI), 0.0)
            s = s_scr[d, h]
            o_o[:, h * GLA_DV:(h + 1) * GLA_DV] = _dot(att, v_h, HI) + _dot_nt(q_in[:, ks], s, HI)
            s_scr[d, h] = s * gtot[:, ks] + _dot_tn(v_h, k_end[:, ks], HI)

    @pl.when(i == nc - 1)
    def _():
        st_o[...] = s_scr[...]


def _gla_scan_call(zg, s0t, a2, ab, B, T):
    N = zg.shape[0]
    nc = T // CHUNK
    Wx = 2 * GLA_K_WIDTH + GLA_V_WIDTH
    gl_col = (Wx + GLA_V_WIDTH) // LANES
    xf = pl.BlockSpec((CHUNK, Wx), lambda b, i: (b * nc + i, 0))
    xb = pl.BlockSpec((CHUNK, Wx), lambda b, i: (b * nc + nc - 1 - i, 0))
    gf = pl.BlockSpec((CHUNK, LANES), lambda b, i: (b * nc + i, gl_col))
    gb = pl.BlockSpec((CHUNK, LANES), lambda b, i: (b * nc + nc - 1 - i, gl_col))
    of = pl.BlockSpec((CHUNK, GLA_V_WIDTH), lambda b, i: (b * nc + i, 0))
    ob = pl.BlockSpec((CHUNK, GLA_V_WIDTH), lambda b, i: (b * nc + nc - 1 - i, 0))
    st = pl.BlockSpec((None, 2, GLA_HEADS, GLA_DV, GLA_DK), lambda b, i: (b, 0, 0, 0, 0))
    o = jax.ShapeDtypeStruct((N, GLA_V_WIDTH), F32)
    return pl.pallas_call(
        functools.partial(_gla_scan_kernel, nc=nc),
        grid=(B, nc),
        in_specs=[xf, gf, xb, gb, _const_spec((LANES, 2 * GLA_K_WIDTH)), _const_spec((1, 2 * GLA_K_WIDTH)), st],
        out_specs=[of, ob, st],
        out_shape=[o, o, jax.ShapeDtypeStruct(s0t.shape, F32)],
        scratch_shapes=[pltpu.VMEM((2, GLA_HEADS, GLA_DV, GLA_DK), F32)],
        compiler_params=_params(("parallel", "arbitrary")),
        name="gla_scan",
    )(zg, zg, zg, zg, a2, ab, s0t)


def _rope(x, cos, sin):
    n = x.shape[1]
    lane = lax.broadcasted_iota(jnp.int32, x.shape, 1)
    up = pltpu.roll(x, n - 16, 1)
    dn = pltpu.roll(x, 16, 1)
    sw = jnp.where((lane % 32) < 16, up, dn)
    reps = n // LANES
    if reps > 1:
        cos = jnp.concatenate([cos] * reps, axis=1)
        sin = jnp.concatenate([sin] * reps, axis=1)
    return x * cos + sw * sin


def _attn_prep_kernel(*refs, rope):
    if rope:
        z_ref, qn_ref, kn_ref, segq_ref, segk_ref, cos_ref, sin_ref, q_o, kn_o, kr_o = refs
    else:
        z_ref, qn_ref, kn_ref, segq_ref, segk_ref, q_o, kn_o, kr_o = refs
    zq = z_ref[:, 0:ATT_WIDTH]
    zk = z_ref[:, ATT_WIDTH:ATT_WIDTH + KV_WIDTH]
    qh = zq * lax.rsqrt(_seg_sum(zq * zq, segq_ref[...]) * (1.0 / HEAD_DIM) + NORM_EPS) * qn_ref[...]
    kh = zk * lax.rsqrt(_seg_sum(zk * zk, segk_ref[...]) * (1.0 / HEAD_DIM) + NORM_EPS) * kn_ref[...]
    kn_o[...] = kh
    if rope:
        qh = _rope(qh, cos_ref[...], sin_ref[...])
        kh = _rope(kh, cos_ref[...], sin_ref[...])
    q_o[...] = (qh * (HEAD_DIM ** -0.5)).astype(BF16)
    kr_o[...] = kh.astype(BF16)


def _attn_prep_call(za, lw, seq_len, rope_tabs):
    N, Wz = za.shape
    tm = ROW_TILE
    bps = seq_len // tm
    rope = rope_tabs is not None
    in_specs = [pl.BlockSpec((tm, Wz), lambda i: (i, 0)),
                _const_spec((1, ATT_WIDTH)), _const_spec((1, KV_WIDTH)),
                _const_spec((ATT_WIDTH, ATT_WIDTH)), _const_spec((KV_WIDTH, KV_WIDTH))]
    args = [za, lw["qn"], lw["kn"], lw["seg64"], lw["seg64"][:KV_WIDTH, :KV_WIDTH]]
    if rope:
        in_specs += [pl.BlockSpec((tm, LANES), lambda i: (i % bps, 0))] * 2
        args += list(rope_tabs)
    return pl.pallas_call(
        functools.partial(_attn_prep_kernel, rope=rope),
        grid=(N // tm,),
        in_specs=in_specs,
        out_specs=[pl.BlockSpec((tm, ATT_WIDTH), lambda i: (i, 0)),
                   pl.BlockSpec((tm, KV_WIDTH), lambda i: (i, 0)),
                   pl.BlockSpec((tm, KV_WIDTH), lambda i: (i, 0))],
        out_shape=[jax.ShapeDtypeStruct((N, ATT_WIDTH), BF16),
                   jax.ShapeDtypeStruct((N, KV_WIDTH), F32),
                   jax.ShapeDtypeStruct((N, KV_WIDTH), BF16)],
        compiler_params=_params(("parallel",)),
        name="attn_prep",
    )(*args)


def _attn_kernel(*refs, cached):
    if cached:
        q_ref, k_ref, v_ref, ck_ref, cv_ref, o_ref = refs
    else:
        q_ref, k_ref, v_ref, o_ref = refs
    G = N_HEADS // KV_HEADS
    for g in range(KV_HEADS):
        gs = slice(g * HEAD_DIM, (g + 1) * HEAD_DIM)
        kg = k_ref[:, gs]
        vg = v_ref[:, gs].astype(BF16)
        if cached:
            ckg = ck_ref[:, gs].astype(BF16)
            cvg = cv_ref[:, gs].astype(BF16)
        for hh in range(G):
            hs = slice((g * G + hh) * HEAD_DIM, (g * G + hh + 1) * HEAD_DIM)
            qh = q_ref[:, hs]
            s1 = _dot_nt(qh, kg)
            m = jnp.max(s1, axis=-1, keepdims=True)
            if cached:
                s2 = _dot_nt(qh, ckg)
                m = jnp.maximum(m, jnp.max(s2, axis=-1, keepdims=True))
            p1 = jnp.exp(s1 - m)
            l = jnp.sum(p1, axis=-1, keepdims=True)
            o = _dot(p1.astype(BF16), vg)
            if cached:
                p2 = jnp.exp(s2 - m)
                l = l + jnp.sum(p2, axis=-1, keepdims=True)
                o = o + _dot(p2.astype(BF16), cvg)
            o_ref[:, hs] = o / l


def _attn_call(q, k, za, cache, B, T):
    N = q.shape[0]
    tq = Q_TILE
    nq = T // tq
    v_col = (ATT_WIDTH + KV_WIDTH) // KV_WIDTH
    in_specs = [pl.BlockSpec((tq, ATT_WIDTH), lambda b, i: (b * nq + i, 0)),
                pl.BlockSpec((T, KV_WIDTH), lambda b, i: (b, 0)),
                pl.BlockSpec((T, KV_WIDTH), lambda b, i: (b, v_col))]
    args = [q, k, za]
    if cache is not None:
        ck, cv, layer = cache
        P = ck.shape[2]
        cspec = pl.BlockSpec((None, None, P, KV_WIDTH), lambda b, i: (b, layer, 0, 0))
        in_specs += [cspec, cspec]
        args += [ck, cv]
    return pl.pallas_call(
        functools.partial(_attn_kernel, cached=cache is not None),
        grid=(B, nq),
        in_specs=in_specs,
        out_specs=pl.BlockSpec((tq, ATT_WIDTH), lambda b, i: (b * nq + i, 0)),
        out_shape=jax.ShapeDtypeStruct((N, ATT_WIDTH), F32),
        compiler_params=_params(("parallel", "arbitrary")),
        name="attention",
    )(*args)


def _merge_kernel(yf, yb, bon, g, oa, gf, gb, gr, mr, ma, mg, x_ref, m_ref,
                  lng, lnb, gn, nf, seg_ref, pr, pa, pg, wo, rw, rb,
                  x1_o, h2_o, ti_o, tg_o):
    seg = seg_ref[...]
    inv = 1.0 / RWKV_HEAD_DIM
    y = yf[...] + yb[...]
    mu = _seg_sum(y, seg) * inv
    yc = y - mu
    var = _seg_sum(yc * yc, seg) * inv
    o_r = (yc * lax.rsqrt(var + RWKV_GN_EPS) * lng[...] + lnb[...] + bon[...]) * g[...]

    gate_r = gr[...]
    silu_r = gate_r * _sigmoid(gate_r)
    gnv = gn[...]
    cols = []
    for h in range(GLA_HEADS):
        vs = slice(h * GLA_DV, (h + 1) * GLA_DV)
        o = gf[:, vs] + gb[:, vs]
        ms = jnp.mean(o * o, axis=-1, keepdims=True)
        cols.append(o * lax.rsqrt(ms + NORM_EPS) * gnv * silu_r[:, vs])
    o_g = jnp.concatenate(cols, axis=1)

    merged = (_sigmoid(mr[...]) * _dot(o_r.astype(BF16), pr[...])
              + _sigmoid(ma[...]) * _dot(oa[...].astype(BF16), pa[...])
              + _sigmoid(mg[...]) * _dot(o_g.astype(BF16), pg[...]))
    x1 = x_ref[...] + m_ref[2:3, :] * _dot(merged.astype(BF16), wo[...])
    x1_o[...] = x1
    hn = x1 * lax.rsqrt(jnp.mean(x1 * x1, axis=-1, keepdims=True) + NORM_EPS) * nf[...]
    h2 = hn * (1.0 + m_ref[4:5, :]) + m_ref[3:4, :]
    h2_o[...] = h2.astype(BF16)

    logits = _dot(h2, rw[...], HI) + rb[...]
    lane = lax.broadcasted_iota(jnp.int32, logits.shape, 1)
    vals, idxs = [], []
    for _ in range(TOP_K):
        m = jnp.max(logits, axis=-1, keepdims=True)
        idx = jnp.min(jnp.where(logits == m, lane, LANES), axis=-1, keepdims=True)
        vals.append(m)
        idxs.append(idx)
        logits = jnp.where(lane == idx, -jnp.inf, logits)
    es = [jnp.exp(vv - vals[0]) for vv in vals]
    den = es[0] + es[1] + es[2] + es[3]
    ti = jnp.zeros(logits.shape, jnp.int32)
    tg = jnp.zeros(logits.shape, F32)
    for j in range(TOP_K):
        ti = jnp.where(lane == j, idxs[j], ti)
        tg = jnp.where(lane == j, es[j] / den, tg)
    ti_o[...] = ti
    tg_o[...] = tg


def _merge_call(yf, yb, bon, g, oa, gf, gb, zg, zm, x, mod, lw, rows_per_cond):
    N, D = x.shape
    tm = ROW_TILE
    Wd = RWKV_WIDTH

    def rows(w, col=0):
        return pl.BlockSpec((tm, w), lambda i: (i, col))

    in_specs = [rows(Wd)] * 4 + [rows(ATT_WIDTH), rows(GLA_V_WIDTH), rows(GLA_V_WIDTH),
                                 rows(GLA_V_WIDTH, 2), rows(D, 0), rows(D, 1), rows(D, 2), rows(D),
                                 pl.BlockSpec((None, 6, D), lambda i: ((i * tm) // rows_per_cond, 0, 0)),
                                 _const_spec((1, Wd)), _const_spec((1, Wd)), _const_spec((1, GLA_DV)),
                                 _const_spec((1, D)), _const_spec((Wd, Wd)),
                                 _const_spec((Wd, D)), _const_spec((ATT_WIDTH, D)), _const_spec((GLA_V_WIDTH, D)),
                                 _const_spec((D, D)), _const_spec((D, LANES)), _const_spec((1, LANES))]
    return pl.pallas_call(
        _merge_kernel,
        grid=(N // tm,),
        in_specs=in_specs,
        out_specs=[rows(D), rows(D), rows(LANES), rows(LANES)],
        out_shape=[jax.ShapeDtypeStruct((N, D), F32), jax.ShapeDtypeStruct((N, D), BF16),
                   jax.ShapeDtypeStruct((N, LANES), jnp.int32), jax.ShapeDtypeStruct((N, LANES), F32)],
        compiler_params=_params(("parallel",)),
        name="merge",
    )(yf, yb, bon, g, oa, gf, gb, zg, zm, zm, zm, x, mod,
      lw["ln_g"], lw["ln_b"], lw["gla_norm"], lw["norm_ffn"], lw["seg64"],
      lw["p_rwkv"], lw["p_attn"], lw["p_gla"], lw["w_out"], lw["router_w"], lw["router_b"])


def _moe_kernel(be_ref, na_ref, x_ref, gate_ref, w1_ref, b1_ref, w2_ref, b2_ref, o_ref):
    i = pl.program_id(0)

    @pl.when(i < na_ref[0])
    def _():
        z = _dot(x_ref[...], w1_ref[...]) + b1_ref[...]
        glu = jnp.minimum(z[:, :D_EXPERT], SWIGLU_LIMIT)
        lin = jnp.clip(z[:, D_EXPERT:], -SWIGLU_LIMIT, SWIGLU_LIMIT)
        act = glu * _sigmoid(SWIGLU_ALPHA * glu) * (lin + 1.0)
        o_ref[...] = (_dot(act.astype(BF16), w2_ref[...]) + b2_ref[...]) * gate_ref[...]

    @pl.when(i >= na_ref[0])
    def _():
        o_ref[...] = jnp.zeros_like(o_ref)


def _moe_call(xg, row_gate, block_exp, n_active, w1, b1, w2, b2):
    R, D = xg.shape
    tb = EXPERT_BLOCK
    F2 = w1.shape[2]
    grid_spec = pltpu.PrefetchScalarGridSpec(
        num_scalar_prefetch=2,
        grid=(R // tb,),
        in_specs=[pl.BlockSpec((tb, D), lambda i, be, na: (i, 0)),
                  pl.BlockSpec((tb, 1), lambda i, be, na: (i, 0)),
                  pl.BlockSpec((None, D, F2), lambda i, be, na: (be[i], 0, 0)),
                  pl.BlockSpec((None, 1, F2), lambda i, be, na: (be[i], 0, 0)),
                  pl.BlockSpec((None, F2 // 2, D), lambda i, be, na: (be[i], 0, 0)),
                  pl.BlockSpec((None, 1, D), lambda i, be, na: (be[i], 0, 0))],
        out_specs=pl.BlockSpec((tb, D), lambda i, be, na: (i, 0)),
    )
    return pl.pallas_call(
        _moe_kernel,
        grid_spec=grid_spec,
        out_shape=jax.ShapeDtypeStruct((R, D), F32),
        compiler_params=_params(("arbitrary",)),
        name="moe_experts",
    )(block_exp, n_active, xg, row_gate, w1, b1, w2, b2)


def _combine_kernel(x1_ref, m_ref, y_ref, o_ref):
    D = D_MODEL
    y = (y_ref[:, 0:D] + y_ref[:, D:2 * D]) + (y_ref[:, 2 * D:3 * D] + y_ref[:, 3 * D:4 * D])
    o_ref[...] = x1_ref[...] + m_ref[5:6, :] * y


def _combine_call(x1, mod, y4, rows_per_cond):
    N, D = x1.shape
    tm = ROW_TILE
    return pl.pallas_call(
        _combine_kernel,
        grid=(N // tm,),
        in_specs=[pl.BlockSpec((tm, D), lambda i: (i, 0)),
                  pl.BlockSpec((None, 6, D), lambda i: ((i * tm) // rows_per_cond, 0, 0)),
                  pl.BlockSpec((tm, TOP_K * D), lambda i: (i, 0))],
        out_specs=pl.BlockSpec((tm, D), lambda i: (i, 0)),
        out_shape=jax.ShapeDtypeStruct((N, D), F32),
        compiler_params=_params(("parallel",)),
        name="combine",
    )(x1, mod, y4)


def _moe_ffn(h2, top_i, top_g, lw):
    N, D = h2.shape
    n_as = N * TOP_K
    n_blocks = -(-n_as // EXPERT_BLOCK) + N_EXPERTS
    flat_e = top_i.reshape(n_as)
    order = jnp.argsort(flat_e)
    e_sorted = flat_e[order]
    counts = jnp.bincount(flat_e, length=N_EXPERTS)
    starts = jnp.cumsum(counts) - counts
    padded = (counts + EXPERT_BLOCK - 1) // EXPERT_BLOCK * EXPERT_BLOCK
    pends = jnp.cumsum(padded)
    pstarts = pends - padded
    dest = (pstarts[e_sorted] + jnp.arange(n_as) - starts[e_sorted]).astype(jnp.int32)
    R = n_blocks * EXPERT_BLOCK
    row_tok = jnp.zeros((R,), jnp.int32).at[dest].set((order // TOP_K).astype(jnp.int32))
    row_gate = jnp.zeros((R,), F32).at[dest].set(top_g.reshape(n_as)[order])
    block_exp = jnp.minimum(jnp.searchsorted(pends, jnp.arange(n_blocks) * EXPERT_BLOCK, side="right"),
                            N_EXPERTS - 1).astype(jnp.int32)
    n_active = (pends[-1] // EXPERT_BLOCK).astype(jnp.int32).reshape(1)
    pos = jnp.zeros((n_as,), jnp.int32).at[order].set(dest)
    out = _moe_call(h2[row_tok], row_gate.reshape(R, 1), block_exp, n_active,
                    lw["moe_w1"], lw["moe_b1"], lw["moe_w2"], lw["moe_b2"])
    return out[pos].reshape(N, TOP_K * D)


def _seg_matrix(n, seg):
    idx = jnp.arange(n) // seg
    return (idx[:, None] == idx[None, :]).astype(BF16)


def _rope_tables(T):
    quarter = HEAD_DIM // 4
    inv_freq = ROPE_THETA ** (-jnp.arange(quarter, dtype=F32) / quarter)
    t = jnp.arange(T)
    row = (t // GRID_W).astype(F32)
    col = (t % GRID_W).astype(F32)
    ang_r = row[:, None] * inv_freq
    ang_c = col[:, None] * inv_freq
    cos = jnp.concatenate([jnp.cos(ang_r)] * 2 + [jnp.cos(ang_c)] * 2, axis=1)
    sin = jnp.concatenate([-jnp.sin(ang_r), jnp.sin(ang_r), -jnp.sin(ang_c), jnp.sin(ang_c)], axis=1)
    return jnp.concatenate([cos, cos], axis=1), jnp.concatenate([sin, sin], axis=1)


def _layer_weights(p, l):
    Wd = RWKV_WIDTH
    w_in = p["w_in"][l]
    c0 = RWKV_COLS
    aq, ak, av, gq, gk, gv, gl, gr = (c0, c0 + 512, c0 + 640, c0 + 768, c0 + 1280, c0 + 1792, c0 + 2816, c0 + 2848)
    m0 = gr + GLA_V_WIDTH
    D = D_MODEL
    w_g = jnp.concatenate([w_in[:, gq:gl], w_in[:, gr:m0], w_in[:, gl:gr],
                           jnp.zeros((D, LANES - 2 * GLA_LORA), F32)], axis=1)
    z64 = jnp.zeros((DECAY_LORA, Wd), F32)
    w2 = jnp.concatenate([jnp.concatenate([p["rwkv_w2"][l, 0], z64], axis=1),
                          jnp.concatenate([z64, p["rwkv_w2"][l, 1]], axis=1)], axis=0)
    a2 = jnp.concatenate([jnp.concatenate([p["rwkv_a2"][l, 0], z64], axis=1),
                          jnp.concatenate([z64, p["rwkv_a2"][l, 1]], axis=1)], axis=0)
    z16 = jnp.zeros((GLA_LORA, GLA_K_WIDTH), F32)
    ga2 = jnp.concatenate([jnp.concatenate([p["gla_a2"][l, 0], z16], axis=1),
                           jnp.concatenate([z16, p["gla_a2"][l, 1]], axis=1),
                           jnp.zeros((LANES - 2 * GLA_LORA, 2 * GLA_K_WIDTH), F32)], axis=0)
    return {
        "w_r": w_in[:, :c0].astype(BF16), "w_a": w_in[:, aq:gq].astype(BF16),
        "w_g": w_g.astype(BF16), "w_m": w_in[:, m0:].astype(BF16),
        "norm_mix": p["norm_mix"][l].reshape(1, D), "norm_ffn": p["norm_ffn"][l].reshape(1, D),
        "mu": p["rwkv_mu"][l].reshape(1, c0),
        "kk": p["rwkv_kk"][l].reshape(1, Wd), "ka": p["rwkv_ka"][l].reshape(1, Wd),
        "rk": p["rwkv_rk"][l].reshape(1, Wd),
        "w0": p["rwkv_w0"][l].reshape(1, 2 * Wd), "a0": p["rwkv_a0"][l].reshape(1, 2 * Wd),
        "w2": w2, "a2": a2, "g2": p["rwkv_g2"][l],
        "ln_g": p["rwkv_ln_g"][l].reshape(1, Wd), "ln_b": p["rwkv_ln_b"][l].reshape(1, Wd),
        "seg64": _seg_matrix(Wd, RWKV_HEAD_DIM),
        "qn": jnp.tile(p["attn_qn"][l], N_HEADS).reshape(1, ATT_WIDTH),
        "kn": jnp.tile(p["attn_kn"][l], KV_HEADS).reshape(1, KV_WIDTH),
        "gla_a2": ga2, "gla_ab": p["gla_ab"][l].reshape(1, 2 * GLA_K_WIDTH),
        "gla_norm": p["gla_norm"][l].reshape(1, GLA_DV),
        "p_rwkv": p["p_rwkv"][l].astype(BF16), "p_attn": p["p_attn"][l].astype(BF16),
        "p_gla": p["p_gla"][l].astype(BF16), "w_out": p["w_out"][l].astype(BF16),
        "router_w": jnp.concatenate([p["router_w"][l], jnp.zeros((D, LANES - N_EXPERTS), F32)], axis=1),
        "router_b": jnp.concatenate([p["router_b"][l], jnp.full((LANES - N_EXPERTS,), -jnp.inf, F32)]).reshape(1, LANES),
        "moe_w1": p["moe_w1"][l].astype(BF16), "moe_b1": p["moe_b1"][l].reshape(N_EXPERTS, 1, 2 * D_EXPERT),
        "moe_w2": p["moe_w2"][l].astype(BF16), "moe_b2": p["moe_b2"][l].reshape(N_EXPERTS, 1, D_MODEL),
    }


def _trunk_layer(x, mod, lw, B, T, ctx):
    N = B * T
    rpc = N if mod.shape[0] == 1 else T
    zr = _inproj_call(x, mod, lw["norm_mix"], lw["w_r"], rpc, 640)
    za = _inproj_call(x, mod, lw["norm_mix"], lw["w_a"], rpc, 768)
    zg = _inproj_call(x, mod, lw["norm_mix"], lw["w_g"], rpc, 640)
    zm = _inproj_call(x, mod, lw["norm_mix"], lw["w_m"], rpc, 1024)

    r, v, kk, g, bon, lw0, lw1, k0, k1, b0, b1 = _rwkv_prep_call(zr, T, lw)
    if ctx is None:
        s0r = jnp.zeros((B, 2, RWKV_HEADS, RWKV_HEAD_DIM, RWKV_HEAD_DIM), F32)
        s0g = jnp.zeros((B, 2, GLA_HEADS, GLA_DV, GLA_DK), F32)
        cache, tabs = None, None
    else:
        s0r = ctx["rwkv"]
        s0g = jnp.swapaxes(ctx["gla"], -1, -2)
        cache, tabs = (ctx["k"], ctx["v"], ctx["layer"]), _rope_tables(T)
    yf, yb, s_r = _rwkv_scan_call((r, v, kk, lw0, lw1, k0, k1, b0, b1), s0r, B, T)
    q, kn, kr = _attn_prep_call(za, lw, T, tabs)
    oa = _attn_call(q, kr, za, cache, B, T)
    gf, gb, s_g = _gla_scan_call(zg, s0g, lw["gla_a2"], lw["gla_ab"], B, T)
    x1, h2, ti, tg = _merge_call(yf, yb, bon, g, oa, gf, gb, zg, zm, x, mod, lw, rpc)
    y4 = _moe_ffn(h2, ti[:, :TOP_K], tg[:, :TOP_K], lw)
    x2 = _combine_call(x1, mod, y4, rpc)
    return x2, (kn, za[:, ATT_WIDTH + KV_WIDTH:], s_r, jnp.swapaxes(s_g, -1, -2))


def kernel(x_prompt, x_sample, c, cache_k, cache_v, state_rwkv, state_gla, c_ctx, ada_w, ada_b, norm_mix, norm_ffn, w_in, rwkv_mu, rwkv_w0, rwkv_w2, rwkv_a0, rwkv_a2, rwkv_g2, rwkv_kk, rwkv_ka, rwkv_rk, rwkv_ln_g, rwkv_ln_b, attn_qn, attn_kn, gla_a2, gla_ab, gla_norm, p_rwkv, p_attn, p_gla, w_out, router_w, router_b, moe_w1, moe_b1, moe_w2, moe_b2):
    p = {"w_in": w_in, "norm_mix": norm_mix, "norm_ffn": norm_ffn, "rwkv_mu": rwkv_mu, "rwkv_w0": rwkv_w0,
         "rwkv_w2": rwkv_w2, "rwkv_a0": rwkv_a0, "rwkv_a2": rwkv_a2, "rwkv_g2": rwkv_g2, "rwkv_kk": rwkv_kk,
         "rwkv_ka": rwkv_ka, "rwkv_rk": rwkv_rk, "rwkv_ln_g": rwkv_ln_g, "rwkv_ln_b": rwkv_ln_b,
         "attn_qn": attn_qn, "attn_kn": attn_kn, "gla_a2": gla_a2, "gla_ab": gla_ab, "gla_norm": gla_norm,
         "p_rwkv": p_rwkv, "p_attn": p_attn, "p_gla": p_gla, "w_out": w_out, "router_w": router_w,
         "router_b": router_b, "moe_w1": moe_w1, "moe_b1": moe_b1, "moe_w2": moe_w2, "moe_b2": moe_b2}
    Bp, Tp, D = x_prompt.shape
    Bs, Ts, _ = x_sample.shape
    L = ada_w.shape[0]
    P = cache_k.shape[2]

    cond8 = jnp.concatenate([c_ctx[None, :], c, jnp.zeros((8 - 1 - Bs, D), F32)], axis=0)
    mod = _ada_call(cond8, ada_w, ada_b).reshape(L, 8, 6, D)
    weights = [_layer_weights(p, l) for l in range(L)]
    ck = cache_k.reshape(Bs, L, P, KV_WIDTH)
    cv = cache_v.reshape(Bs, L, P, KV_WIDTH)

    yp = x_prompt.reshape(Bp * Tp, D)
    ks, vs, srs, sgs = [], [], [], []
    for l in range(L):
        yp, (kn, vh, s_r, s_g) = _trunk_layer(yp, mod[l, 0:1], weights[l], Bp, Tp, None)
        ks.append(kn.reshape(Bp, Tp, KV_HEADS, HEAD_DIM))
        vs.append(vh.reshape(Bp, Tp, KV_HEADS, HEAD_DIM))
        srs.append(s_r)
        sgs.append(s_g)

    ys = x_sample.reshape(Bs * Ts, D)
    for l in range(L):
        ctx = {"k": ck, "v": cv, "layer": l, "rwkv": state_rwkv[:, l], "gla": state_gla[:, l]}
        ys, _ = _trunk_layer(ys, mod[l, 1:1 + Bs], weights[l], Bs, Ts, ctx)

    return (yp.reshape(Bp, Tp, D), ys.reshape(Bs, Ts, D),
            jnp.stack(ks, axis=1), jnp.stack(vs, axis=1),
            jnp.stack(srs, axis=1), jnp.stack(sgs, axis=1))
```

```python
import functools

import jax
import jax.numpy as jnp
from jax import lax
from jax.experimental import pallas as pl
from jax.experimental.pallas import tpu as pltpu

D_MODEL = 1024
DEPTH = 2
GRID_W = 64
NORM_EPS = 1e-6

RWKV_HEADS = 8
RWKV_HEAD_DIM = 64
RWKV_WIDTH = RWKV_HEADS * RWKV_HEAD_DIM
DECAY_LORA = 64
ICLR_LORA = 64
GATE_LORA = 128
RWKV_GN_EPS = 6.4e-4
RWKV_COLS = 3 * RWKV_WIDTH + 2 * DECAY_LORA + 2 * ICLR_LORA + GATE_LORA

N_HEADS = 8
KV_HEADS = 2
HEAD_DIM = 64
ATT_WIDTH = N_HEADS * HEAD_DIM
KV_WIDTH = KV_HEADS * HEAD_DIM
ROPE_THETA = 10000.0

GLA_HEADS = 4
GLA_DK = 128
GLA_DV = 256
GLA_K_WIDTH = GLA_HEADS * GLA_DK
GLA_V_WIDTH = GLA_HEADS * GLA_DV
GLA_LORA = 16
GLA_GATE_NORMALIZER = 16.0

N_EXPERTS = 32
TOP_K = 4
D_EXPERT = 1024
SWIGLU_ALPHA = 1.702
SWIGLU_LIMIT = 7.0
EXPERT_BLOCK = 256

LANES = 128
CHUNK = 64
ROW_TILE = 256
Q_TILE = 128
VMEM_LIMIT = 52 * 1024 * 1024

F32 = jnp.float32
BF16 = jnp.bfloat16
HI = lax.Precision.HIGHEST


def _dot(a, b, prec=None):
    return jnp.dot(a, b, preferred_element_type=F32, precision=prec)


def _dot_nt(a, b, prec=None):
    return lax.dot_general(a, b, (((1,), (1,)), ((), ())), preferred_element_type=F32, precision=prec)


def _dot_tn(a, b, prec=None):
    return lax.dot_general(a, b, (((0,), (0,)), ((), ())), preferred_element_type=F32, precision=prec)


_NN = (((1,), (0,)), ((), ()))
_NT = (((1,), (1,)), ((), ()))
_TN = (((0,), (0,)), ((), ()))


def _dg(a, b, dn):
    return lax.dot_general(a, b, dn, preferred_element_type=F32)


def _split(x):
    h = x.astype(BF16)
    return h, (x - h.astype(F32)).astype(BF16)


def _mm3(a, b, dn=_NN):
    return _dg(a[0], b[0], dn) + _dg(a[0], b[1], dn) + _dg(a[1], b[0], dn)


def _cumsum_rows(mask, x):
    m = jnp.where(mask, 1.0, 0.0).astype(BF16)
    h = x.astype(BF16)
    r1 = x - h.astype(F32)
    mid = r1.astype(BF16)
    lo = (r1 - mid.astype(F32)).astype(BF16)
    return _dg(m, h, _NN) + _dg(m, mid, _NN) + _dg(m, lo, _NN)


def _seg_sum(x, seg_bf16):
    xh = x.astype(BF16)
    xl = (x - xh.astype(F32)).astype(BF16)
    return _dot(xh, seg_bf16) + _dot(xl, seg_bf16)


def _sigmoid(x):
    return 1.0 / (1.0 + jnp.exp(-x))


def _params(sem):
    return pltpu.CompilerParams(dimension_semantics=sem, vmem_limit_bytes=VMEM_LIMIT)


def _const_spec(shape):
    nd = len(shape)
    return pl.BlockSpec(shape, lambda *_: (0,) * nd)


def _ada_kernel(c_ref, w_ref, b_ref, o_ref):
    c = c_ref[...]
    s = c * _sigmoid(c)
    o_ref[...] = _dot(s, w_ref[...], HI) + b_ref[...]


def _ada_call(cond8, ada_w, ada_b):
    L, D, W = ada_w.shape
    tn = 1536
    return pl.pallas_call(
        _ada_kernel,
        grid=(L, W // tn),
        in_specs=[pl.BlockSpec((8, D), lambda l, j: (0, 0)),
                  pl.BlockSpec((None, D, tn), lambda l, j: (l, 0, j)),
                  pl.BlockSpec((None, 1, tn), lambda l, j: (l, 0, j))],
        out_specs=pl.BlockSpec((None, 8, tn), lambda l, j: (l, 0, j)),
        out_shape=jax.ShapeDtypeStruct((L, 8, W), F32),
        compiler_params=_params(("parallel", "parallel")),
        name="ada",
    )(cond8, ada_w, ada_b.reshape(L, 1, W))


def _inproj_kernel(x_ref, m_ref, g_ref, w_ref, o_ref, h_scr):
    @pl.when(pl.program_id(1) == 0)
    def _():
        x = x_ref[...]
        y = x * lax.rsqrt(jnp.mean(x * x, axis=-1, keepdims=True) + NORM_EPS) * g_ref[...]
        h = y * (1.0 + m_ref[1:2, :]) + m_ref[0:1, :]
        h_scr[...] = h.astype(BF16)

    o_ref[...] = _dot(h_scr[...], w_ref[...])


def _inproj_call(x, mod, gain, w, rows_per_cond, tn):
    N, D = x.shape
    W = w.shape[1]
    tm = ROW_TILE
    return pl.pallas_call(
        _inproj_kernel,
        grid=(N // tm, W // tn),
        in_specs=[pl.BlockSpec((tm, D), lambda i, j: (i, 0)),
                  pl.BlockSpec((None, 6, D), lambda i, j: ((i * tm) // rows_per_cond, 0, 0)),
                  pl.BlockSpec((1, D), lambda i, j: (0, 0)),
                  pl.BlockSpec((D, tn), lambda i, j: (0, j))],
        out_specs=pl.BlockSpec((tm, tn), lambda i, j: (i, j)),
        out_shape=jax.ShapeDtypeStruct((N, W), F32),
        scratch_shapes=[pltpu.VMEM((tm, D), BF16)],
        compiler_params=_params(("parallel", "arbitrary")),
        name="inproj",
    )(x, mod, gain, w)


def _rwkv_prep_kernel(z_ref, zp_ref, zn_ref, mu_ref, kkp_ref, ka_ref, rk_ref, w0_ref, a0_ref,
                      w2_ref, a2_ref, g2_ref, seg_ref,
                      r_o, v_o, kk_o, g_o, bon_o, lw0_o, lw1_o, k0_o, k1_o, b0_o, b1_o,
                      *, tm, blocks_per_seq):
    i = pl.program_id(0)
    z = z_ref[...]
    row = lax.broadcasted_iota(jnp.int32, (tm, 1), 0)
    first = (i % blocks_per_seq) == 0
    last = (i % blocks_per_seq) == blocks_per_seq - 1
    pz = jnp.where(first, 0.0, zp_ref[7:8, :])
    nz = jnp.where(last, 0.0, zn_ref[0:1, :])
    prev = jnp.where(row == 0, pz, pltpu.roll(z, 1, 0))
    nxt = jnp.where(row == tm - 1, nz, pltpu.roll(z, tm - 1, 0))
    zm = z + mu_ref[...] * (0.5 * (prev + nxt) - z)

    Wd = RWKV_WIDTH
    r = zm[:, 0:Wd]
    k = zm[:, Wd:2 * Wd]
    v = zm[:, 2 * Wd:3 * Wd]
    lw = zm[:, 3 * Wd:3 * Wd + 128]
    la = zm[:, 3 * Wd + 128:3 * Wd + 256]
    lg = zm[:, 3 * Wd + 256:3 * Wd + 384]
    seg = seg_ref[...]

    kk = k * kkp_ref[...]
    kk = kk / jnp.maximum(jnp.sqrt(_seg_sum(kk * kk, seg)), 1e-12)
    w_raw = _dot(jnp.tanh(lw), w2_ref[...], HI) + w0_ref[...]
    logw = -_sigmoid(w_raw) * 0.6065306597126334
    a = _sigmoid(_dot(la, a2_ref[...], HI) + a0_ref[...])
    ka = ka_ref[...]
    rk = rk_ref[...]
    bonus = jnp.zeros_like(r)
    for d, (lw_o, k_o, b_o) in enumerate(((lw0_o, k0_o, b0_o), (lw1_o, k1_o, b1_o))):
        a_d = a[:, d * Wd:(d + 1) * Wd]
        k_d = k * (1.0 + (a_d - 1.0) * ka)
        lw_o[...] = logw[:, d * Wd:(d + 1) * Wd]
        k_o[...] = k_d
        b_o[...] = kk * a_d
        bonus = bonus + _seg_sum(r * k_d * rk, seg) * v
    r_o[...] = r
    v_o[...] = v
    kk_o[...] = kk
    bon_o[...] = bonus
    g_o[...] = _dot(_sigmoid(lg), g2_ref[...], HI)


def _rwkv_prep_call(zr, seq_len, lw):
    N, Wz = zr.shape
    tm = ROW_TILE
    bps = seq_len // tm
    Wd = RWKV_WIDTH
    nb8 = N // 8
    row_spec = pl.BlockSpec((tm, Wd), lambda i: (i, 0))
    out = jax.ShapeDtypeStruct((N, Wd), F32)
    return pl.pallas_call(
        functools.partial(_rwkv_prep_kernel, tm=tm, blocks_per_seq=bps),
        grid=(N // tm,),
        in_specs=[pl.BlockSpec((tm, Wz), lambda i: (i, 0)),
                  pl.BlockSpec((8, Wz), lambda i: (jnp.maximum(i * (tm // 8) - 1, 0), 0)),
                  pl.BlockSpec((8, Wz), lambda i: (jnp.minimum((i + 1) * (tm // 8), nb8 - 1), 0)),
                  _const_spec((1, Wz)), _const_spec((1, Wd)), _const_spec((1, Wd)), _const_spec((1, Wd)),
                  _const_spec((1, 2 * Wd)), _const_spec((1, 2 * Wd)),
                  _const_spec((128, 2 * Wd)), _const_spec((128, 2 * Wd)), _const_spec((128, Wd)),
                  _const_spec((Wd, Wd))],
        out_specs=[row_spec] * 11,
        out_shape=[out] * 11,
        compiler_params=_params(("parallel",)),
        name="rwkv_prep",
    )(zr, zr, zr, lw["mu"], lw["kk"], lw["ka"], lw["rk"], lw["w0"], lw["a0"],
      lw["w2"], lw["a2"], lw["g2"], lw["seg64"])


def _tri_masks(d):
    ti = lax.broadcasted_iota(jnp.int32, (CHUNK, CHUNK), 0)
    si = lax.broadcasted_iota(jnp.int32, (CHUNK, CHUNK), 1)
    if d == 0:
        return si <= ti, si < ti
    return si >= ti, si > ti


def _rwkv_scan_kernel(rf, vf, kkf, lwf, kf, bf, rb, vb, kkb, lwb, kb, bb, s0_ref,
                      yf_o, yb_o, st_o, s_scr, *, nc):
    i = pl.program_id(1)

    @pl.when(i == 0)
    def _():
        s_scr[...] = s0_ref[...]

    C = CHUNK
    eye = (lax.broadcasted_iota(jnp.int32, (C, C), 0)
           == lax.broadcasted_iota(jnp.int32, (C, C), 1)).astype(F32)
    H, Dh = RWKV_HEADS, RWKV_HEAD_DIM
    sls = [slice(h * Dh, (h + 1) * Dh) for h in range(H)]
    dirs = ((rf, vf, kkf, lwf, kf, bf, yf_o), (rb, vb, kkb, lwb, kb, bb, yb_o))
    for d, (r_ref, v_ref, kk_ref, lw_ref, k_ref, b_ref, y_o) in enumerate(dirs):
        incl, strict = _tri_masks(d)
        lw = lw_ref[...]
        b = _cumsum_rows(incl, lw)
        btot = jnp.sum(lw, axis=0, keepdims=True)
        r, v, kk, k, beta = r_ref[...], v_ref[...], kk_ref[...], k_ref[...], b_ref[...]
        nb = jnp.exp(-b)
        eb = jnp.exp(btot - b)
        gtot = jnp.exp(btot)
        ar = _split(jnp.concatenate([-kk * jnp.exp(b - lw), r * jnp.exp(b)], axis=0))
        bbar = _split(beta * nb)
        kbar = (k * nb).astype(BF16)
        hat = jnp.concatenate([beta * eb, k * eb], axis=0).astype(BF16)
        v16 = v.astype(BF16)

        g1 = [_mm3((ar[0][:, s], ar[1][:, s]), (bbar[0][:, s], bbar[1][:, s]), _NT) for s in sls]
        g2 = [_dg(ar[0][:, s], kbar[:, s], _NT) for s in sls]
        a_ab = [jnp.where(strict, g[:C], 0.0) for g in g1]
        a_rb = [jnp.where(incl, g[C:], 0.0).astype(BF16) for g in g1]
        a_ak = [jnp.where(strict, g[:C], 0.0).astype(BF16) for g in g2]
        a_rk = [jnp.where(incl, g[C:], 0.0).astype(BF16) for g in g2]
        av = [_dg(a_ak[h], v16[:, sls[h]], _NN) for h in range(H)]
        p = [_split(a) for a in a_ab]
        t = [eye + a for a in a_ab]
        for _ in range(5):
            p = [_split(_mm3(pp, pp)) for pp in p]
            t = [tt + _mm3(_split(tt), pp) for tt, pp in zip(t, p)]
        for h in range(H):
            s = s_scr[d, h]
            z = _dg(ar[0][:, sls[h]], s.astype(BF16), _NT)
            u = _mm3(_split(t[h]), _split(z[:C] + av[h])).astype(BF16)
            vh = v16[:, sls[h]]
            y_o[:, sls[h]] = z[C:] + _dg(a_rb[h], u, _NN) + _dg(a_rk[h], vh, _NN)
            s_scr[d, h] = (s * gtot[:, sls[h]]
                           + _dg(jnp.concatenate([u, vh], axis=0), hat[:, sls[h]], _TN))

    @pl.when(i == nc - 1)
    def _():
        st_o[...] = s_scr[...]


def _rwkv_scan_call(prep, s0, B, T):
    r, v, kk, lw0, lw1, k0, k1, b0, b1 = prep
    N, Wd = r.shape
    nc = T // CHUNK
    fwd = pl.BlockSpec((CHUNK, Wd), lambda b, i: (b * nc + i, 0))
    bwd = pl.BlockSpec((CHUNK, Wd), lambda b, i: (b * nc + nc - 1 - i, 0))
    st = pl.BlockSpec((None, 2, RWKV_HEADS, RWKV_HEAD_DIM, RWKV_HEAD_DIM), lambda b, i: (b, 0, 0, 0, 0))
    y = jax.ShapeDtypeStruct((N, Wd), F32)
    return pl.pallas_call(
        functools.partial(_rwkv_scan_kernel, nc=nc),
        grid=(B, nc),
        in_specs=[fwd] * 6 + [bwd] * 6 + [st],
        out_specs=[fwd, bwd, st],
        out_shape=[y, y, jax.ShapeDtypeStruct(s0.shape, F32)],
        scratch_shapes=[pltpu.VMEM((2, RWKV_HEADS, RWKV_HEAD_DIM, RWKV_HEAD_DIM), F32)],
        compiler_params=_params(("parallel", "arbitrary")),
        name="rwkv_scan",
    )(r, v, kk, lw0, k0, b0, r, v, kk, lw1, k1, b1, s0)


def _gla_scan_kernel(xf, glf, xb, glb, a2_ref, ab_ref, s0_ref, of_o, ob_o, st_o, s_scr, *, nc):
    i = pl.program_id(1)

    @pl.when(i == 0)
    def _():
        s_scr[...] = s0_ref[...]

    Kw = GLA_K_WIDTH
    for d, (x_ref, gl_ref, o_o) in enumerate(((xf, glf, of_o), (xb, glb, ob_o))):
        incl, _ = _tri_masks(d)
        xa = (_mm3(_split(gl_ref[...]), _split(a2_ref[:, d * Kw:(d + 1) * Kw]))
              + ab_ref[:, d * Kw:(d + 1) * Kw])
        log_a = (jnp.minimum(xa, 0.0) - jnp.log1p(jnp.exp(-jnp.abs(xa)))) * (1.0 / GLA_GATE_NORMALIZER)
        b = _cumsum_rows(incl, log_a)
        bl = jnp.sum(log_a, axis=0, keepdims=True)
        q = x_ref[:, 0:Kw] * (GLA_DK ** -0.5)
        k = x_ref[:, Kw:2 * Kw]
        q_in = (q * jnp.exp(b)).astype(BF16)
        k_in = (k * jnp.exp(-b)).astype(BF16)
        k_end = (k * jnp.exp(bl - b)).astype(BF16)
        gtot = jnp.exp(bl)
        for h in range(GLA_HEADS):
            ks = slice(h * GLA_DK, (h + 1) * GLA_DK)
            v_h = x_ref[:, 2 * Kw + h * GLA_DV:2 * Kw + (h + 1) * GLA_DV].astype(BF16)
            att = jnp.where(incl, _dg(q_in[:, ks], k_in[:, ks], _NT), 0.0).astype(BF16)
            s = s_scr[d, h]
            o_o[:, h * GLA_DV:(h + 1) * GLA_DV] = (_dg(att, v_h, _NN)
                                                   + _dg(q_in[:, ks], s.astype(BF16), _NT))
            s_scr[d, h] = s * gtot[:, ks] + _dg(v_h, k_end[:, ks], _TN)

    @pl.when(i == nc - 1)
    def _():
        st_o[...] = s_scr[...]


def _gla_scan_call(zg, s0t, a2, ab, B, T):
    N = zg.shape[0]
    nc = T // CHUNK
    Wx = 2 * GLA_K_WIDTH + GLA_V_WIDTH
    gl_col = (Wx + GLA_V_WIDTH) // LANES
    xf = pl.BlockSpec((CHUNK, Wx), lambda b, i: (b * nc + i, 0))
    xb = pl.BlockSpec((CHUNK, Wx), lambda b, i: (b * nc + nc - 1 - i, 0))
    gf = pl.BlockSpec((CHUNK, LANES), lambda b, i: (b * nc + i, gl_col))
    gb = pl.BlockSpec((CHUNK, LANES), lambda b, i: (b * nc + nc - 1 - i, gl_col))
    of = pl.BlockSpec((CHUNK, GLA_V_WIDTH), lambda b, i: (b * nc + i, 0))
    ob = pl.BlockSpec((CHUNK, GLA_V_WIDTH), lambda b, i: (b * nc + nc - 1 - i, 0))
    st = pl.BlockSpec((None, 2, GLA_HEADS, GLA_DV, GLA_DK), lambda b, i: (b, 0, 0, 0, 0))
    o = jax.ShapeDtypeStruct((N, GLA_V_WIDTH), F32)
    return pl.pallas_call(
        functools.partial(_gla_scan_kernel, nc=nc),
        grid=(B, nc),
        in_specs=[xf, gf, xb, gb, _const_spec((LANES, 2 * GLA_K_WIDTH)), _const_spec((1, 2 * GLA_K_WIDTH)), st],
        out_specs=[of, ob, st],
        out_shape=[o, o, jax.ShapeDtypeStruct(s0t.shape, F32)],
        scratch_shapes=[pltpu.VMEM((2, GLA_HEADS, GLA_DV, GLA_DK), F32)],
        compiler_params=_params(("parallel", "arbitrary")),
        name="gla_scan",
    )(zg, zg, zg, zg, a2, ab, s0t)


def _rope(x, cos, sin):
    n = x.shape[1]
    lane = lax.broadcasted_iota(jnp.int32, x.shape, 1)
    up = pltpu.roll(x, n - 16, 1)
    dn = pltpu.roll(x, 16, 1)
    sw = jnp.where((lane % 32) < 16, up, dn)
    reps = n // LANES
    if reps > 1:
        cos = jnp.concatenate([cos] * reps, axis=1)
        sin = jnp.concatenate([sin] * reps, axis=1)
    return x * cos + sw * sin


def _attn_prep_kernel(*refs, rope):
    if rope:
        z_ref, qn_ref, kn_ref, segq_ref, segk_ref, cos_ref, sin_ref, q_o, kn_o, kr_o = refs
    else:
        z_ref, qn_ref, kn_ref, segq_ref, segk_ref, q_o, kn_o, kr_o = refs
    zq = z_ref[:, 0:ATT_WIDTH]
    zk = z_ref[:, ATT_WIDTH:ATT_WIDTH + KV_WIDTH]
    qh = zq * lax.rsqrt(_seg_sum(zq * zq, segq_ref[...]) * (1.0 / HEAD_DIM) + NORM_EPS) * qn_ref[...]
    kh = zk * lax.rsqrt(_seg_sum(zk * zk, segk_ref[...]) * (1.0 / HEAD_DIM) + NORM_EPS) * kn_ref[...]
    kn_o[...] = kh
    if rope:
        qh = _rope(qh, cos_ref[...], sin_ref[...])
        kh = _rope(kh, cos_ref[...], sin_ref[...])
    q_o[...] = (qh * (HEAD_DIM ** -0.5)).astype(BF16)
    kr_o[...] = kh.astype(BF16)


def _attn_prep_call(za, lw, seq_len, rope_tabs):
    N, Wz = za.shape
    tm = ROW_TILE
    bps = seq_len // tm
    rope = rope_tabs is not None
    in_specs = [pl.BlockSpec((tm, Wz), lambda i: (i, 0)),
                _const_spec((1, ATT_WIDTH)), _const_spec((1, KV_WIDTH)),
                _const_spec((ATT_WIDTH, ATT_WIDTH)), _const_spec((KV_WIDTH, KV_WIDTH))]
    args = [za, lw["qn"], lw["kn"], lw["seg64"], lw["seg64"][:KV_WIDTH, :KV_WIDTH]]
    if rope:
        in_specs += [pl.BlockSpec((tm, LANES), lambda i: (i % bps, 0))] * 2
        args += list(rope_tabs)
    return pl.pallas_call(
        functools.partial(_attn_prep_kernel, rope=rope),
        grid=(N // tm,),
        in_specs=in_specs,
        out_specs=[pl.BlockSpec((tm, ATT_WIDTH), lambda i: (i, 0)),
                   pl.BlockSpec((tm, KV_WIDTH), lambda i: (i, 0)),
                   pl.BlockSpec((tm, KV_WIDTH), lambda i: (i, 0))],
        out_shape=[jax.ShapeDtypeStruct((N, ATT_WIDTH), BF16),
                   jax.ShapeDtypeStruct((N, KV_WIDTH), F32),
                   jax.ShapeDtypeStruct((N, KV_WIDTH), BF16)],
        compiler_params=_params(("parallel",)),
        name="attn_prep",
    )(*args)


def _attn_kernel(*refs, cached):
    if cached:
        q_ref, k_ref, v_ref, ck_ref, cv_ref, o_ref = refs
    else:
        q_ref, k_ref, v_ref, o_ref = refs
    G = N_HEADS // KV_HEADS
    for g in range(KV_HEADS):
        gs = slice(g * HEAD_DIM, (g + 1) * HEAD_DIM)
        kg = k_ref[:, gs]
        vg = v_ref[:, gs].astype(BF16)
        if cached:
            ckg = ck_ref[:, gs].astype(BF16)
            cvg = cv_ref[:, gs].astype(BF16)
        for hh in range(G):
            hs = slice((g * G + hh) * HEAD_DIM, (g * G + hh + 1) * HEAD_DIM)
            qh = q_ref[:, hs]
            s1 = _dot_nt(qh, kg)
            m = jnp.max(s1, axis=-1, keepdims=True)
            if cached:
                s2 = _dot_nt(qh, ckg)
                m = jnp.maximum(m, jnp.max(s2, axis=-1, keepdims=True))
            p1 = jnp.exp(s1 - m)
            l = jnp.sum(p1, axis=-1, keepdims=True)
            o = _dot(p1.astype(BF16), vg)
            if cached:
                p2 = jnp.exp(s2 - m)
                l = l + jnp.sum(p2, axis=-1, keepdims=True)
                o = o + _dot(p2.astype(BF16), cvg)
            o_ref[:, hs] = o / l


def _attn_call(q, k, za, cache, B, T):
    N = q.shape[0]
    tq = Q_TILE
    nq = T // tq
    v_col = (ATT_WIDTH + KV_WIDTH) // KV_WIDTH
    in_specs = [pl.BlockSpec((tq, ATT_WIDTH), lambda b, i: (b * nq + i, 0)),
                pl.BlockSpec((T, KV_WIDTH), lambda b, i: (b, 0)),
                pl.BlockSpec((T, KV_WIDTH), lambda b, i: (b, v_col))]
    args = [q, k, za]
    if cache is not None:
        ck, cv, layer = cache
        P = ck.shape[2]
        cspec = pl.BlockSpec((None, None, P, KV_WIDTH), lambda b, i: (b, layer, 0, 0))
        in_specs += [cspec, cspec]
        args += [ck, cv]
    return pl.pallas_call(
        functools.partial(_attn_kernel, cached=cache is not None),
        grid=(B, nq),
        in_specs=in_specs,
        out_specs=pl.BlockSpec((tq, ATT_WIDTH), lambda b, i: (b * nq + i, 0)),
        out_shape=jax.ShapeDtypeStruct((N, ATT_WIDTH), F32),
        compiler_params=_params(("parallel", "arbitrary")),
        name="attention",
    )(*args)


def _merge_kernel(yf, yb, bon, g, oa, gf, gb, gr, mr, ma, mg, x_ref, m_ref,
                  lng, lnb, gn, nf, seg_ref, pr, pa, pg, wo, rw, rb,
                  x1_o, h2_o, ti_o, tg_o):
    seg = seg_ref[...]
    inv = 1.0 / RWKV_HEAD_DIM
    y = yf[...] + yb[...]
    mu = _seg_sum(y, seg) * inv
    yc = y - mu
    var = _seg_sum(yc * yc, seg) * inv
    o_r = (yc * lax.rsqrt(var + RWKV_GN_EPS) * lng[...] + lnb[...] + bon[...]) * g[...]

    gate_r = gr[...]
    silu_r = gate_r * _sigmoid(gate_r)
    gnv = gn[...]
    cols = []
    for h in range(GLA_HEADS):
        vs = slice(h * GLA_DV, (h + 1) * GLA_DV)
        o = gf[:, vs] + gb[:, vs]
        ms = jnp.mean(o * o, axis=-1, keepdims=True)
        cols.append(o * lax.rsqrt(ms + NORM_EPS) * gnv * silu_r[:, vs])
    o_g = jnp.concatenate(cols, axis=1)

    merged = (_sigmoid(mr[...]) * _dot(o_r.astype(BF16), pr[...])
              + _sigmoid(ma[...]) * _dot(oa[...].astype(BF16), pa[...])
              + _sigmoid(mg[...]) * _dot(o_g.astype(BF16), pg[...]))
    x1 = x_ref[...] + m_ref[2:3, :] * _dot(merged.astype(BF16), wo[...])
    x1_o[...] = x1
    hn = x1 * lax.rsqrt(jnp.mean(x1 * x1, axis=-1, keepdims=True) + NORM_EPS) * nf[...]
    h2 = hn * (1.0 + m_ref[4:5, :]) + m_ref[3:4, :]
    h2_o[...] = h2.astype(BF16)

    logits = _dot(h2, rw[...], HI) + rb[...]
    lane = lax.broadcasted_iota(jnp.int32, logits.shape, 1)
    vals, idxs = [], []
    for _ in range(TOP_K):
        m = jnp.max(logits, axis=-1, keepdims=True)
        idx = jnp.min(jnp.where(logits == m, lane, LANES), axis=-1, keepdims=True)
        vals.append(m)
        idxs.append(idx)
        logits = jnp.where(lane == idx, -jnp.inf, logits)
    es = [jnp.exp(vv - vals[0]) for vv in vals]
    den = es[0] + es[1] + es[2] + es[3]
    ti = jnp.zeros(logits.shape, jnp.int32)
    tg = jnp.zeros(logits.shape, F32)
    for j in range(TOP_K):
        ti = jnp.where(lane == j, idxs[j], ti)
        tg = jnp.where(lane == j, es[j] / den, tg)
    ti_o[...] = ti
    tg_o[...] = tg


def _merge_call(yf, yb, bon, g, oa, gf, gb, zg, zm, x, mod, lw, rows_per_cond):
    N, D = x.shape
    tm = ROW_TILE
    Wd = RWKV_WIDTH

    def rows(w, col=0):
        return pl.BlockSpec((tm, w), lambda i: (i, col))

    in_specs = [rows(Wd)] * 4 + [rows(ATT_WIDTH), rows(GLA_V_WIDTH), rows(GLA_V_WIDTH),
                                 rows(GLA_V_WIDTH, 2), rows(D, 0), rows(D, 1), rows(D, 2), rows(D),
                                 pl.BlockSpec((None, 6, D), lambda i: ((i * tm) // rows_per_cond, 0, 0)),
                                 _const_spec((1, Wd)), _const_spec((1, Wd)), _const_spec((1, GLA_DV)),
                                 _const_spec((1, D)), _const_spec((Wd, Wd)),
                                 _const_spec((Wd, D)), _const_spec((ATT_WIDTH, D)), _const_spec((GLA_V_WIDTH, D)),
                                 _const_spec((D, D)), _const_spec((D, LANES)), _const_spec((1, LANES))]
    return pl.pallas_call(
        _merge_kernel,
        grid=(N // tm,),
        in_specs=in_specs,
        out_specs=[rows(D), rows(D), rows(LANES), rows(LANES)],
        out_shape=[jax.ShapeDtypeStruct((N, D), F32), jax.ShapeDtypeStruct((N, D), BF16),
                   jax.ShapeDtypeStruct((N, LANES), jnp.int32), jax.ShapeDtypeStruct((N, LANES), F32)],
        compiler_params=_params(("parallel",)),
        name="merge",
    )(yf, yb, bon, g, oa, gf, gb, zg, zm, zm, zm, x, mod,
      lw["ln_g"], lw["ln_b"], lw["gla_norm"], lw["norm_ffn"], lw["seg64"],
      lw["p_rwkv"], lw["p_attn"], lw["p_gla"], lw["w_out"], lw["router_w"], lw["router_b"])


def _moe_kernel(be_ref, na_ref, x_ref, gate_ref, w1_ref, b1_ref, w2_ref, b2_ref, o_ref):
    i = pl.program_id(0)

    @pl.when(i < na_ref[0])
    def _():
        z = _dot(x_ref[...], w1_ref[...]) + b1_ref[...]
        glu = jnp.minimum(z[:, :D_EXPERT], SWIGLU_LIMIT)
        lin = jnp.clip(z[:, D_EXPERT:], -SWIGLU_LIMIT, SWIGLU_LIMIT)
        act = glu * _sigmoid(SWIGLU_ALPHA * glu) * (lin + 1.0)
        o_ref[...] = (_dot(act.astype(BF16), w2_ref[...]) + b2_ref[...]) * gate_ref[...]

    @pl.when(i >= na_ref[0])
    def _():
        o_ref[...] = jnp.zeros_like(o_ref)


def _moe_call(xg, row_gate, block_exp, n_active, w1, b1, w2, b2):
    R, D = xg.shape
    tb = EXPERT_BLOCK
    F2 = w1.shape[2]
    grid_spec = pltpu.PrefetchScalarGridSpec(
        num_scalar_prefetch=2,
        grid=(R // tb,),
        in_specs=[pl.BlockSpec((tb, D), lambda i, be, na: (i, 0)),
                  pl.BlockSpec((tb, 1), lambda i, be, na: (i, 0)),
                  pl.BlockSpec((None, D, F2), lambda i, be, na: (be[i], 0, 0)),
                  pl.BlockSpec((None, 1, F2), lambda i, be, na: (be[i], 0, 0)),
                  pl.BlockSpec((None, F2 // 2, D), lambda i, be, na: (be[i], 0, 0)),
                  pl.BlockSpec((None, 1, D), lambda i, be, na: (be[i], 0, 0))],
        out_specs=pl.BlockSpec((tb, D), lambda i, be, na: (i, 0)),
    )
    return pl.pallas_call(
        _moe_kernel,
        grid_spec=grid_spec,
        out_shape=jax.ShapeDtypeStruct((R, D), F32),
        compiler_params=_params(("arbitrary",)),
        name="moe_experts",
    )(block_exp, n_active, xg, row_gate, w1, b1, w2, b2)


def _combine_kernel(x1_ref, m_ref, y_ref, o_ref):
    D = D_MODEL
    y = (y_ref[:, 0:D] + y_ref[:, D:2 * D]) + (y_ref[:, 2 * D:3 * D] + y_ref[:, 3 * D:4 * D])
    o_ref[...] = x1_ref[...] + m_ref[5:6, :] * y


def _combine_call(x1, mod, y4, rows_per_cond):
    N, D = x1.shape
    tm = ROW_TILE
    return pl.pallas_call(
        _combine_kernel,
        grid=(N // tm,),
        in_specs=[pl.BlockSpec((tm, D), lambda i: (i, 0)),
                  pl.BlockSpec((None, 6, D), lambda i: ((i * tm) // rows_per_cond, 0, 0)),
                  pl.BlockSpec((tm, TOP_K * D), lambda i: (i, 0))],
        out_specs=pl.BlockSpec((tm, D), lambda i: (i, 0)),
        out_shape=jax.ShapeDtypeStruct((N, D), F32),
        compiler_params=_params(("parallel",)),
        name="combine",
    )(x1, mod, y4)


def _moe_ffn(h2, top_i, top_g, lw):
    N, D = h2.shape
    n_as = N * TOP_K
    n_blocks = -(-n_as // EXPERT_BLOCK) + N_EXPERTS
    flat_e = top_i.reshape(n_as)
    order = jnp.argsort(flat_e)
    e_sorted = flat_e[order]
    counts = jnp.bincount(flat_e, length=N_EXPERTS)
    starts = jnp.cumsum(counts) - counts
    padded = (counts + EXPERT_BLOCK - 1) // EXPERT_BLOCK * EXPERT_BLOCK
    pends = jnp.cumsum(padded)
    pstarts = pends - padded
    dest = (pstarts[e_sorted] + jnp.arange(n_as) - starts[e_sorted]).astype(jnp.int32)
    R = n_blocks * EXPERT_BLOCK
    row_tok = jnp.zeros((R,), jnp.int32).at[dest].set((order // TOP_K).astype(jnp.int32))
    row_gate = jnp.zeros((R,), F32).at[dest].set(top_g.reshape(n_as)[order])
    block_exp = jnp.minimum(jnp.searchsorted(pends, jnp.arange(n_blocks) * EXPERT_BLOCK, side="right"),
                            N_EXPERTS - 1).astype(jnp.int32)
    n_active = (pends[-1] // EXPERT_BLOCK).astype(jnp.int32).reshape(1)
    pos = jnp.zeros((n_as,), jnp.int32).at[order].set(dest)
    out = _moe_call(h2[row_tok], row_gate.reshape(R, 1), block_exp, n_active,
                    lw["moe_w1"], lw["moe_b1"], lw["moe_w2"], lw["moe_b2"])
    return out[pos].reshape(N, TOP_K * D)


def _seg_matrix(n, seg):
    idx = jnp.arange(n) // seg
    return (idx[:, None] == idx[None, :]).astype(BF16)


def _rope_tables(T):
    quarter = HEAD_DIM // 4
    inv_freq = ROPE_THETA ** (-jnp.arange(quarter, dtype=F32) / quarter)
    t = jnp.arange(T)
    row = (t // GRID_W).astype(F32)
    col = (t % GRID_W).astype(F32)
    ang_r = row[:, None] * inv_freq
    ang_c = col[:, None] * inv_freq
    cos = jnp.concatenate([jnp.cos(ang_r)] * 2 + [jnp.cos(ang_c)] * 2, axis=1)
    sin = jnp.concatenate([-jnp.sin(ang_r), jnp.sin(ang_r), -jnp.sin(ang_c), jnp.sin(ang_c)], axis=1)
    return jnp.concatenate([cos, cos], axis=1), jnp.concatenate([sin, sin], axis=1)


def _layer_weights(p, l):
    Wd = RWKV_WIDTH
    w_in = p["w_in"][l]
    c0 = RWKV_COLS
    aq, ak, av, gq, gk, gv, gl, gr = (c0, c0 + 512, c0 + 640, c0 + 768, c0 + 1280, c0 + 1792, c0 + 2816, c0 + 2848)
    m0 = gr + GLA_V_WIDTH
    D = D_MODEL
    w_g = jnp.concatenate([w_in[:, gq:gl], w_in[:, gr:m0], w_in[:, gl:gr],
                           jnp.zeros((D, LANES - 2 * GLA_LORA), F32)], axis=1)
    z64 = jnp.zeros((DECAY_LORA, Wd), F32)
    w2 = jnp.concatenate([jnp.concatenate([p["rwkv_w2"][l, 0], z64], axis=1),
                          jnp.concatenate([z64, p["rwkv_w2"][l, 1]], axis=1)], axis=0)
    a2 = jnp.concatenate([jnp.concatenate([p["rwkv_a2"][l, 0], z64], axis=1),
                          jnp.concatenate([z64, p["rwkv_a2"][l, 1]], axis=1)], axis=0)
    z16 = jnp.zeros((GLA_LORA, GLA_K_WIDTH), F32)
    ga2 = jnp.concatenate([jnp.concatenate([p["gla_a2"][l, 0], z16], axis=1),
                           jnp.concatenate([z16, p["gla_a2"][l, 1]], axis=1),
                           jnp.zeros((LANES - 2 * GLA_LORA, 2 * GLA_K_WIDTH), F32)], axis=0)
    return {
        "w_r": w_in[:, :c0].astype(BF16), "w_a": w_in[:, aq:gq].astype(BF16),
        "w_g": w_g.astype(BF16), "w_m": w_in[:, m0:].astype(BF16),
        "norm_mix": p["norm_mix"][l].reshape(1, D), "norm_ffn": p["norm_ffn"][l].reshape(1, D),
        "mu": p["rwkv_mu"][l].reshape(1, c0),
        "kk": p["rwkv_kk"][l].reshape(1, Wd), "ka": p["rwkv_ka"][l].reshape(1, Wd),
        "rk": p["rwkv_rk"][l].reshape(1, Wd),
        "w0": p["rwkv_w0"][l].reshape(1, 2 * Wd), "a0": p["rwkv_a0"][l].reshape(1, 2 * Wd),
        "w2": w2, "a2": a2, "g2": p["rwkv_g2"][l],
        "ln_g": p["rwkv_ln_g"][l].reshape(1, Wd), "ln_b": p["rwkv_ln_b"][l].reshape(1, Wd),
        "seg64": _seg_matrix(Wd, RWKV_HEAD_DIM),
        "qn": jnp.tile(p["attn_qn"][l], N_HEADS).reshape(1, ATT_WIDTH),
        "kn": jnp.tile(p["attn_kn"][l], KV_HEADS).reshape(1, KV_WIDTH),
        "gla_a2": ga2, "gla_ab": p["gla_ab"][l].reshape(1, 2 * GLA_K_WIDTH),
        "gla_norm": p["gla_norm"][l].reshape(1, GLA_DV),
        "p_rwkv": p["p_rwkv"][l].astype(BF16), "p_attn": p["p_attn"][l].astype(BF16),
        "p_gla": p["p_gla"][l].astype(BF16), "w_out": p["w_out"][l].astype(BF16),
        "router_w": jnp.concatenate([p["router_w"][l], jnp.zeros((D, LANES - N_EXPERTS), F32)], axis=1),
        "router_b": jnp.concatenate([p["router_b"][l], jnp.full((LANES - N_EXPERTS,), -jnp.inf, F32)]).reshape(1, LANES),
        "moe_w1": p["moe_w1"][l].astype(BF16), "moe_b1": p["moe_b1"][l].reshape(N_EXPERTS, 1, 2 * D_EXPERT),
        "moe_w2": p["moe_w2"][l].astype(BF16), "moe_b2": p["moe_b2"][l].reshape(N_EXPERTS, 1, D_MODEL),
    }


def _trunk_layer(x, mod, lw, B, T, ctx):
    N = B * T
    rpc = N if mod.shape[0] == 1 else T
    zr = _inproj_call(x, mod, lw["norm_mix"], lw["w_r"], rpc, 640)
    za = _inproj_call(x, mod, lw["norm_mix"], lw["w_a"], rpc, 768)
    zg = _inproj_call(x, mod, lw["norm_mix"], lw["w_g"], rpc, 640)
    zm = _inproj_call(x, mod, lw["norm_mix"], lw["w_m"], rpc, 1024)

    r, v, kk, g, bon, lw0, lw1, k0, k1, b0, b1 = _rwkv_prep_call(zr, T, lw)
    if ctx is None:
        s0r = jnp.zeros((B, 2, RWKV_HEADS, RWKV_HEAD_DIM, RWKV_HEAD_DIM), F32)
        s0g = jnp.zeros((B, 2, GLA_HEADS, GLA_DV, GLA_DK), F32)
        cache, tabs = None, None
    else:
        s0r = ctx["rwkv"]
        s0g = jnp.swapaxes(ctx["gla"], -1, -2)
        cache, tabs = (ctx["k"], ctx["v"], ctx["layer"]), _rope_tables(T)
    yf, yb, s_r = _rwkv_scan_call((r, v, kk, lw0, lw1, k0, k1, b0, b1), s0r, B, T)
    q, kn, kr = _attn_prep_call(za, lw, T, tabs)
    oa = _attn_call(q, kr, za, cache, B, T)
    gf, gb, s_g = _gla_scan_call(zg, s0g, lw["gla_a2"], lw["gla_ab"], B, T)
    x1, h2, ti, tg = _merge_call(yf, yb, bon, g, oa, gf, gb, zg, zm, x, mod, lw, rpc)
    y4 = _moe_ffn(h2, ti[:, :TOP_K], tg[:, :TOP_K], lw)
    x2 = _combine_call(x1, mod, y4, rpc)
    return x2, (kn, za[:, ATT_WIDTH + KV_WIDTH:], s_r, jnp.swapaxes(s_g, -1, -2))


def kernel(x_prompt, x_sample, c, cache_k, cache_v, state_rwkv, state_gla, c_ctx, ada_w, ada_b, norm_mix, norm_ffn, w_in, rwkv_mu, rwkv_w0, rwkv_w2, rwkv_a0, rwkv_a2, rwkv_g2, rwkv_kk, rwkv_ka, rwkv_rk, rwkv_ln_g, rwkv_ln_b, attn_qn, attn_kn, gla_a2, gla_ab, gla_norm, p_rwkv, p_attn, p_gla, w_out, router_w, router_b, moe_w1, moe_b1, moe_w2, moe_b2):
    p = {"w_in": w_in, "norm_mix": norm_mix, "norm_ffn": norm_ffn, "rwkv_mu": rwkv_mu, "rwkv_w0": rwkv_w0,
         "rwkv_w2": rwkv_w2, "rwkv_a0": rwkv_a0, "rwkv_a2": rwkv_a2, "rwkv_g2": rwkv_g2, "rwkv_kk": rwkv_kk,
         "rwkv_ka": rwkv_ka, "rwkv_rk": rwkv_rk, "rwkv_ln_g": rwkv_ln_g, "rwkv_ln_b": rwkv_ln_b,
         "attn_qn": attn_qn, "attn_kn": attn_kn, "gla_a2": gla_a2, "gla_ab": gla_ab, "gla_norm": gla_norm,
         "p_rwkv": p_rwkv, "p_attn": p_attn, "p_gla": p_gla, "w_out": w_out, "router_w": router_w,
         "router_b": router_b, "moe_w1": moe_w1, "moe_b1": moe_b1, "moe_w2": moe_w2, "moe_b2": moe_b2}
    Bp, Tp, D = x_prompt.shape
    Bs, Ts, _ = x_sample.shape
    L = ada_w.shape[0]
    P = cache_k.shape[2]

    cond8 = jnp.concatenate([c_ctx[None, :], c, jnp.zeros((8 - 1 - Bs, D), F32)], axis=0)
    mod = _ada_call(cond8, ada_w, ada_b).reshape(L, 8, 6, D)
    weights = [_layer_weights(p, l) for l in range(L)]
    ck = cache_k.reshape(Bs, L, P, KV_WIDTH)
    cv = cache_v.reshape(Bs, L, P, KV_WIDTH)

    yp = x_prompt.reshape(Bp * Tp, D)
    ks, vs, srs, sgs = [], [], [], []
    for l in range(L):
        yp, (kn, vh, s_r, s_g) = _trunk_layer(yp, mod[l, 0:1], weights[l], Bp, Tp, None)
        ks.append(kn.reshape(Bp, Tp, KV_HEADS, HEAD_DIM))
        vs.append(vh.reshape(Bp, Tp, KV_HEADS, HEAD_DIM))
        srs.append(s_r)
        sgs.append(s_g)

    ys = x_sample.reshape(Bs * Ts, D)
    for l in range(L):
        ctx = {"k": ck, "v": cv, "layer": l, "rwkv": state_rwkv[:, l], "gla": state_gla[:, l]}
        ys, _ = _trunk_layer(ys, mod[l, 1:1 + Bs], weights[l], Bs, Ts, ctx)

    return (yp.reshape(Bp, Tp, D), ys.reshape(Bs, Ts, D),
            jnp.stack(ks, axis=1), jnp.stack(vs, axis=1),
            jnp.stack(srs, axis=1), jnp.stack(sgs, axis=1))
```

```python
import functools

import jax
import jax.numpy as jnp
from jax import lax
from jax.experimental import pallas as pl
from jax.experimental.pallas import tpu as pltpu

D_MODEL = 1024
DEPTH = 2
GRID_W = 64
NORM_EPS = 1e-6

RWKV_HEADS = 8
RWKV_HEAD_DIM = 64
RWKV_WIDTH = RWKV_HEADS * RWKV_HEAD_DIM
DECAY_LORA = 64
ICLR_LORA = 64
GATE_LORA = 128
RWKV_GN_EPS = 6.4e-4
RWKV_COLS = 3 * RWKV_WIDTH + 2 * DECAY_LORA + 2 * ICLR_LORA + GATE_LORA

N_HEADS = 8
KV_HEADS = 2
HEAD_DIM = 64
ATT_WIDTH = N_HEADS * HEAD_DIM
KV_WIDTH = KV_HEADS * HEAD_DIM
ROPE_THETA = 10000.0

GLA_HEADS = 4
GLA_DK = 128
GLA_DV = 256
GLA_K_WIDTH = GLA_HEADS * GLA_DK
GLA_V_WIDTH = GLA_HEADS * GLA_DV
GLA_LORA = 16
GLA_GATE_NORMALIZER = 16.0

N_EXPERTS = 32
TOP_K = 4
D_EXPERT = 1024
SWIGLU_ALPHA = 1.702
SWIGLU_LIMIT = 7.0
EXPERT_BLOCK = 256

LANES = 128
CHUNK = 64
ROW_TILE = 256
INPROJ_TILE = 512
Q_TILE = 128
VMEM_LIMIT = 52 * 1024 * 1024

F32 = jnp.float32
BF16 = jnp.bfloat16
HI = lax.Precision.HIGHEST


def _dot(a, b, prec=None):
    return jnp.dot(a, b, preferred_element_type=F32, precision=prec)


def _dot_nt(a, b, prec=None):
    return lax.dot_general(a, b, (((1,), (1,)), ((), ())), preferred_element_type=F32, precision=prec)


def _dot_tn(a, b, prec=None):
    return lax.dot_general(a, b, (((0,), (0,)), ((), ())), preferred_element_type=F32, precision=prec)


_NN = (((1,), (0,)), ((), ()))
_NT = (((1,), (1,)), ((), ()))
_TN = (((0,), (0,)), ((), ()))


def _dg(a, b, dn):
    return lax.dot_general(a, b, dn, preferred_element_type=F32)


def _split(x):
    h = x.astype(BF16)
    return h, (x - h.astype(F32)).astype(BF16)


def _mm3(a, b, dn=_NN):
    return _dg(a[0], b[0], dn) + _dg(a[0], b[1], dn) + _dg(a[1], b[0], dn)


def _cumsum_rows(mask, x):
    m = jnp.where(mask, 1.0, 0.0).astype(BF16)
    h = x.astype(BF16)
    r1 = x - h.astype(F32)
    mid = r1.astype(BF16)
    lo = (r1 - mid.astype(F32)).astype(BF16)
    return _dg(m, h, _NN) + _dg(m, mid, _NN) + _dg(m, lo, _NN)


def _seg_sum(x, seg_bf16):
    xh = x.astype(BF16)
    xl = (x - xh.astype(F32)).astype(BF16)
    return _dot(xh, seg_bf16) + _dot(xl, seg_bf16)


def _sigmoid(x):
    return 1.0 / (1.0 + jnp.exp(-x))


def _params(sem):
    return pltpu.CompilerParams(dimension_semantics=sem, vmem_limit_bytes=VMEM_LIMIT)


def _const_spec(shape):
    nd = len(shape)
    return pl.BlockSpec(shape, lambda *_: (0,) * nd)


def _ada_kernel(c_ref, w_ref, b_ref, o_ref):
    c = c_ref[...]
    s = c * _sigmoid(c)
    o_ref[...] = _mm3(_split(s), _split(w_ref[...])) + b_ref[...]


def _ada_call(cond8, ada_w, ada_b):
    L, D, W = ada_w.shape
    tn = 1536
    return pl.pallas_call(
        _ada_kernel,
        grid=(L, W // tn),
        in_specs=[pl.BlockSpec((8, D), lambda l, j: (0, 0)),
                  pl.BlockSpec((None, D, tn), lambda l, j: (l, 0, j)),
                  pl.BlockSpec((None, 1, tn), lambda l, j: (l, 0, j))],
        out_specs=pl.BlockSpec((None, 8, tn), lambda l, j: (l, 0, j)),
        out_shape=jax.ShapeDtypeStruct((L, 8, W), F32),
        compiler_params=_params(("parallel", "parallel")),
        name="ada",
    )(cond8, ada_w, ada_b.reshape(L, 1, W))


def _inproj_kernel(x_ref, m_ref, g_ref, w_ref, o_ref, h_scr):
    @pl.when(pl.program_id(1) == 0)
    def _():
        x = x_ref[...]
        y = x * lax.rsqrt(jnp.mean(x * x, axis=-1, keepdims=True) + NORM_EPS) * g_ref[...]
        h = y * (1.0 + m_ref[1:2, :]) + m_ref[0:1, :]
        h_scr[...] = h.astype(BF16)

    o_ref[...] = _dot(h_scr[...], w_ref[...])


def _inproj_call(x, mod, gain, w, rows_per_cond, tn):
    N, D = x.shape
    W = w.shape[1]
    tm = INPROJ_TILE
    return pl.pallas_call(
        _inproj_kernel,
        grid=(N // tm, W // tn),
        in_specs=[pl.BlockSpec((tm, D), lambda i, j: (i, 0)),
                  pl.BlockSpec((None, 6, D), lambda i, j: ((i * tm) // rows_per_cond, 0, 0)),
                  pl.BlockSpec((1, D), lambda i, j: (0, 0)),
                  pl.BlockSpec((D, tn), lambda i, j: (0, j))],
        out_specs=pl.BlockSpec((tm, tn), lambda i, j: (i, j)),
        out_shape=jax.ShapeDtypeStruct((N, W), F32),
        scratch_shapes=[pltpu.VMEM((tm, D), BF16)],
        compiler_params=_params(("parallel", "arbitrary")),
        name="inproj",
    )(x, mod, gain, w)


def _rwkv_prep_kernel(z_ref, zp_ref, zn_ref, mu_ref, kkp_ref, ka_ref, rk_ref, w0_ref, a0_ref,
                      w2_ref, a2_ref, g2_ref, seg_ref,
                      r_o, v_o, kk_o, g_o, bon_o, lw0_o, lw1_o, k0_o, k1_o, b0_o, b1_o,
                      *, tm, blocks_per_seq):
    i = pl.program_id(0)
    z = z_ref[...]
    row = lax.broadcasted_iota(jnp.int32, (tm, 1), 0)
    first = (i % blocks_per_seq) == 0
    last = (i % blocks_per_seq) == blocks_per_seq - 1
    pz = jnp.where(first, 0.0, zp_ref[7:8, :])
    nz = jnp.where(last, 0.0, zn_ref[0:1, :])
    prev = jnp.where(row == 0, pz, pltpu.roll(z, 1, 0))
    nxt = jnp.where(row == tm - 1, nz, pltpu.roll(z, tm - 1, 0))
    zm = z + mu_ref[...] * (0.5 * (prev + nxt) - z)

    Wd = RWKV_WIDTH
    r = zm[:, 0:Wd]
    k = zm[:, Wd:2 * Wd]
    v = zm[:, 2 * Wd:3 * Wd]
    lw = zm[:, 3 * Wd:3 * Wd + 128]
    la = zm[:, 3 * Wd + 128:3 * Wd + 256]
    lg = zm[:, 3 * Wd + 256:3 * Wd + 384]
    seg = seg_ref[...]

    kk = k * kkp_ref[...]
    kk = kk / jnp.maximum(jnp.sqrt(_seg_sum(kk * kk, seg)), 1e-12)
    w_raw = _mm3(_split(jnp.tanh(lw)), _split(w2_ref[...])) + w0_ref[...]
    logw = -_sigmoid(w_raw) * 0.6065306597126334
    a = _sigmoid(_mm3(_split(la), _split(a2_ref[...])) + a0_ref[...])
    ka = ka_ref[...]
    rk = rk_ref[...]
    bonus = jnp.zeros_like(r)
    for d, (lw_o, k_o, b_o) in enumerate(((lw0_o, k0_o, b0_o), (lw1_o, k1_o, b1_o))):
        a_d = a[:, d * Wd:(d + 1) * Wd]
        k_d = k * (1.0 + (a_d - 1.0) * ka)
        lw_o[...] = logw[:, d * Wd:(d + 1) * Wd]
        k_o[...] = k_d
        b_o[...] = kk * a_d
        bonus = bonus + _seg_sum(r * k_d * rk, seg) * v
    r_o[...] = r
    v_o[...] = v
    kk_o[...] = kk
    bon_o[...] = bonus
    g_o[...] = _mm3(_split(_sigmoid(lg)), _split(g2_ref[...]))


def _rwkv_prep_call(zr, seq_len, lw):
    N, Wz = zr.shape
    tm = ROW_TILE
    bps = seq_len // tm
    Wd = RWKV_WIDTH
    nb8 = N // 8
    row_spec = pl.BlockSpec((tm, Wd), lambda i: (i, 0))
    out = jax.ShapeDtypeStruct((N, Wd), F32)
    return pl.pallas_call(
        functools.partial(_rwkv_prep_kernel, tm=tm, blocks_per_seq=bps),
        grid=(N // tm,),
        in_specs=[pl.BlockSpec((tm, Wz), lambda i: (i, 0)),
                  pl.BlockSpec((8, Wz), lambda i: (jnp.maximum(i * (tm // 8) - 1, 0), 0)),
                  pl.BlockSpec((8, Wz), lambda i: (jnp.minimum((i + 1) * (tm // 8), nb8 - 1), 0)),
                  _const_spec((1, Wz)), _const_spec((1, Wd)), _const_spec((1, Wd)), _const_spec((1, Wd)),
                  _const_spec((1, 2 * Wd)), _const_spec((1, 2 * Wd)),
                  _const_spec((128, 2 * Wd)), _const_spec((128, 2 * Wd)), _const_spec((128, Wd)),
                  _const_spec((Wd, Wd))],
        out_specs=[row_spec] * 11,
        out_shape=[out] * 11,
        compiler_params=_params(("parallel",)),
        name="rwkv_prep",
    )(zr, zr, zr, lw["mu"], lw["kk"], lw["ka"], lw["rk"], lw["w0"], lw["a0"],
      lw["w2"], lw["a2"], lw["g2"], lw["seg64"])


def _tri_masks(d):
    ti = lax.broadcasted_iota(jnp.int32, (CHUNK, CHUNK), 0)
    si = lax.broadcasted_iota(jnp.int32, (CHUNK, CHUNK), 1)
    if d == 0:
        return si <= ti, si < ti
    return si >= ti, si > ti


def _rwkv_scan_kernel(rf, vf, kkf, lwf, kf, bf, rb, vb, kkb, lwb, kb, bb, s0_ref,
                      yf_o, yb_o, st_o, s_scr, *, nc):
    i = pl.program_id(1)

    @pl.when(i == 0)
    def _():
        s_scr[...] = s0_ref[...]

    C = CHUNK
    eye = (lax.broadcasted_iota(jnp.int32, (C, C), 0)
           == lax.broadcasted_iota(jnp.int32, (C, C), 1)).astype(F32)
    H, Dh = RWKV_HEADS, RWKV_HEAD_DIM
    sls = [slice(h * Dh, (h + 1) * Dh) for h in range(H)]
    dirs = ((rf, vf, kkf, lwf, kf, bf), (rb, vb, kkb, lwb, kb, bb))
    pre = []
    for d, (r_ref, v_ref, kk_ref, lw_ref, k_ref, b_ref) in enumerate(dirs):
        incl, strict = _tri_masks(d)
        lw = lw_ref[...]
        b = _cumsum_rows(incl, lw)
        btot = jnp.sum(lw, axis=0, keepdims=True)
        r, v, kk, k, beta = r_ref[...], v_ref[...], kk_ref[...], k_ref[...], b_ref[...]
        nb = jnp.exp(-b)
        eb = jnp.exp(btot - b)
        gtot = jnp.exp(btot)
        ar = _split(jnp.concatenate([-kk * jnp.exp(b - lw), r * jnp.exp(b)], axis=0))
        bbar = _split(beta * nb)
        kbar = (k * nb).astype(BF16)
        hat = jnp.concatenate([beta * eb, k * eb], axis=0).astype(BF16)
        v16 = v.astype(BF16)

        g1 = [_mm3((ar[0][:, s], ar[1][:, s]), (bbar[0][:, s], bbar[1][:, s]), _NT) for s in sls]
        g2 = [_dg(ar[0][:, s], kbar[:, s], _NT) for s in sls]
        a_ab = [jnp.where(strict, g[:C], 0.0) for g in g1]
        a_rb = [jnp.where(incl, g[C:], 0.0).astype(BF16) for g in g1]
        a_ak = [jnp.where(strict, g[:C], 0.0).astype(BF16) for g in g2]
        a_rk = [jnp.where(incl, g[C:], 0.0).astype(BF16) for g in g2]
        av = [_dg(a_ak[h], v16[:, sls[h]], _NN) for h in range(H)]
        p = [_split(a) for a in a_ab]
        t = [eye + a for a in a_ab]
        for _ in range(5):
            p = [_split(_mm3(pp, pp)) for pp in p]
            t = [tt + _mm3(_split(tt), pp) for tt, pp in zip(t, p)]
        pre.append((ar[0], av, [_split(tt) for tt in t], a_rb, a_rk, v16, hat, gtot))

    hd = [(d, h) for d in range(2) for h in range(H)]
    s_old = [s_scr[d, h] for d, h in hd]
    z = [_dg(pre[d][0][:, sls[h]], s.astype(BF16), _NT) for (d, h), s in zip(hd, s_old)]
    u = [_mm3(pre[d][2][h], _split(zz[:C] + pre[d][1][h])).astype(BF16) for (d, h), zz in zip(hd, z)]
    ys = [zz[C:] + _dg(pre[d][3][h], uu, _NN) + _dg(pre[d][4][h], pre[d][5][:, sls[h]], _NN)
          for (d, h), zz, uu in zip(hd, z, u)]
    s_new = [s * pre[d][7][:, sls[h]]
             + _dg(jnp.concatenate([uu, pre[d][5][:, sls[h]]], axis=0), pre[d][6][:, sls[h]], _TN)
             for (d, h), s, uu in zip(hd, s_old, u)]
    for (d, h), yy, ss in zip(hd, ys, s_new):
        (yf_o, yb_o)[d][:, sls[h]] = yy
        s_scr[d, h] = ss

    @pl.when(i == nc - 1)
    def _():
        st_o[...] = s_scr[...]


def _rwkv_scan_call(prep, s0, B, T):
    r, v, kk, lw0, lw1, k0, k1, b0, b1 = prep
    N, Wd = r.shape
    nc = T // CHUNK
    fwd = pl.BlockSpec((CHUNK, Wd), lambda b, i: (b * nc + i, 0))
    bwd = pl.BlockSpec((CHUNK, Wd), lambda b, i: (b * nc + nc - 1 - i, 0))
    st = pl.BlockSpec((None, 2, RWKV_HEADS, RWKV_HEAD_DIM, RWKV_HEAD_DIM), lambda b, i: (b, 0, 0, 0, 0))
    y = jax.ShapeDtypeStruct((N, Wd), F32)
    return pl.pallas_call(
        functools.partial(_rwkv_scan_kernel, nc=nc),
        grid=(B, nc),
        in_specs=[fwd] * 6 + [bwd] * 6 + [st],
        out_specs=[fwd, bwd, st],
        out_shape=[y, y, jax.ShapeDtypeStruct(s0.shape, F32)],
        scratch_shapes=[pltpu.VMEM((2, RWKV_HEADS, RWKV_HEAD_DIM, RWKV_HEAD_DIM), F32)],
        compiler_params=_params(("parallel", "arbitrary")),
        name="rwkv_scan",
    )(r, v, kk, lw0, k0, b0, r, v, kk, lw1, k1, b1, s0)


def _gla_scan_kernel(xf, glf, xb, glb, a2_ref, ab_ref, s0_ref, of_o, ob_o, st_o, s_scr, *, nc):
    i = pl.program_id(1)

    @pl.when(i == 0)
    def _():
        s_scr[...] = s0_ref[...]

    Kw = GLA_K_WIDTH
    work = []
    for d, (x_ref, gl_ref) in enumerate(((xf, glf), (xb, glb))):
        incl, _ = _tri_masks(d)
        xa = (_mm3(_split(gl_ref[...]), _split(a2_ref[:, d * Kw:(d + 1) * Kw]))
              + ab_ref[:, d * Kw:(d + 1) * Kw])
        log_a = (jnp.minimum(xa, 0.0) - jnp.log1p(jnp.exp(-jnp.abs(xa)))) * (1.0 / GLA_GATE_NORMALIZER)
        b = _cumsum_rows(incl, log_a)
        bl = jnp.sum(log_a, axis=0, keepdims=True)
        q = x_ref[:, 0:Kw] * (GLA_DK ** -0.5)
        k = x_ref[:, Kw:2 * Kw]
        q_in = (q * jnp.exp(b)).astype(BF16)
        k_in = (k * jnp.exp(-b)).astype(BF16)
        k_end = (k * jnp.exp(bl - b)).astype(BF16)
        gtot = jnp.exp(bl)
        for h in range(GLA_HEADS):
            ks = slice(h * GLA_DK, (h + 1) * GLA_DK)
            v_h = x_ref[:, 2 * Kw + h * GLA_DV:2 * Kw + (h + 1) * GLA_DV].astype(BF16)
            att = jnp.where(incl, _dg(q_in[:, ks], k_in[:, ks], _NT), 0.0).astype(BF16)
            work.append((d, h, _dg(att, v_h, _NN), q_in[:, ks], gtot[:, ks], _dg(v_h, k_end[:, ks], _TN)))

    s_old = [s_scr[d, h] for d, h, *_ in work]
    outs = [o_in + _dg(qh, s.astype(BF16), _NT) for (_, _, o_in, qh, _, _), s in zip(work, s_old)]
    for (d, h, _, _, g, kv), s, o in zip(work, s_old, outs):
        (of_o, ob_o)[d][:, h * GLA_DV:(h + 1) * GLA_DV] = o
        s_scr[d, h] = s * g + kv

    @pl.when(i == nc - 1)
    def _():
        st_o[...] = s_scr[...]


def _gla_scan_call(zg, s0t, a2, ab, B, T):
    N = zg.shape[0]
    nc = T // CHUNK
    Wx = 2 * GLA_K_WIDTH + GLA_V_WIDTH
    gl_col = (Wx + GLA_V_WIDTH) // LANES
    xf = pl.BlockSpec((CHUNK, Wx), lambda b, i: (b * nc + i, 0))
    xb = pl.BlockSpec((CHUNK, Wx), lambda b, i: (b * nc + nc - 1 - i, 0))
    gf = pl.BlockSpec((CHUNK, LANES), lambda b, i: (b * nc + i, gl_col))
    gb = pl.BlockSpec((CHUNK, LANES), lambda b, i: (b * nc + nc - 1 - i, gl_col))
    of = pl.BlockSpec((CHUNK, GLA_V_WIDTH), lambda b, i: (b * nc + i, 0))
    ob = pl.BlockSpec((CHUNK, GLA_V_WIDTH), lambda b, i: (b * nc + nc - 1 - i, 0))
    st = pl.BlockSpec((None, 2, GLA_HEADS, GLA_DV, GLA_DK), lambda b, i: (b, 0, 0, 0, 0))
    o = jax.ShapeDtypeStruct((N, GLA_V_WIDTH), F32)
    return pl.pallas_call(
        functools.partial(_gla_scan_kernel, nc=nc),
        grid=(B, nc),
        in_specs=[xf, gf, xb, gb, _const_spec((LANES, 2 * GLA_K_WIDTH)), _const_spec((1, 2 * GLA_K_WIDTH)), st],
        out_specs=[of, ob, st],
        out_shape=[o, o, jax.ShapeDtypeStruct(s0t.shape, F32)],
        scratch_shapes=[pltpu.VMEM((2, GLA_HEADS, GLA_DV, GLA_DK), F32)],
        compiler_params=_params(("parallel", "arbitrary")),
        name="gla_scan",
    )(zg, zg, zg, zg, a2, ab, s0t)


def _rope(x, cos, sin):
    n = x.shape[1]
    lane = lax.broadcasted_iota(jnp.int32, x.shape, 1)
    up = pltpu.roll(x, n - 16, 1)
    dn = pltpu.roll(x, 16, 1)
    sw = jnp.where((lane % 32) < 16, up, dn)
    reps = n // LANES
    if reps > 1:
        cos = jnp.concatenate([cos] * reps, axis=1)
        sin = jnp.concatenate([sin] * reps, axis=1)
    return x * cos + sw * sin


def _attn_prep_kernel(*refs, rope):
    if rope:
        z_ref, qn_ref, kn_ref, segq_ref, segk_ref, cos_ref, sin_ref, q_o, kn_o, kr_o = refs
    else:
        z_ref, qn_ref, kn_ref, segq_ref, segk_ref, q_o, kn_o, kr_o = refs
    zq = z_ref[:, 0:ATT_WIDTH]
    zk = z_ref[:, ATT_WIDTH:ATT_WIDTH + KV_WIDTH]
    qh = zq * lax.rsqrt(_seg_sum(zq * zq, segq_ref[...]) * (1.0 / HEAD_DIM) + NORM_EPS) * qn_ref[...]
    kh = zk * lax.rsqrt(_seg_sum(zk * zk, segk_ref[...]) * (1.0 / HEAD_DIM) + NORM_EPS) * kn_ref[...]
    kn_o[...] = kh
    if rope:
        qh = _rope(qh, cos_ref[...], sin_ref[...])
        kh = _rope(kh, cos_ref[...], sin_ref[...])
    q_o[...] = (qh * (HEAD_DIM ** -0.5)).astype(BF16)
    kr_o[...] = kh.astype(BF16)


def _attn_prep_call(za, lw, seq_len, rope_tabs):
    N, Wz = za.shape
    tm = ROW_TILE
    bps = seq_len // tm
    rope = rope_tabs is not None
    in_specs = [pl.BlockSpec((tm, Wz), lambda i: (i, 0)),
                _const_spec((1, ATT_WIDTH)), _const_spec((1, KV_WIDTH)),
                _const_spec((ATT_WIDTH, ATT_WIDTH)), _const_spec((KV_WIDTH, KV_WIDTH))]
    args = [za, lw["qn"], lw["kn"], lw["seg64"], lw["seg64"][:KV_WIDTH, :KV_WIDTH]]
    if rope:
        in_specs += [pl.BlockSpec((tm, LANES), lambda i: (i % bps, 0))] * 2
        args += list(rope_tabs)
    return pl.pallas_call(
        functools.partial(_attn_prep_kernel, rope=rope),
        grid=(N // tm,),
        in_specs=in_specs,
        out_specs=[pl.BlockSpec((tm, ATT_WIDTH), lambda i: (i, 0)),
                   pl.BlockSpec((tm, KV_WIDTH), lambda i: (i, 0)),
                   pl.BlockSpec((tm, KV_WIDTH), lambda i: (i, 0))],
        out_shape=[jax.ShapeDtypeStruct((N, ATT_WIDTH), BF16),
                   jax.ShapeDtypeStruct((N, KV_WIDTH), F32),
                   jax.ShapeDtypeStruct((N, KV_WIDTH), BF16)],
        compiler_params=_params(("parallel",)),
        name="attn_prep",
    )(*args)


def _attn_kernel(*refs, cached):
    if cached:
        q_ref, k_ref, v_ref, ck_ref, cv_ref, o_ref = refs
    else:
        q_ref, k_ref, v_ref, o_ref = refs
    G = N_HEADS // KV_HEADS
    for g in range(KV_HEADS):
        gs = slice(g * HEAD_DIM, (g + 1) * HEAD_DIM)
        kg = k_ref[:, gs]
        vg = v_ref[:, gs].astype(BF16)
        if cached:
            ckg = ck_ref[:, gs].astype(BF16)
            cvg = cv_ref[:, gs].astype(BF16)
        for hh in range(G):
            hs = slice((g * G + hh) * HEAD_DIM, (g * G + hh + 1) * HEAD_DIM)
            qh = q_ref[:, hs]
            s1 = _dot_nt(qh, kg)
            m = jnp.max(s1, axis=-1, keepdims=True)
            if cached:
                s2 = _dot_nt(qh, ckg)
                m = jnp.maximum(m, jnp.max(s2, axis=-1, keepdims=True))
            p1 = jnp.exp(s1 - m)
            l = jnp.sum(p1, axis=-1, keepdims=True)
            o = _dot(p1.astype(BF16), vg)
            if cached:
                p2 = jnp.exp(s2 - m)
                l = l + jnp.sum(p2, axis=-1, keepdims=True)
                o = o + _dot(p2.astype(BF16), cvg)
            o_ref[:, hs] = o / l


def _attn_call(q, k, za, cache, B, T):
    N = q.shape[0]
    tq = Q_TILE
    nq = T // tq
    v_col = (ATT_WIDTH + KV_WIDTH) // KV_WIDTH
    in_specs = [pl.BlockSpec((tq, ATT_WIDTH), lambda b, i: (b * nq + i, 0)),
                pl.BlockSpec((T, KV_WIDTH), lambda b, i: (b, 0)),
                pl.BlockSpec((T, KV_WIDTH), lambda b, i: (b, v_col))]
    args = [q, k, za]
    if cache is not None:
        ck, cv, layer = cache
        P = ck.shape[2]
        cspec = pl.BlockSpec((None, None, P, KV_WIDTH), lambda b, i: (b, layer, 0, 0))
        in_specs += [cspec, cspec]
        args += [ck, cv]
    return pl.pallas_call(
        functools.partial(_attn_kernel, cached=cache is not None),
        grid=(B, nq),
        in_specs=in_specs,
        out_specs=pl.BlockSpec((tq, ATT_WIDTH), lambda b, i: (b * nq + i, 0)),
        out_shape=jax.ShapeDtypeStruct((N, ATT_WIDTH), F32),
        compiler_params=_params(("parallel", "arbitrary")),
        name="attention",
    )(*args)


def _merge_kernel(yf, yb, bon, g, oa, gf, gb, gr, mr, ma, mg, x_ref, m_ref,
                  lng, lnb, gn, nf, seg_ref, pr, pa, pg, wo, rw, rb,
                  x1_o, h2_o, ti_o, tg_o):
    seg = seg_ref[...]
    inv = 1.0 / RWKV_HEAD_DIM
    y = yf[...] + yb[...]
    mu = _seg_sum(y, seg) * inv
    yc = y - mu
    var = _seg_sum(yc * yc, seg) * inv
    o_r = (yc * lax.rsqrt(var + RWKV_GN_EPS) * lng[...] + lnb[...] + bon[...]) * g[...]

    gate_r = gr[...]
    silu_r = gate_r * _sigmoid(gate_r)
    gnv = gn[...]
    cols = []
    for h in range(GLA_HEADS):
        vs = slice(h * GLA_DV, (h + 1) * GLA_DV)
        o = gf[:, vs] + gb[:, vs]
        ms = jnp.mean(o * o, axis=-1, keepdims=True)
        cols.append(o * lax.rsqrt(ms + NORM_EPS) * gnv * silu_r[:, vs])
    o_g = jnp.concatenate(cols, axis=1)

    merged = (_sigmoid(mr[...]) * _dot(o_r.astype(BF16), pr[...])
              + _sigmoid(ma[...]) * _dot(oa[...].astype(BF16), pa[...])
              + _sigmoid(mg[...]) * _dot(o_g.astype(BF16), pg[...]))
    x1 = x_ref[...] + m_ref[2:3, :] * _dot(merged.astype(BF16), wo[...])
    x1_o[...] = x1
    hn = x1 * lax.rsqrt(jnp.mean(x1 * x1, axis=-1, keepdims=True) + NORM_EPS) * nf[...]
    h2 = hn * (1.0 + m_ref[4:5, :]) + m_ref[3:4, :]
    h2_o[...] = h2

    logits = _mm3(_split(h2), _split(rw[...])) + rb[...]
    lane = lax.broadcasted_iota(jnp.int32, logits.shape, 1)
    vals, idxs = [], []
    for _ in range(TOP_K):
        m = jnp.max(logits, axis=-1, keepdims=True)
        idx = jnp.min(jnp.where(logits == m, lane, LANES), axis=-1, keepdims=True)
        vals.append(m)
        idxs.append(idx)
        logits = jnp.where(lane == idx, -jnp.inf, logits)
    es = [jnp.exp(vv - vals[0]) for vv in vals]
    den = es[0] + es[1] + es[2] + es[3]
    ti = jnp.zeros(logits.shape, jnp.int32)
    tg = jnp.zeros(logits.shape, F32)
    for j in range(TOP_K):
        ti = jnp.where(lane == j, idxs[j], ti)
        tg = jnp.where(lane == j, es[j] / den, tg)
    ti_o[...] = ti
    tg_o[...] = tg


def _merge_call(yf, yb, bon, g, oa, gf, gb, zg, zm, x, mod, lw, rows_per_cond):
    N, D = x.shape
    tm = ROW_TILE
    Wd = RWKV_WIDTH

    def rows(w, col=0):
        return pl.BlockSpec((tm, w), lambda i: (i, col))

    in_specs = [rows(Wd)] * 4 + [rows(ATT_WIDTH), rows(GLA_V_WIDTH), rows(GLA_V_WIDTH),
                                 rows(GLA_V_WIDTH, 2), rows(D, 0), rows(D, 1), rows(D, 2), rows(D),
                                 pl.BlockSpec((None, 6, D), lambda i: ((i * tm) // rows_per_cond, 0, 0)),
                                 _const_spec((1, Wd)), _const_spec((1, Wd)), _const_spec((1, GLA_DV)),
                                 _const_spec((1, D)), _const_spec((Wd, Wd)),
                                 _const_spec((Wd, D)), _const_spec((ATT_WIDTH, D)), _const_spec((GLA_V_WIDTH, D)),
                                 _const_spec((D, D)), _const_spec((D, LANES)), _const_spec((1, LANES))]
    return pl.pallas_call(
        _merge_kernel,
        grid=(N // tm,),
        in_specs=in_specs,
        out_specs=[rows(D), rows(D), rows(LANES), rows(LANES)],
        out_shape=[jax.ShapeDtypeStruct((N, D), F32), jax.ShapeDtypeStruct((N, D), F32),
                   jax.ShapeDtypeStruct((N, LANES), jnp.int32), jax.ShapeDtypeStruct((N, LANES), F32)],
        compiler_params=_params(("parallel",)),
        name="merge",
    )(yf, yb, bon, g, oa, gf, gb, zg, zm, zm, zm, x, mod,
      lw["ln_g"], lw["ln_b"], lw["gla_norm"], lw["norm_ffn"], lw["seg64"],
      lw["p_rwkv"], lw["p_attn"], lw["p_gla"], lw["w_out"], lw["router_w"], lw["router_b"])


def _moe_kernel(be_ref, na_ref, x_ref, gate_ref, w1_ref, b1_ref, w2_ref, b2_ref, o_ref):
    i = pl.program_id(0)

    @pl.when(i < na_ref[0])
    def _():
        z = _dot(x_ref[...].astype(BF16), w1_ref[...]) + b1_ref[...]
        glu = jnp.minimum(z[:, :D_EXPERT], SWIGLU_LIMIT)
        lin = jnp.clip(z[:, D_EXPERT:], -SWIGLU_LIMIT, SWIGLU_LIMIT)
        act = glu * _sigmoid(SWIGLU_ALPHA * glu) * (lin + 1.0)
        o_ref[...] = (_dot(act.astype(BF16), w2_ref[...]) + b2_ref[...]) * gate_ref[...]

    @pl.when(i >= na_ref[0])
    def _():
        o_ref[...] = jnp.zeros_like(o_ref)


def _moe_call(xg, row_gate, block_exp, n_active, w1, b1, w2, b2):
    R, D = xg.shape
    tb = EXPERT_BLOCK
    F2 = w1.shape[2]
    grid_spec = pltpu.PrefetchScalarGridSpec(
        num_scalar_prefetch=2,
        grid=(R // tb,),
        in_specs=[pl.BlockSpec((tb, D), lambda i, be, na: (i, 0)),
                  pl.BlockSpec((tb, 1), lambda i, be, na: (i, 0)),
                  pl.BlockSpec((None, D, F2), lambda i, be, na: (be[i], 0, 0)),
                  pl.BlockSpec((None, 1, F2), lambda i, be, na: (be[i], 0, 0)),
                  pl.BlockSpec((None, F2 // 2, D), lambda i, be, na: (be[i], 0, 0)),
                  pl.BlockSpec((None, 1, D), lambda i, be, na: (be[i], 0, 0))],
        out_specs=pl.BlockSpec((tb, D), lambda i, be, na: (i, 0)),
    )
    return pl.pallas_call(
        _moe_kernel,
        grid_spec=grid_spec,
        out_shape=jax.ShapeDtypeStruct((R, D), F32),
        compiler_params=_params(("arbitrary",)),
        name="moe_experts",
    )(block_exp, n_active, xg, row_gate, w1, b1, w2, b2)


def _combine_kernel(x1_ref, m_ref, y_ref, o_ref):
    D = D_MODEL
    y = (y_ref[:, 0:D] + y_ref[:, D:2 * D]) + (y_ref[:, 2 * D:3 * D] + y_ref[:, 3 * D:4 * D])
    o_ref[...] = x1_ref[...] + m_ref[5:6, :] * y


def _combine_call(x1, mod, y4, rows_per_cond):
    N, D = x1.shape
    tm = ROW_TILE
    return pl.pallas_call(
        _combine_kernel,
        grid=(N // tm,),
        in_specs=[pl.BlockSpec((tm, D), lambda i: (i, 0)),
                  pl.BlockSpec((None, 6, D), lambda i: ((i * tm) // rows_per_cond, 0, 0)),
                  pl.BlockSpec((tm, TOP_K * D), lambda i: (i, 0))],
        out_specs=pl.BlockSpec((tm, D), lambda i: (i, 0)),
        out_shape=jax.ShapeDtypeStruct((N, D), F32),
        compiler_params=_params(("parallel",)),
        name="combine",
    )(x1, mod, y4)


def _moe_ffn(h2, top_i, top_g, lw):
    N, D = h2.shape
    n_as = N * TOP_K
    n_blocks = -(-n_as // EXPERT_BLOCK) + N_EXPERTS
    R = n_blocks * EXPERT_BLOCK
    ex = jnp.arange(N_EXPERTS, dtype=jnp.int32)

    def lut(onehot, table):
        return jnp.sum(jnp.where(onehot, table[None, :], 0), axis=1)

    flat_e = top_i.reshape(n_as)
    flat_g = top_g.reshape(n_as)
    oh_a = flat_e[:, None] == ex[None, :]
    counts = jnp.sum(oh_a, axis=0, dtype=jnp.int32)
    starts = jnp.cumsum(counts) - counts
    padded = (counts + EXPERT_BLOCK - 1) // EXPERT_BLOCK * EXPERT_BLOCK
    pends = jnp.cumsum(padded)
    pstarts = pends - padded
    order = jnp.argsort(flat_e).astype(jnp.int32)
    rank = jnp.argsort(order).astype(jnp.int32)
    pos = rank + lut(oh_a, pstarts - starts)
    blk_start = jnp.arange(n_blocks, dtype=jnp.int32) * EXPERT_BLOCK
    block_exp = jnp.minimum(jnp.sum(pends[None, :] <= blk_start[:, None], axis=1),
                            N_EXPERTS - 1).astype(jnp.int32)
    n_active = (pends[-1] // EXPERT_BLOCK).astype(jnp.int32).reshape(1)
    oh_p = jnp.repeat(block_exp, EXPERT_BLOCK)[:, None] == ex[None, :]
    idx = jnp.arange(R, dtype=jnp.int32) - lut(oh_p, pstarts)
    valid = (idx >= 0) & (idx < lut(oh_p, counts))
    asg = order[jnp.clip(lut(oh_p, starts) + idx, 0, n_as - 1)]
    row_tok = jnp.where(valid, asg // TOP_K, 0)
    row_gate = jnp.where(valid, flat_g[asg], 0.0)
    out = _moe_call(h2[row_tok], row_gate.reshape(R, 1), block_exp, n_active,
                    lw["moe_w1"], lw["moe_b1"], lw["moe_w2"], lw["moe_b2"])
    return out[pos].reshape(N, TOP_K * D)


def _seg_matrix(n, seg):
    idx = jnp.arange(n) // seg
    return (idx[:, None] == idx[None, :]).astype(BF16)


def _rope_tables(T):
    quarter = HEAD_DIM // 4
    inv_freq = ROPE_THETA ** (-jnp.arange(quarter, dtype=F32) / quarter)
    t = jnp.arange(T)
    row = (t // GRID_W).astype(F32)
    col = (t % GRID_W).astype(F32)
    ang_r = row[:, None] * inv_freq
    ang_c = col[:, None] * inv_freq
    cos = jnp.concatenate([jnp.cos(ang_r)] * 2 + [jnp.cos(ang_c)] * 2, axis=1)
    sin = jnp.concatenate([-jnp.sin(ang_r), jnp.sin(ang_r), -jnp.sin(ang_c), jnp.sin(ang_c)], axis=1)
    return jnp.concatenate([cos, cos], axis=1), jnp.concatenate([sin, sin], axis=1)


def _layer_weights(p, l):
    Wd = RWKV_WIDTH
    w_in = p["w_in"][l]
    c0 = RWKV_COLS
    aq, ak, av, gq, gk, gv, gl, gr = (c0, c0 + 512, c0 + 640, c0 + 768, c0 + 1280, c0 + 1792, c0 + 2816, c0 + 2848)
    m0 = gr + GLA_V_WIDTH
    D = D_MODEL
    w_g = jnp.concatenate([w_in[:, gq:gl], w_in[:, gr:m0], w_in[:, gl:gr],
                           jnp.zeros((D, LANES - 2 * GLA_LORA), F32)], axis=1)
    z64 = jnp.zeros((DECAY_LORA, Wd), F32)
    w2 = jnp.concatenate([jnp.concatenate([p["rwkv_w2"][l, 0], z64], axis=1),
                          jnp.concatenate([z64, p["rwkv_w2"][l, 1]], axis=1)], axis=0)
    a2 = jnp.concatenate([jnp.concatenate([p["rwkv_a2"][l, 0], z64], axis=1),
                          jnp.concatenate([z64, p["rwkv_a2"][l, 1]], axis=1)], axis=0)
    z16 = jnp.zeros((GLA_LORA, GLA_K_WIDTH), F32)
    ga2 = jnp.concatenate([jnp.concatenate([p["gla_a2"][l, 0], z16], axis=1),
                           jnp.concatenate([z16, p["gla_a2"][l, 1]], axis=1),
                           jnp.zeros((LANES - 2 * GLA_LORA, 2 * GLA_K_WIDTH), F32)], axis=0)
    return {
        "w_r": w_in[:, :c0].astype(BF16), "w_a": w_in[:, aq:gq].astype(BF16),
        "w_g": w_g.astype(BF16), "w_m": w_in[:, m0:].astype(BF16),
        "norm_mix": p["norm_mix"][l].reshape(1, D), "norm_ffn": p["norm_ffn"][l].reshape(1, D),
        "mu": p["rwkv_mu"][l].reshape(1, c0),
        "kk": p["rwkv_kk"][l].reshape(1, Wd), "ka": p["rwkv_ka"][l].reshape(1, Wd),
        "rk": p["rwkv_rk"][l].reshape(1, Wd),
        "w0": p["rwkv_w0"][l].reshape(1, 2 * Wd), "a0": p["rwkv_a0"][l].reshape(1, 2 * Wd),
        "w2": w2, "a2": a2, "g2": p["rwkv_g2"][l],
        "ln_g": p["rwkv_ln_g"][l].reshape(1, Wd), "ln_b": p["rwkv_ln_b"][l].reshape(1, Wd),
        "seg64": _seg_matrix(Wd, RWKV_HEAD_DIM),
        "qn": jnp.tile(p["attn_qn"][l], N_HEADS).reshape(1, ATT_WIDTH),
        "kn": jnp.tile(p["attn_kn"][l], KV_HEADS).reshape(1, KV_WIDTH),
        "gla_a2": ga2, "gla_ab": p["gla_ab"][l].reshape(1, 2 * GLA_K_WIDTH),
        "gla_norm": p["gla_norm"][l].reshape(1, GLA_DV),
        "p_rwkv": p["p_rwkv"][l].astype(BF16), "p_attn": p["p_attn"][l].astype(BF16),
        "p_gla": p["p_gla"][l].astype(BF16), "w_out": p["w_out"][l].astype(BF16),
        "router_w": jnp.concatenate([p["router_w"][l], jnp.zeros((D, LANES - N_EXPERTS), F32)], axis=1),
        "router_b": jnp.concatenate([p["router_b"][l], jnp.full((LANES - N_EXPERTS,), -jnp.inf, F32)]).reshape(1, LANES),
        "moe_w1": p["moe_w1"][l].astype(BF16), "moe_b1": p["moe_b1"][l].reshape(N_EXPERTS, 1, 2 * D_EXPERT),
        "moe_w2": p["moe_w2"][l].astype(BF16), "moe_b2": p["moe_b2"][l].reshape(N_EXPERTS, 1, D_MODEL),
    }


def _trunk_layer(x, mod, lw, B, T, ctx):
    N = B * T
    rpc = N if mod.shape[0] == 1 else T
    zr, za, zg, zm = (_inproj_call(x, mod, lw["norm_mix"], lw[name], rpc, lw[name].shape[1])
                      for name in ("w_r", "w_a", "w_g", "w_m"))

    r, v, kk, g, bon, lw0, lw1, k0, k1, b0, b1 = _rwkv_prep_call(zr, T, lw)
    if ctx is None:
        s0r = jnp.zeros((B, 2, RWKV_HEADS, RWKV_HEAD_DIM, RWKV_HEAD_DIM), F32)
        s0g = jnp.zeros((B, 2, GLA_HEADS, GLA_DV, GLA_DK), F32)
        cache, tabs = None, None
    else:
        s0r = ctx["rwkv"]
        s0g = jnp.swapaxes(ctx["gla"], -1, -2)
        cache, tabs = (ctx["k"], ctx["v"], ctx["layer"]), _rope_tables(T)
    yf, yb, s_r = _rwkv_scan_call((r, v, kk, lw0, lw1, k0, k1, b0, b1), s0r, B, T)
    q, kn, kr = _attn_prep_call(za, lw, T, tabs)
    oa = _attn_call(q, kr, za, cache, B, T)
    gf, gb, s_g = _gla_scan_call(zg, s0g, lw["gla_a2"], lw["gla_ab"], B, T)
    x1, h2, ti, tg = _merge_call(yf, yb, bon, g, oa, gf, gb, zg, zm, x, mod, lw, rpc)
    y4 = _moe_ffn(h2, ti[:, :TOP_K], tg[:, :TOP_K], lw)
    x2 = _combine_call(x1, mod, y4, rpc)
    return x2, (kn, za[:, ATT_WIDTH + KV_WIDTH:], s_r, jnp.swapaxes(s_g, -1, -2))


def kernel(x_prompt, x_sample, c, cache_k, cache_v, state_rwkv, state_gla, c_ctx, ada_w, ada_b, norm_mix, norm_ffn, w_in, rwkv_mu, rwkv_w0, rwkv_w2, rwkv_a0, rwkv_a2, rwkv_g2, rwkv_kk, rwkv_ka, rwkv_rk, rwkv_ln_g, rwkv_ln_b, attn_qn, attn_kn, gla_a2, gla_ab, gla_norm, p_rwkv, p_attn, p_gla, w_out, router_w, router_b, moe_w1, moe_b1, moe_w2, moe_b2):
    p = {"w_in": w_in, "norm_mix": norm_mix, "norm_ffn": norm_ffn, "rwkv_mu": rwkv_mu, "rwkv_w0": rwkv_w0,
         "rwkv_w2": rwkv_w2, "rwkv_a0": rwkv_a0, "rwkv_a2": rwkv_a2, "rwkv_g2": rwkv_g2, "rwkv_kk": rwkv_kk,
         "rwkv_ka": rwkv_ka, "rwkv_rk": rwkv_rk, "rwkv_ln_g": rwkv_ln_g, "rwkv_ln_b": rwkv_ln_b,
         "attn_qn": attn_qn, "attn_kn": attn_kn, "gla_a2": gla_a2, "gla_ab": gla_ab, "gla_norm": gla_norm,
         "p_rwkv": p_rwkv, "p_attn": p_attn, "p_gla": p_gla, "w_out": w_out, "router_w": router_w,
         "router_b": router_b, "moe_w1": moe_w1, "moe_b1": moe_b1, "moe_w2": moe_w2, "moe_b2": moe_b2}
    Bp, Tp, D = x_prompt.shape
    Bs, Ts, _ = x_sample.shape
    L = ada_w.shape[0]
    P = cache_k.shape[2]

    cond8 = jnp.concatenate([c_ctx[None, :], c, jnp.zeros((8 - 1 - Bs, D), F32)], axis=0)
    mod = _ada_call(cond8, ada_w, ada_b).reshape(L, 8, 6, D)
    weights = [_layer_weights(p, l) for l in range(L)]
    ck = cache_k.reshape(Bs, L, P, KV_WIDTH)
    cv = cache_v.reshape(Bs, L, P, KV_WIDTH)

    yp = x_prompt.reshape(Bp * Tp, D)
    ks, vs, srs, sgs = [], [], [], []
    for l in range(L):
        yp, (kn, vh, s_r, s_g) = _trunk_layer(yp, mod[l, 0:1], weights[l], Bp, Tp, None)
        ks.append(kn.reshape(Bp, Tp, KV_HEADS, HEAD_DIM))
        vs.append(vh.reshape(Bp, Tp, KV_HEADS, HEAD_DIM))
        srs.append(s_r)
        sgs.append(s_g)

    ys = x_sample.reshape(Bs * Ts, D)
    for l in range(L):
        ctx = {"k": ck, "v": cv, "layer": l, "rwkv": state_rwkv[:, l], "gla": state_gla[:, l]}
        ys, _ = _trunk_layer(ys, mod[l, 1:1 + Bs], weights[l], Bs, Ts, ctx)

    return (yp.reshape(Bp, Tp, D), ys.reshape(Bs, Ts, D),
            jnp.stack(ks, axis=1), jnp.stack(vs, axis=1),
            jnp.stack(srs, axis=1), jnp.stack(sgs, axis=1))
```

```python
import functools

import jax
import jax.numpy as jnp
from jax import lax
from jax.experimental import pallas as pl
from jax.experimental.pallas import tpu as pltpu

D_MODEL = 1024
DEPTH = 2
GRID_W = 64
NORM_EPS = 1e-6

RWKV_HEADS = 8
RWKV_HEAD_DIM = 64
RWKV_WIDTH = RWKV_HEADS * RWKV_HEAD_DIM
DECAY_LORA = 64
ICLR_LORA = 64
GATE_LORA = 128
RWKV_GN_EPS = 6.4e-4
RWKV_COLS = 3 * RWKV_WIDTH + 2 * DECAY_LORA + 2 * ICLR_LORA + GATE_LORA

N_HEADS = 8
KV_HEADS = 2
HEAD_DIM = 64
ATT_WIDTH = N_HEADS * HEAD_DIM
KV_WIDTH = KV_HEADS * HEAD_DIM
ROPE_THETA = 10000.0

GLA_HEADS = 4
GLA_DK = 128
GLA_DV = 256
GLA_K_WIDTH = GLA_HEADS * GLA_DK
GLA_V_WIDTH = GLA_HEADS * GLA_DV
GLA_LORA = 16
GLA_GATE_NORMALIZER = 16.0

N_EXPERTS = 32
TOP_K = 4
D_EXPERT = 1024
SWIGLU_ALPHA = 1.702
SWIGLU_LIMIT = 7.0
EXPERT_BLOCK = 256

LANES = 128
CHUNK = 64
ROW_TILE = 256
INPROJ_TILE = 512
Q_TILE = 128
ATT_STACK = 2
VMEM_LIMIT = 52 * 1024 * 1024
LOG2_E = 1.4426950408889634

F32 = jnp.float32
BF16 = jnp.bfloat16
HI = lax.Precision.HIGHEST


def _dot(a, b, prec=None):
    return jnp.dot(a, b, preferred_element_type=F32, precision=prec)


def _dot_nt(a, b, prec=None):
    return lax.dot_general(a, b, (((1,), (1,)), ((), ())), preferred_element_type=F32, precision=prec)


def _dot_tn(a, b, prec=None):
    return lax.dot_general(a, b, (((0,), (0,)), ((), ())), preferred_element_type=F32, precision=prec)


_NN = (((1,), (0,)), ((), ()))
_NT = (((1,), (1,)), ((), ()))
_TN = (((0,), (0,)), ((), ()))


def _dg(a, b, dn):
    return lax.dot_general(a, b, dn, preferred_element_type=F32)


def _split(x):
    h = x.astype(BF16)
    return h, (x - h.astype(F32)).astype(BF16)


def _mm3(a, b, dn=_NN):
    return _dg(a[0], b[0], dn) + _dg(a[0], b[1], dn) + _dg(a[1], b[0], dn)


def _mm3s(a, b, dn=_NN):
    m = a[0].shape[0]
    top = _dg(jnp.concatenate([a[0], a[1]], axis=0), b[0], dn)
    return top[:m] + top[m:] + _dg(a[0], b[1], dn)


def _cumsum_rows(mask, x):
    m = jnp.where(mask, 1.0, 0.0).astype(BF16)
    h = x.astype(BF16)
    r1 = x - h.astype(F32)
    mid = r1.astype(BF16)
    lo = (r1 - mid.astype(F32)).astype(BF16)
    return _dg(m, h, _NN) + _dg(m, mid, _NN) + _dg(m, lo, _NN)


def _seg_sum(x, seg_bf16):
    xh = x.astype(BF16)
    xl = (x - xh.astype(F32)).astype(BF16)
    return _dot(xh, seg_bf16) + _dot(xl, seg_bf16)


def _sigmoid(x):
    return 1.0 / (1.0 + jnp.exp(-x))


def _params(sem):
    return pltpu.CompilerParams(dimension_semantics=sem, vmem_limit_bytes=VMEM_LIMIT)


def _const_spec(shape):
    nd = len(shape)
    return pl.BlockSpec(shape, lambda *_: (0,) * nd)


def _ada_kernel(c_ref, w_ref, b_ref, o_ref):
    c = c_ref[...]
    s = c * _sigmoid(c)
    o_ref[...] = _mm3(_split(s), _split(w_ref[...])) + b_ref[...]


def _ada_call(cond8, ada_w, ada_b):
    L, D, W = ada_w.shape
    tn = 1536
    return pl.pallas_call(
        _ada_kernel,
        grid=(L, W // tn),
        in_specs=[pl.BlockSpec((8, D), lambda l, j: (0, 0)),
                  pl.BlockSpec((None, D, tn), lambda l, j: (l, 0, j)),
                  pl.BlockSpec((None, 1, tn), lambda l, j: (l, 0, j))],
        out_specs=pl.BlockSpec((None, 8, tn), lambda l, j: (l, 0, j)),
        out_shape=jax.ShapeDtypeStruct((L, 8, W), F32),
        compiler_params=_params(("parallel", "parallel")),
        name="ada",
    )(cond8, ada_w, ada_b.reshape(L, 1, W))


def _inproj_kernel(x_ref, m_ref, g_ref, w_ref, o_ref, h_scr):
    @pl.when(pl.program_id(1) == 0)
    def _():
        x = x_ref[...]
        y = x * lax.rsqrt(jnp.mean(x * x, axis=-1, keepdims=True) + NORM_EPS) * g_ref[...]
        h = y * (1.0 + m_ref[1:2, :]) + m_ref[0:1, :]
        h_scr[...] = h.astype(BF16)

    o_ref[...] = _dot(h_scr[...], w_ref[...])


def _inproj_call(x, mod, gain, w, rows_per_cond, tn):
    N, D = x.shape
    W = w.shape[1]
    tm = INPROJ_TILE
    return pl.pallas_call(
        _inproj_kernel,
        grid=(N // tm, W // tn),
        in_specs=[pl.BlockSpec((tm, D), lambda i, j: (i, 0)),
                  pl.BlockSpec((None, 6, D), lambda i, j: ((i * tm) // rows_per_cond, 0, 0)),
                  pl.BlockSpec((1, D), lambda i, j: (0, 0)),
                  pl.BlockSpec((D, tn), lambda i, j: (0, j))],
        out_specs=pl.BlockSpec((tm, tn), lambda i, j: (i, j)),
        out_shape=jax.ShapeDtypeStruct((N, W), F32),
        scratch_shapes=[pltpu.VMEM((tm, D), BF16)],
        compiler_params=_params(("parallel", "arbitrary")),
        name="inproj",
    )(x, mod, gain, w)


def _rwkv_prep_kernel(z_ref, zp_ref, zn_ref, mu_ref, kkp_ref, ka_ref, rk_ref, w0_ref, a0_ref,
                      w2_ref, a2_ref, g2_ref, seg_ref,
                      r_o, v_o, kk_o, g_o, bon_o, lw0_o, lw1_o, k0_o, k1_o, b0_o, b1_o,
                      *, tm, blocks_per_seq):
    i = pl.program_id(0)
    z = z_ref[...]
    row = lax.broadcasted_iota(jnp.int32, (tm, 1), 0)
    first = (i % blocks_per_seq) == 0
    last = (i % blocks_per_seq) == blocks_per_seq - 1
    pz = jnp.where(first, 0.0, zp_ref[7:8, :])
    nz = jnp.where(last, 0.0, zn_ref[0:1, :])
    prev = jnp.where(row == 0, pz, pltpu.roll(z, 1, 0))
    nxt = jnp.where(row == tm - 1, nz, pltpu.roll(z, tm - 1, 0))
    zm = z + mu_ref[...] * (0.5 * (prev + nxt) - z)

    Wd = RWKV_WIDTH
    r = zm[:, 0:Wd]
    k = zm[:, Wd:2 * Wd]
    v = zm[:, 2 * Wd:3 * Wd]
    lw = zm[:, 3 * Wd:3 * Wd + 128]
    la = zm[:, 3 * Wd + 128:3 * Wd + 256]
    lg = zm[:, 3 * Wd + 256:3 * Wd + 384]
    seg = seg_ref[...]

    kk = k * kkp_ref[...]
    kk = kk / jnp.maximum(jnp.sqrt(_seg_sum(kk * kk, seg)), 1e-12)
    w_raw = _mm3(_split(jnp.tanh(lw)), _split(w2_ref[...])) + w0_ref[...]
    logw = -_sigmoid(w_raw) * 0.6065306597126334
    a = _sigmoid(_mm3(_split(la), _split(a2_ref[...])) + a0_ref[...])
    ka = ka_ref[...]
    rk = rk_ref[...]
    bonus = jnp.zeros_like(r)
    for d, (lw_o, k_o, b_o) in enumerate(((lw0_o, k0_o, b0_o), (lw1_o, k1_o, b1_o))):
        a_d = a[:, d * Wd:(d + 1) * Wd]
        k_d = k * (1.0 + (a_d - 1.0) * ka)
        lw_o[...] = logw[:, d * Wd:(d + 1) * Wd]
        k_o[...] = k_d
        b_o[...] = kk * a_d
        bonus = bonus + _seg_sum(r * k_d * rk, seg) * v
    r_o[...] = r
    v_o[...] = v
    kk_o[...] = kk
    bon_o[...] = bonus
    g_o[...] = _mm3(_split(_sigmoid(lg)), _split(g2_ref[...]))


def _rwkv_prep_call(zr, seq_len, lw):
    N, Wz = zr.shape
    tm = ROW_TILE
    bps = seq_len // tm
    Wd = RWKV_WIDTH
    nb8 = N // 8
    row_spec = pl.BlockSpec((tm, Wd), lambda i: (i, 0))
    out = jax.ShapeDtypeStruct((N, Wd), F32)
    return pl.pallas_call(
        functools.partial(_rwkv_prep_kernel, tm=tm, blocks_per_seq=bps),
        grid=(N // tm,),
        in_specs=[pl.BlockSpec((tm, Wz), lambda i: (i, 0)),
                  pl.BlockSpec((8, Wz), lambda i: (jnp.maximum(i * (tm // 8) - 1, 0), 0)),
                  pl.BlockSpec((8, Wz), lambda i: (jnp.minimum((i + 1) * (tm // 8), nb8 - 1), 0)),
                  _const_spec((1, Wz)), _const_spec((1, Wd)), _const_spec((1, Wd)), _const_spec((1, Wd)),
                  _const_spec((1, 2 * Wd)), _const_spec((1, 2 * Wd)),
                  _const_spec((128, 2 * Wd)), _const_spec((128, 2 * Wd)), _const_spec((128, Wd)),
                  _const_spec((Wd, Wd))],
        out_specs=[row_spec] * 11,
        out_shape=[out] * 11,
        compiler_params=_params(("parallel",)),
        name="rwkv_prep",
    )(zr, zr, zr, lw["mu"], lw["kk"], lw["ka"], lw["rk"], lw["w0"], lw["a0"],
      lw["w2"], lw["a2"], lw["g2"], lw["seg64"])


def _tri_masks(d):
    ti = lax.broadcasted_iota(jnp.int32, (CHUNK, CHUNK), 0)
    si = lax.broadcasted_iota(jnp.int32, (CHUNK, CHUNK), 1)
    if d == 0:
        return si <= ti, si < ti
    return si >= ti, si > ti


def _rwkv_scan_kernel(rf, vf, kkf, lwf, kf, bf, rb, vb, kkb, lwb, kb, bb, s0_ref,
                      yf_o, yb_o, st_o, s_scr, *, nc):
    i = pl.program_id(1)

    @pl.when(i == 0)
    def _():
        s_scr[...] = s0_ref[...]

    C = CHUNK
    eye = (lax.broadcasted_iota(jnp.int32, (C, C), 0)
           == lax.broadcasted_iota(jnp.int32, (C, C), 1)).astype(F32)
    H, Dh = RWKV_HEADS, RWKV_HEAD_DIM
    sls = [slice(h * Dh, (h + 1) * Dh) for h in range(H)]
    dirs = ((rf, vf, kkf, lwf, kf, bf), (rb, vb, kkb, lwb, kb, bb))
    pre = []
    for d, (r_ref, v_ref, kk_ref, lw_ref, k_ref, b_ref) in enumerate(dirs):
        incl, strict = _tri_masks(d)
        lw = lw_ref[...]
        b = _cumsum_rows(incl, lw)
        btot = jnp.sum(lw, axis=0, keepdims=True)
        r, v, kk, k, beta = r_ref[...], v_ref[...], kk_ref[...], k_ref[...], b_ref[...]
        nb = jnp.exp(-b)
        eb = jnp.exp(btot - b)
        gtot = jnp.exp(btot)
        ar = _split(jnp.concatenate([-kk * jnp.exp(b - lw), r * jnp.exp(b)], axis=0))
        bbar = _split(beta * nb)
        kbar = (k * nb).astype(BF16)
        hat = jnp.concatenate([beta * eb, k * eb], axis=0).astype(BF16)
        v16 = v.astype(BF16)

        g1 = [_mm3s((ar[0][:, s], ar[1][:, s]), (bbar[0][:, s], bbar[1][:, s]), _NT) for s in sls]
        g2 = [_dg(ar[0][:, s], kbar[:, s], _NT) for s in sls]
        a_ab = [jnp.where(strict, g[:C], 0.0) for g in g1]
        a_rb = [jnp.where(incl, g[C:], 0.0).astype(BF16) for g in g1]
        a_ak = [jnp.where(strict, g[:C], 0.0).astype(BF16) for g in g2]
        a_rk = [jnp.where(incl, g[C:], 0.0).astype(BF16) for g in g2]
        av = [_dg(a_ak[h], v16[:, sls[h]], _NN) for h in range(H)]
        t = [eye + a for a in a_ab]
        p = [_split(a) for a in a_ab]
        p = [_split(_mm3s(pp, pp)) for pp in p]
        for _ in range(4):
            ts = [_split(tt) for tt in t]
            res = [_mm3s((jnp.concatenate([pp[0], tt[0]], axis=0), jnp.concatenate([pp[1], tt[1]], axis=0)), pp)
                   for pp, tt in zip(p, ts)]
            p = [_split(rr[:C]) for rr in res]
            t = [tt + rr[C:] for tt, rr in zip(t, res)]
        t = [tt + _mm3s(_split(tt), pp) for tt, pp in zip(t, p)]
        pre.append((ar[0], av, [_split(tt) for tt in t], a_rb, a_rk, v16, hat, gtot))

    hd = [(d, h) for d in range(2) for h in range(H)]
    s_old = [s_scr[d, h] for d, h in hd]
    z = [_dg(pre[d][0][:, sls[h]], s.astype(BF16), _NT) for (d, h), s in zip(hd, s_old)]
    u = [_mm3s(pre[d][2][h], _split(zz[:C] + pre[d][1][h])).astype(BF16) for (d, h), zz in zip(hd, z)]
    ys = [zz[C:] + _dg(pre[d][3][h], uu, _NN) + _dg(pre[d][4][h], pre[d][5][:, sls[h]], _NN)
          for (d, h), zz, uu in zip(hd, z, u)]
    s_new = [s * pre[d][7][:, sls[h]]
             + _dg(jnp.concatenate([uu, pre[d][5][:, sls[h]]], axis=0), pre[d][6][:, sls[h]], _TN)
             for (d, h), s, uu in zip(hd, s_old, u)]
    for (d, h), yy, ss in zip(hd, ys, s_new):
        (yf_o, yb_o)[d][:, sls[h]] = yy
        s_scr[d, h] = ss

    @pl.when(i == nc - 1)
    def _():
        st_o[...] = s_scr[...]


def _rwkv_scan_call(prep, s0, B, T):
    r, v, kk, lw0, lw1, k0, k1, b0, b1 = prep
    N, Wd = r.shape
    nc = T // CHUNK
    fwd = pl.BlockSpec((CHUNK, Wd), lambda b, i: (b * nc + i, 0))
    bwd = pl.BlockSpec((CHUNK, Wd), lambda b, i: (b * nc + nc - 1 - i, 0))
    st = pl.BlockSpec((None, 2, RWKV_HEADS, RWKV_HEAD_DIM, RWKV_HEAD_DIM), lambda b, i: (b, 0, 0, 0, 0))
    y = jax.ShapeDtypeStruct((N, Wd), F32)
    return pl.pallas_call(
        functools.partial(_rwkv_scan_kernel, nc=nc),
        grid=(B, nc),
        in_specs=[fwd] * 6 + [bwd] * 6 + [st],
        out_specs=[fwd, bwd, st],
        out_shape=[y, y, jax.ShapeDtypeStruct(s0.shape, F32)],
        scratch_shapes=[pltpu.VMEM((2, RWKV_HEADS, RWKV_HEAD_DIM, RWKV_HEAD_DIM), F32)],
        compiler_params=_params(("parallel", "arbitrary")),
        name="rwkv_scan",
    )(r, v, kk, lw0, k0, b0, r, v, kk, lw1, k1, b1, s0)


def _gla_scan_kernel(xf, glf, xb, glb, a2_ref, ab_ref, s0_ref, of_o, ob_o, st_o, s_scr, *, nc):
    i = pl.program_id(1)

    @pl.when(i == 0)
    def _():
        s_scr[...] = s0_ref[...]

    Kw = GLA_K_WIDTH
    work = []
    for d, (x_ref, gl_ref) in enumerate(((xf, glf), (xb, glb))):
        incl, _ = _tri_masks(d)
        xa = (_mm3(_split(gl_ref[...]), _split(a2_ref[:, d * Kw:(d + 1) * Kw]))
              + ab_ref[:, d * Kw:(d + 1) * Kw])
        log_a = (jnp.minimum(xa, 0.0) - jnp.log1p(jnp.exp(-jnp.abs(xa)))) * (1.0 / GLA_GATE_NORMALIZER)
        b = _cumsum_rows(incl, log_a)
        bl = jnp.sum(log_a, axis=0, keepdims=True)
        q = x_ref[:, 0:Kw] * (GLA_DK ** -0.5)
        k = x_ref[:, Kw:2 * Kw]
        q_in = (q * jnp.exp(b)).astype(BF16)
        k_in = (k * jnp.exp(-b)).astype(BF16)
        k_end = (k * jnp.exp(bl - b)).astype(BF16)
        gtot = jnp.exp(bl)
        for h in range(GLA_HEADS):
            ks = slice(h * GLA_DK, (h + 1) * GLA_DK)
            v_h = x_ref[:, 2 * Kw + h * GLA_DV:2 * Kw + (h + 1) * GLA_DV].astype(BF16)
            att = jnp.where(incl, _dg(q_in[:, ks], k_in[:, ks], _NT), 0.0).astype(BF16)
            work.append((d, h, _dg(att, v_h, _NN), q_in[:, ks], gtot[:, ks], _dg(v_h, k_end[:, ks], _TN)))

    s_old = [s_scr[d, h] for d, h, *_ in work]
    outs = [o_in + _dg(qh, s.astype(BF16), _NT) for (_, _, o_in, qh, _, _), s in zip(work, s_old)]
    for (d, h, _, _, g, kv), s, o in zip(work, s_old, outs):
        (of_o, ob_o)[d][:, h * GLA_DV:(h + 1) * GLA_DV] = o
        s_scr[d, h] = s * g + kv

    @pl.when(i == nc - 1)
    def _():
        st_o[...] = s_scr[...]


def _gla_scan_call(zg, s0t, a2, ab, B, T):
    N = zg.shape[0]
    nc = T // CHUNK
    Wx = 2 * GLA_K_WIDTH + GLA_V_WIDTH
    gl_col = (Wx + GLA_V_WIDTH) // LANES
    xf = pl.BlockSpec((CHUNK, Wx), lambda b, i: (b * nc + i, 0))
    xb = pl.BlockSpec((CHUNK, Wx), lambda b, i: (b * nc + nc - 1 - i, 0))
    gf = pl.BlockSpec((CHUNK, LANES), lambda b, i: (b * nc + i, gl_col))
    gb = pl.BlockSpec((CHUNK, LANES), lambda b, i: (b * nc + nc - 1 - i, gl_col))
    of = pl.BlockSpec((CHUNK, GLA_V_WIDTH), lambda b, i: (b * nc + i, 0))
    ob = pl.BlockSpec((CHUNK, GLA_V_WIDTH), lambda b, i: (b * nc + nc - 1 - i, 0))
    st = pl.BlockSpec((None, 2, GLA_HEADS, GLA_DV, GLA_DK), lambda b, i: (b, 0, 0, 0, 0))
    o = jax.ShapeDtypeStruct((N, GLA_V_WIDTH), F32)
    return pl.pallas_call(
        functools.partial(_gla_scan_kernel, nc=nc),
        grid=(B, nc),
        in_specs=[xf, gf, xb, gb, _const_spec((LANES, 2 * GLA_K_WIDTH)), _const_spec((1, 2 * GLA_K_WIDTH)), st],
        out_specs=[of, ob, st],
        out_shape=[o, o, jax.ShapeDtypeStruct(s0t.shape, F32)],
        scratch_shapes=[pltpu.VMEM((2, GLA_HEADS, GLA_DV, GLA_DK), F32)],
        compiler_params=_params(("parallel", "arbitrary")),
        name="gla_scan",
    )(zg, zg, zg, zg, a2, ab, s0t)


def _rope(x, cos, sin):
    n = x.shape[1]
    lane = lax.broadcasted_iota(jnp.int32, x.shape, 1)
    up = pltpu.roll(x, n - 16, 1)
    dn = pltpu.roll(x, 16, 1)
    sw = jnp.where((lane % 32) < 16, up, dn)
    reps = n // LANES
    if reps > 1:
        cos = jnp.concatenate([cos] * reps, axis=1)
        sin = jnp.concatenate([sin] * reps, axis=1)
    return x * cos + sw * sin


def _attn_prep_kernel(*refs, rope):
    if rope:
        z_ref, qn_ref, kn_ref, segq_ref, segk_ref, cos_ref, sin_ref, q_o, kn_o, kr_o, ve_o = refs
    else:
        z_ref, qn_ref, kn_ref, segq_ref, segk_ref, q_o, kn_o, kr_o, ve_o = refs
    zq = z_ref[:, 0:ATT_WIDTH]
    zk = z_ref[:, ATT_WIDTH:ATT_WIDTH + KV_WIDTH]
    qh = zq * lax.rsqrt(_seg_sum(zq * zq, segq_ref[...]) * (1.0 / HEAD_DIM) + NORM_EPS) * qn_ref[...]
    kh = zk * lax.rsqrt(_seg_sum(zk * zk, segk_ref[...]) * (1.0 / HEAD_DIM) + NORM_EPS) * kn_ref[...]
    kn_o[...] = kh
    if rope:
        qh = _rope(qh, cos_ref[...], sin_ref[...])
        kh = _rope(kh, cos_ref[...], sin_ref[...])
    q_o[...] = (qh * (LOG2_E * HEAD_DIM ** -0.5)).astype(BF16)
    kr_o[...] = kh.astype(BF16)
    v = z_ref[:, ATT_WIDTH + KV_WIDTH:ATT_WIDTH + 2 * KV_WIDTH]
    group = lax.broadcasted_iota(jnp.int32, v.shape, 1) // HEAD_DIM
    for g in range(KV_HEADS):
        ve_o[:, g * LANES:(g + 1) * LANES] = jnp.where(group == g, v, 1.0).astype(BF16)


def _attn_prep_call(za, lw, seq_len, rope_tabs):
    N, Wz = za.shape
    tm = ROW_TILE
    bps = seq_len // tm
    rope = rope_tabs is not None
    in_specs = [pl.BlockSpec((tm, Wz), lambda i: (i, 0)),
                _const_spec((1, ATT_WIDTH)), _const_spec((1, KV_WIDTH)),
                _const_spec((ATT_WIDTH, ATT_WIDTH)), _const_spec((KV_WIDTH, KV_WIDTH))]
    args = [za, lw["qn"], lw["kn"], lw["seg64"], lw["seg64"][:KV_WIDTH, :KV_WIDTH]]
    if rope:
        in_specs += [pl.BlockSpec((tm, LANES), lambda i: (i % bps, 0))] * 2
        args += list(rope_tabs)
    return pl.pallas_call(
        functools.partial(_attn_prep_kernel, rope=rope),
        grid=(N // tm,),
        in_specs=in_specs,
        out_specs=[pl.BlockSpec((tm, ATT_WIDTH), lambda i: (i, 0)),
                   pl.BlockSpec((tm, KV_WIDTH), lambda i: (i, 0)),
                   pl.BlockSpec((tm, KV_WIDTH), lambda i: (i, 0)),
                   pl.BlockSpec((tm, KV_HEADS * LANES), lambda i: (i, 0))],
        out_shape=[jax.ShapeDtypeStruct((N, ATT_WIDTH), BF16),
                   jax.ShapeDtypeStruct((N, KV_WIDTH), F32),
                   jax.ShapeDtypeStruct((N, KV_WIDTH), BF16),
                   jax.ShapeDtypeStruct((N, KV_HEADS * LANES), BF16)],
        compiler_params=_params(("parallel",)),
        name="attn_prep",
    )(*args)


def _attn_kernel(*refs, cached):
    if cached:
        q_ref, k_ref, v_ref, ck_ref, cv_ref, o_ref = refs
    else:
        q_ref, k_ref, v_ref, o_ref = refs
    G = N_HEADS // KV_HEADS
    tq = q_ref.shape[0]
    for g in range(KV_HEADS):
        gs = slice(g * HEAD_DIM, (g + 1) * HEAD_DIM)
        other = slice((1 - g) * HEAD_DIM, (2 - g) * HEAD_DIM)
        kg = k_ref[:, gs]
        vg = v_ref[:, g * LANES:(g + 1) * LANES]
        if cached:
            cv = cv_ref[...]
            own = (lax.broadcasted_iota(jnp.int32, cv.shape, 1) // HEAD_DIM) == g
            ckg = ck_ref[:, gs].astype(BF16)
            cvg = jnp.where(own, cv, 1.0).astype(BF16)
        for h0 in range(g * G, (g + 1) * G, ATT_STACK):
            heads = range(h0, h0 + ATT_STACK)
            qs = jnp.concatenate([q_ref[:, h * HEAD_DIM:(h + 1) * HEAD_DIM] for h in heads], axis=0)
            s1 = _dg(qs, kg, _NT)
            m = jnp.max(s1, axis=-1, keepdims=True)
            if cached:
                s2 = _dg(qs, ckg, _NT)
                m = jnp.maximum(m, jnp.max(s2, axis=-1, keepdims=True))
            oe = _dg(jnp.exp2(s1 - m).astype(BF16), vg, _NN)
            if cached:
                oe = oe + _dg(jnp.exp2(s2 - m).astype(BF16), cvg, _NN)
            o = oe[:, gs] / oe[:, other]
            for j, h in enumerate(heads):
                o_ref[:, h * HEAD_DIM:(h + 1) * HEAD_DIM] = o[j * tq:(j + 1) * tq]


def _attn_call(q, k, vext, cache, B, T):
    N = q.shape[0]
    tq = Q_TILE
    nq = T // tq
    in_specs = [pl.BlockSpec((tq, ATT_WIDTH), lambda b, i: (b * nq + i, 0)),
                pl.BlockSpec((T, KV_WIDTH), lambda b, i: (b, 0)),
                pl.BlockSpec((T, KV_HEADS * LANES), lambda b, i: (b, 0))]
    args = [q, k, vext]
    if cache is not None:
        ck, cv, layer = cache
        P = ck.shape[2]
        cspec = pl.BlockSpec((None, None, P, KV_WIDTH), lambda b, i: (b, layer, 0, 0))
        in_specs += [cspec, cspec]
        args += [ck, cv]
    return pl.pallas_call(
        functools.partial(_attn_kernel, cached=cache is not None),
        grid=(B, nq),
        in_specs=in_specs,
        out_specs=pl.BlockSpec((tq, ATT_WIDTH), lambda b, i: (b * nq + i, 0)),
        out_shape=jax.ShapeDtypeStruct((N, ATT_WIDTH), F32),
        compiler_params=_params(("parallel", "arbitrary")),
        name="attention",
    )(*args)


def _merge_kernel(yf, yb, bon, g, oa, gf, gb, gr, mr, ma, mg, x_ref, m_ref,
                  lng, lnb, gn, nf, seg_ref, pr, pa, pg, wo, rw, rb,
                  x1_o, h2_o, ti_o, tg_o):
    seg = seg_ref[...]
    inv = 1.0 / RWKV_HEAD_DIM
    y = yf[...] + yb[...]
    mu = _seg_sum(y, seg) * inv
    yc = y - mu
    var = _seg_sum(yc * yc, seg) * inv
    o_r = (yc * lax.rsqrt(var + RWKV_GN_EPS) * lng[...] + lnb[...] + bon[...]) * g[...]

    gate_r = gr[...]
    silu_r = gate_r * _sigmoid(gate_r)
    gnv = gn[...]
    cols = []
    for h in range(GLA_HEADS):
        vs = slice(h * GLA_DV, (h + 1) * GLA_DV)
        o = gf[:, vs] + gb[:, vs]
        ms = jnp.mean(o * o, axis=-1, keepdims=True)
        cols.append(o * lax.rsqrt(ms + NORM_EPS) * gnv * silu_r[:, vs])
    o_g = jnp.concatenate(cols, axis=1)

    merged = (_sigmoid(mr[...]) * _dot(o_r.astype(BF16), pr[...])
              + _sigmoid(ma[...]) * _dot(oa[...].astype(BF16), pa[...])
              + _sigmoid(mg[...]) * _dot(o_g.astype(BF16), pg[...]))
    x1 = x_ref[...] + m_ref[2:3, :] * _dot(merged.astype(BF16), wo[...])
    x1_o[...] = x1
    hn = x1 * lax.rsqrt(jnp.mean(x1 * x1, axis=-1, keepdims=True) + NORM_EPS) * nf[...]
    h2 = hn * (1.0 + m_ref[4:5, :]) + m_ref[3:4, :]
    h2_o[...] = h2

    logits = _mm3(_split(h2), _split(rw[...])) + rb[...]
    lane = lax.broadcasted_iota(jnp.int32, logits.shape, 1)
    vals, idxs = [], []
    for _ in range(TOP_K):
        m = jnp.max(logits, axis=-1, keepdims=True)
        idx = jnp.min(jnp.where(logits == m, lane, LANES), axis=-1, keepdims=True)
        vals.append(m)
        idxs.append(idx)
        logits = jnp.where(lane == idx, -jnp.inf, logits)
    es = [jnp.exp(vv - vals[0]) for vv in vals]
    den = es[0] + es[1] + es[2] + es[3]
    ti = jnp.zeros(logits.shape, jnp.int32)
    tg = jnp.zeros(logits.shape, F32)
    for j in range(TOP_K):
        ti = jnp.where(lane == j, idxs[j], ti)
        tg = jnp.where(lane == j, es[j] / den, tg)
    ti_o[...] = ti
    tg_o[...] = tg


def _merge_call(yf, yb, bon, g, oa, gf, gb, zg, zm, x, mod, lw, rows_per_cond):
    N, D = x.shape
    tm = ROW_TILE
    Wd = RWKV_WIDTH

    def rows(w, col=0):
        return pl.BlockSpec((tm, w), lambda i: (i, col))

    in_specs = [rows(Wd)] * 4 + [rows(ATT_WIDTH), rows(GLA_V_WIDTH), rows(GLA_V_WIDTH),
                                 rows(GLA_V_WIDTH, 2), rows(D, 0), rows(D, 1), rows(D, 2), rows(D),
                                 pl.BlockSpec((None, 6, D), lambda i: ((i * tm) // rows_per_cond, 0, 0)),
                                 _const_spec((1, Wd)), _const_spec((1, Wd)), _const_spec((1, GLA_DV)),
                                 _const_spec((1, D)), _const_spec((Wd, Wd)),
                                 _const_spec((Wd, D)), _const_spec((ATT_WIDTH, D)), _const_spec((GLA_V_WIDTH, D)),
                                 _const_spec((D, D)), _const_spec((D, LANES)), _const_spec((1, LANES))]
    return pl.pallas_call(
        _merge_kernel,
        grid=(N // tm,),
        in_specs=in_specs,
        out_specs=[rows(D), rows(D), rows(LANES), rows(LANES)],
        out_shape=[jax.ShapeDtypeStruct((N, D), F32), jax.ShapeDtypeStruct((N, D), F32),
                   jax.ShapeDtypeStruct((N, LANES), jnp.int32), jax.ShapeDtypeStruct((N, LANES), F32)],
        compiler_params=_params(("parallel",)),
        name="merge",
    )(yf, yb, bon, g, oa, gf, gb, zg, zm, zm, zm, x, mod,
      lw["ln_g"], lw["ln_b"], lw["gla_norm"], lw["norm_ffn"], lw["seg64"],
      lw["p_rwkv"], lw["p_attn"], lw["p_gla"], lw["w_out"], lw["router_w"], lw["router_b"])


def _moe_kernel(be_ref, na_ref, x_ref, gate_ref, w1_ref, b1_ref, w2_ref, b2_ref, o_ref, w1_scr, w2_scr):
    i = pl.program_id(0)

    @pl.when(jnp.logical_or(i == 0, be_ref[i] != be_ref[jnp.maximum(i - 1, 0)]))
    def _():
        w1_scr[...] = w1_ref[...].astype(BF16)
        w2_scr[...] = w2_ref[...].astype(BF16)

    @pl.when(i < na_ref[0])
    def _():
        z = _dot(x_ref[...].astype(BF16), w1_scr[...]) + b1_ref[...]
        glu = jnp.minimum(z[:, :D_EXPERT], SWIGLU_LIMIT)
        lin = jnp.clip(z[:, D_EXPERT:], -SWIGLU_LIMIT, SWIGLU_LIMIT)
        act = glu * _sigmoid(SWIGLU_ALPHA * glu) * (lin + 1.0)
        o_ref[...] = (_dot(act.astype(BF16), w2_scr[...]) + b2_ref[...]) * gate_ref[...]

    @pl.when(i >= na_ref[0])
    def _():
        o_ref[...] = jnp.zeros_like(o_ref)


def _moe_call(xg, row_gate, block_exp, n_active, layer, w1, b1, w2, b2):
    R, D = xg.shape
    tb = EXPERT_BLOCK
    F2 = w1.shape[3]
    grid_spec = pltpu.PrefetchScalarGridSpec(
        num_scalar_prefetch=2,
        grid=(R // tb,),
        in_specs=[pl.BlockSpec((tb, D), lambda i, be, na: (i, 0)),
                  pl.BlockSpec((tb, 1), lambda i, be, na: (i, 0)),
                  pl.BlockSpec((None, None, D, F2), lambda i, be, na: (layer, be[i], 0, 0)),
                  pl.BlockSpec((None, 1, F2), lambda i, be, na: (be[i], 0, 0)),
                  pl.BlockSpec((None, None, F2 // 2, D), lambda i, be, na: (layer, be[i], 0, 0)),
                  pl.BlockSpec((None, 1, D), lambda i, be, na: (be[i], 0, 0))],
        out_specs=pl.BlockSpec((tb, D), lambda i, be, na: (i, 0)),
        scratch_shapes=[pltpu.VMEM((D, F2), BF16), pltpu.VMEM((F2 // 2, D), BF16)],
    )
    return pl.pallas_call(
        _moe_kernel,
        grid_spec=grid_spec,
        out_shape=jax.ShapeDtypeStruct((R, D), F32),
        compiler_params=_params(("arbitrary",)),
        name="moe_experts",
    )(block_exp, n_active, xg, row_gate, w1, b1, w2, b2)


def _combine_kernel(x1_ref, m_ref, y_ref, o_ref):
    D = D_MODEL
    y = (y_ref[:, 0:D] + y_ref[:, D:2 * D]) + (y_ref[:, 2 * D:3 * D] + y_ref[:, 3 * D:4 * D])
    o_ref[...] = x1_ref[...] + m_ref[5:6, :] * y


def _combine_call(x1, mod, y4, rows_per_cond):
    N, D = x1.shape
    tm = ROW_TILE
    return pl.pallas_call(
        _combine_kernel,
        grid=(N // tm,),
        in_specs=[pl.BlockSpec((tm, D), lambda i: (i, 0)),
                  pl.BlockSpec((None, 6, D), lambda i: ((i * tm) // rows_per_cond, 0, 0)),
                  pl.BlockSpec((tm, TOP_K * D), lambda i: (i, 0))],
        out_specs=pl.BlockSpec((tm, D), lambda i: (i, 0)),
        out_shape=jax.ShapeDtypeStruct((N, D), F32),
        compiler_params=_params(("parallel",)),
        name="combine",
    )(x1, mod, y4)


def _moe_ffn(h2, top_i, top_g, lw):
    N, D = h2.shape
    n_as = N * TOP_K
    n_blocks = -(-n_as // EXPERT_BLOCK) + N_EXPERTS
    R = n_blocks * EXPERT_BLOCK
    ex = jnp.arange(N_EXPERTS, dtype=jnp.int32)

    def lut(onehot, table):
        return jnp.sum(jnp.where(onehot, table[None, :], 0), axis=1)

    flat_e = top_i.reshape(n_as)
    flat_g = top_g.reshape(n_as)
    oh_a = flat_e[:, None] == ex[None, :]
    counts = jnp.sum(oh_a, axis=0, dtype=jnp.int32)
    starts = jnp.cumsum(counts) - counts
    padded = (counts + EXPERT_BLOCK - 1) // EXPERT_BLOCK * EXPERT_BLOCK
    pends = jnp.cumsum(padded)
    pstarts = pends - padded
    order = jnp.argsort(flat_e).astype(jnp.int32)
    rank = jnp.argsort(order).astype(jnp.int32)
    pos = rank + lut(oh_a, pstarts - starts)
    blk_start = jnp.arange(n_blocks, dtype=jnp.int32) * EXPERT_BLOCK
    block_exp = jnp.minimum(jnp.sum(pends[None, :] <= blk_start[:, None], axis=1),
                            N_EXPERTS - 1).astype(jnp.int32)
    n_active = (pends[-1] // EXPERT_BLOCK).astype(jnp.int32).reshape(1)
    oh_p = jnp.repeat(block_exp, EXPERT_BLOCK)[:, None] == ex[None, :]
    idx = jnp.arange(R, dtype=jnp.int32) - lut(oh_p, pstarts)
    valid = (idx >= 0) & (idx < lut(oh_p, counts))
    asg = order[jnp.clip(lut(oh_p, starts) + idx, 0, n_as - 1)]
    row_tok = jnp.where(valid, asg // TOP_K, 0)
    row_gate = jnp.where(valid, flat_g[asg], 0.0)
    out = _moe_call(h2[row_tok], row_gate.reshape(R, 1), block_exp, n_active,
                    lw["layer"], lw["moe_w1"], lw["moe_b1"], lw["moe_w2"], lw["moe_b2"])
    return out[pos].reshape(N, TOP_K * D)


def _seg_matrix(n, seg):
    idx = jnp.arange(n) // seg
    return (idx[:, None] == idx[None, :]).astype(BF16)


def _rope_tables(T):
    quarter = HEAD_DIM // 4
    inv_freq = ROPE_THETA ** (-jnp.arange(quarter, dtype=F32) / quarter)
    t = jnp.arange(T)
    row = (t // GRID_W).astype(F32)
    col = (t % GRID_W).astype(F32)
    ang_r = row[:, None] * inv_freq
    ang_c = col[:, None] * inv_freq
    cos = jnp.concatenate([jnp.cos(ang_r)] * 2 + [jnp.cos(ang_c)] * 2, axis=1)
    sin = jnp.concatenate([-jnp.sin(ang_r), jnp.sin(ang_r), -jnp.sin(ang_c), jnp.sin(ang_c)], axis=1)
    return jnp.concatenate([cos, cos], axis=1), jnp.concatenate([sin, sin], axis=1)


def _layer_weights(p, l):
    Wd = RWKV_WIDTH
    w_in = p["w_in"][l]
    c0 = RWKV_COLS
    aq, ak, av, gq, gk, gv, gl, gr = (c0, c0 + 512, c0 + 640, c0 + 768, c0 + 1280, c0 + 1792, c0 + 2816, c0 + 2848)
    m0 = gr + GLA_V_WIDTH
    D = D_MODEL
    w_g = jnp.concatenate([w_in[:, gq:gl], w_in[:, gr:m0], w_in[:, gl:gr],
                           jnp.zeros((D, LANES - 2 * GLA_LORA), F32)], axis=1)
    z64 = jnp.zeros((DECAY_LORA, Wd), F32)
    w2 = jnp.concatenate([jnp.concatenate([p["rwkv_w2"][l, 0], z64], axis=1),
                          jnp.concatenate([z64, p["rwkv_w2"][l, 1]], axis=1)], axis=0)
    a2 = jnp.concatenate([jnp.concatenate([p["rwkv_a2"][l, 0], z64], axis=1),
                          jnp.concatenate([z64, p["rwkv_a2"][l, 1]], axis=1)], axis=0)
    z16 = jnp.zeros((GLA_LORA, GLA_K_WIDTH), F32)
    ga2 = jnp.concatenate([jnp.concatenate([p["gla_a2"][l, 0], z16], axis=1),
                           jnp.concatenate([z16, p["gla_a2"][l, 1]], axis=1),
                           jnp.zeros((LANES - 2 * GLA_LORA, 2 * GLA_K_WIDTH), F32)], axis=0)
    return {
        "w_r": w_in[:, :c0].astype(BF16), "w_a": w_in[:, aq:gq].astype(BF16),
        "w_g": w_g.astype(BF16), "w_m": w_in[:, m0:].astype(BF16),
        "norm_mix": p["norm_mix"][l].reshape(1, D), "norm_ffn": p["norm_ffn"][l].reshape(1, D),
        "mu": p["rwkv_mu"][l].reshape(1, c0),
        "kk": p["rwkv_kk"][l].reshape(1, Wd), "ka": p["rwkv_ka"][l].reshape(1, Wd),
        "rk": p["rwkv_rk"][l].reshape(1, Wd),
        "w0": p["rwkv_w0"][l].reshape(1, 2 * Wd), "a0": p["rwkv_a0"][l].reshape(1, 2 * Wd),
        "w2": w2, "a2": a2, "g2": p["rwkv_g2"][l],
        "ln_g": p["rwkv_ln_g"][l].reshape(1, Wd), "ln_b": p["rwkv_ln_b"][l].reshape(1, Wd),
        "seg64": _seg_matrix(Wd, RWKV_HEAD_DIM),
        "qn": jnp.tile(p["attn_qn"][l], N_HEADS).reshape(1, ATT_WIDTH),
        "kn": jnp.tile(p["attn_kn"][l], KV_HEADS).reshape(1, KV_WIDTH),
        "gla_a2": ga2, "gla_ab": p["gla_ab"][l].reshape(1, 2 * GLA_K_WIDTH),
        "gla_norm": p["gla_norm"][l].reshape(1, GLA_DV),
        "p_rwkv": p["p_rwkv"][l].astype(BF16), "p_attn": p["p_attn"][l].astype(BF16),
        "p_gla": p["p_gla"][l].astype(BF16), "w_out": p["w_out"][l].astype(BF16),
        "router_w": jnp.concatenate([p["router_w"][l], jnp.zeros((D, LANES - N_EXPERTS), F32)], axis=1),
        "router_b": jnp.concatenate([p["router_b"][l], jnp.full((LANES - N_EXPERTS,), -jnp.inf, F32)]).reshape(1, LANES),
        "layer": l, "moe_w1": p["moe_w1"], "moe_b1": p["moe_b1"][l].reshape(N_EXPERTS, 1, 2 * D_EXPERT),
        "moe_w2": p["moe_w2"], "moe_b2": p["moe_b2"][l].reshape(N_EXPERTS, 1, D_MODEL),
    }


def _trunk_layer(x, mod, lw, B, T, ctx):
    N = B * T
    rpc = N if mod.shape[0] == 1 else T
    zr, za, zg, zm = (_inproj_call(x, mod, lw["norm_mix"], lw[name], rpc, lw[name].shape[1])
                      for name in ("w_r", "w_a", "w_g", "w_m"))

    r, v, kk, g, bon, lw0, lw1, k0, k1, b0, b1 = _rwkv_prep_call(zr, T, lw)
    if ctx is None:
        s0r = jnp.zeros((B, 2, RWKV_HEADS, RWKV_HEAD_DIM, RWKV_HEAD_DIM), F32)
        s0g = jnp.zeros((B, 2, GLA_HEADS, GLA_DV, GLA_DK), F32)
        cache, tabs = None, None
    else:
        s0r = ctx["rwkv"]
        s0g = jnp.swapaxes(ctx["gla"], -1, -2)
        cache, tabs = (ctx["k"], ctx["v"], ctx["layer"]), _rope_tables(T)
    yf, yb, s_r = _rwkv_scan_call((r, v, kk, lw0, lw1, k0, k1, b0, b1), s0r, B, T)
    q, kn, kr, vext = _attn_prep_call(za, lw, T, tabs)
    oa = _attn_call(q, kr, vext, cache, B, T)
    gf, gb, s_g = _gla_scan_call(zg, s0g, lw["gla_a2"], lw["gla_ab"], B, T)
    x1, h2, ti, tg = _merge_call(yf, yb, bon, g, oa, gf, gb, zg, zm, x, mod, lw, rpc)
    y4 = _moe_ffn(h2, ti[:, :TOP_K], tg[:, :TOP_K], lw)
    x2 = _combine_call(x1, mod, y4, rpc)
    return x2, (kn, za[:, ATT_WIDTH + KV_WIDTH:], s_r, jnp.swapaxes(s_g, -1, -2))


def kernel(x_prompt, x_sample, c, cache_k, cache_v, state_rwkv, state_gla, c_ctx, ada_w, ada_b, norm_mix, norm_ffn, w_in, rwkv_mu, rwkv_w0, rwkv_w2, rwkv_a0, rwkv_a2, rwkv_g2, rwkv_kk, rwkv_ka, rwkv_rk, rwkv_ln_g, rwkv_ln_b, attn_qn, attn_kn, gla_a2, gla_ab, gla_norm, p_rwkv, p_attn, p_gla, w_out, router_w, router_b, moe_w1, moe_b1, moe_w2, moe_b2):
    p = {"w_in": w_in, "norm_mix": norm_mix, "norm_ffn": norm_ffn, "rwkv_mu": rwkv_mu, "rwkv_w0": rwkv_w0,
         "rwkv_w2": rwkv_w2, "rwkv_a0": rwkv_a0, "rwkv_a2": rwkv_a2, "rwkv_g2": rwkv_g2, "rwkv_kk": rwkv_kk,
         "rwkv_ka": rwkv_ka, "rwkv_rk": rwkv_rk, "rwkv_ln_g": rwkv_ln_g, "rwkv_ln_b": rwkv_ln_b,
         "attn_qn": attn_qn, "attn_kn": attn_kn, "gla_a2": gla_a2, "gla_ab": gla_ab, "gla_norm": gla_norm,
         "p_rwkv": p_rwkv, "p_attn": p_attn, "p_gla": p_gla, "w_out": w_out, "router_w": router_w,
         "router_b": router_b, "moe_w1": moe_w1, "moe_b1": moe_b1, "moe_w2": moe_w2, "moe_b2": moe_b2}
    Bp, Tp, D = x_prompt.shape
    Bs, Ts, _ = x_sample.shape
    L = ada_w.shape[0]
    P = cache_k.shape[2]

    cond8 = jnp.concatenate([c_ctx[None, :], c, jnp.zeros((8 - 1 - Bs, D), F32)], axis=0)
    mod = _ada_call(cond8, ada_w, ada_b).reshape(L, 8, 6, D)
    weights = [_layer_weights(p, l) for l in range(L)]
    ck = cache_k.reshape(Bs, L, P, KV_WIDTH)
    cv = cache_v.reshape(Bs, L, P, KV_WIDTH)

    yp = x_prompt.reshape(Bp * Tp, D)
    ks, vs, srs, sgs = [], [], [], []
    for l in range(L):
        yp, (kn, vh, s_r, s_g) = _trunk_layer(yp, mod[l, 0:1], weights[l], Bp, Tp, None)
        ks.append(kn.reshape(Bp, Tp, KV_HEADS, HEAD_DIM))
        vs.append(vh.reshape(Bp, Tp, KV_HEADS, HEAD_DIM))
        srs.append(s_r)
        sgs.append(s_g)

    ys = x_sample.reshape(Bs * Ts, D)
    for l in range(L):
        ctx = {"k": ck, "v": cv, "layer": l, "rwkv": state_rwkv[:, l], "gla": state_gla[:, l]}
        ys, _ = _trunk_layer(ys, mod[l, 1:1 + Bs], weights[l], Bs, Ts, ctx)

    return (yp.reshape(Bp, Tp, D), ys.reshape(Bs, Ts, D),
            jnp.stack(ks, axis=1), jnp.stack(vs, axis=1),
            jnp.stack(srs, axis=1), jnp.stack(sgs, axis=1))
```

```python
import functools

import jax
import jax.numpy as jnp
from jax import lax
from jax.experimental import pallas as pl
from jax.experimental.pallas import tpu as pltpu

D_MODEL = 1024
DEPTH = 2
GRID_W = 64
NORM_EPS = 1e-6

RWKV_HEADS = 8
RWKV_HEAD_DIM = 64
RWKV_WIDTH = RWKV_HEADS * RWKV_HEAD_DIM
DECAY_LORA = 64
ICLR_LORA = 64
GATE_LORA = 128
RWKV_GN_EPS = 6.4e-4
RWKV_COLS = 3 * RWKV_WIDTH + 2 * DECAY_LORA + 2 * ICLR_LORA + GATE_LORA

N_HEADS = 8
KV_HEADS = 2
HEAD_DIM = 64
ATT_WIDTH = N_HEADS * HEAD_DIM
KV_WIDTH = KV_HEADS * HEAD_DIM
ROPE_THETA = 10000.0

GLA_HEADS = 4
GLA_DK = 128
GLA_DV = 256
GLA_K_WIDTH = GLA_HEADS * GLA_DK
GLA_V_WIDTH = GLA_HEADS * GLA_DV
GLA_LORA = 16
GLA_GATE_NORMALIZER = 16.0

N_EXPERTS = 32
TOP_K = 4
D_EXPERT = 1024
SWIGLU_ALPHA = 1.702
SWIGLU_LIMIT = 7.0
EXPERT_BLOCK = 512

LANES = 128
CHUNK = 64
ROW_TILE = 256
INPROJ_TILE = 512
Q_TILE = 128
ATT_STACK = 2
VMEM_LIMIT = 52 * 1024 * 1024
LOG2_E = 1.4426950408889634

F32 = jnp.float32
BF16 = jnp.bfloat16
HI = lax.Precision.HIGHEST


def _dot(a, b, prec=None):
    return jnp.dot(a, b, preferred_element_type=F32, precision=prec)


def _dot_nt(a, b, prec=None):
    return lax.dot_general(a, b, (((1,), (1,)), ((), ())), preferred_element_type=F32, precision=prec)


def _dot_tn(a, b, prec=None):
    return lax.dot_general(a, b, (((0,), (0,)), ((), ())), preferred_element_type=F32, precision=prec)


_NN = (((1,), (0,)), ((), ()))
_NT = (((1,), (1,)), ((), ()))
_TN = (((0,), (0,)), ((), ()))


def _dg(a, b, dn):
    return lax.dot_general(a, b, dn, preferred_element_type=F32)


def _split(x):
    h = x.astype(BF16)
    return h, (x - h.astype(F32)).astype(BF16)


def _mm3(a, b, dn=_NN):
    return _dg(a[0], b[0], dn) + _dg(a[0], b[1], dn) + _dg(a[1], b[0], dn)


def _mm3s(a, b, dn=_NN):
    m = a[0].shape[0]
    top = _dg(jnp.concatenate([a[0], a[1]], axis=0), b[0], dn)
    return top[:m] + top[m:] + _dg(a[0], b[1], dn)


def _cumsum_rows(mask, x):
    m = jnp.where(mask, 1.0, 0.0).astype(BF16)
    h = x.astype(BF16)
    r1 = x - h.astype(F32)
    mid = r1.astype(BF16)
    lo = (r1 - mid.astype(F32)).astype(BF16)
    return _dg(m, h, _NN) + _dg(m, mid, _NN) + _dg(m, lo, _NN)


def _seg_sum(x, seg_bf16):
    xh = x.astype(BF16)
    xl = (x - xh.astype(F32)).astype(BF16)
    return _dot(xh, seg_bf16) + _dot(xl, seg_bf16)


def _sigmoid(x):
    return 1.0 / (1.0 + jnp.exp(-x))


def _params(sem):
    return pltpu.CompilerParams(dimension_semantics=sem, vmem_limit_bytes=VMEM_LIMIT)


def _const_spec(shape):
    nd = len(shape)
    return pl.BlockSpec(shape, lambda *_: (0,) * nd)


def _ada_kernel(c_ref, w_ref, b_ref, o_ref):
    c = c_ref[...]
    s = c * _sigmoid(c)
    o_ref[...] = _mm3(_split(s), _split(w_ref[...])) + b_ref[...]


def _ada_call(cond8, ada_w, ada_b):
    L, D, W = ada_w.shape
    tn = 1536
    return pl.pallas_call(
        _ada_kernel,
        grid=(L, W // tn),
        in_specs=[pl.BlockSpec((8, D), lambda l, j: (0, 0)),
                  pl.BlockSpec((None, D, tn), lambda l, j: (l, 0, j)),
                  pl.BlockSpec((None, 1, tn), lambda l, j: (l, 0, j))],
        out_specs=pl.BlockSpec((None, 8, tn), lambda l, j: (l, 0, j)),
        out_shape=jax.ShapeDtypeStruct((L, 8, W), F32),
        compiler_params=_params(("parallel", "parallel")),
        name="ada",
    )(cond8, ada_w, ada_b.reshape(L, 1, W))


def _inproj_kernel(x_ref, m_ref, g_ref, w_ref, o_ref, h_scr):
    @pl.when(pl.program_id(1) == 0)
    def _():
        x = x_ref[...]
        y = x * lax.rsqrt(jnp.mean(x * x, axis=-1, keepdims=True) + NORM_EPS) * g_ref[...]
        h = y * (1.0 + m_ref[1:2, :]) + m_ref[0:1, :]
        h_scr[...] = h.astype(BF16)

    o_ref[...] = _dot(h_scr[...], w_ref[...])


def _inproj_call(x, mod, gain, w, rows_per_cond, tn):
    N, D = x.shape
    W = w.shape[1]
    tm = INPROJ_TILE
    return pl.pallas_call(
        _inproj_kernel,
        grid=(N // tm, W // tn),
        in_specs=[pl.BlockSpec((tm, D), lambda i, j: (i, 0)),
                  pl.BlockSpec((None, 6, D), lambda i, j: ((i * tm) // rows_per_cond, 0, 0)),
                  pl.BlockSpec((1, D), lambda i, j: (0, 0)),
                  pl.BlockSpec((D, tn), lambda i, j: (0, j))],
        out_specs=pl.BlockSpec((tm, tn), lambda i, j: (i, j)),
        out_shape=jax.ShapeDtypeStruct((N, W), F32),
        scratch_shapes=[pltpu.VMEM((tm, D), BF16)],
        compiler_params=_params(("parallel", "arbitrary")),
        name="inproj",
    )(x, mod, gain, w)


def _rwkv_prep_kernel(z_ref, zp_ref, zn_ref, mu_ref, kkp_ref, ka_ref, rk_ref, w0_ref, a0_ref,
                      w2_ref, a2_ref, g2_ref, seg_ref,
                      r_o, v_o, kk_o, g_o, bon_o, lw0_o, lw1_o, k0_o, k1_o, b0_o, b1_o,
                      *, tm, blocks_per_seq):
    i = pl.program_id(0)
    z = z_ref[...]
    row = lax.broadcasted_iota(jnp.int32, (tm, 1), 0)
    first = (i % blocks_per_seq) == 0
    last = (i % blocks_per_seq) == blocks_per_seq - 1
    pz = jnp.where(first, 0.0, zp_ref[7:8, :])
    nz = jnp.where(last, 0.0, zn_ref[0:1, :])
    prev = jnp.where(row == 0, pz, pltpu.roll(z, 1, 0))
    nxt = jnp.where(row == tm - 1, nz, pltpu.roll(z, tm - 1, 0))
    zm = z + mu_ref[...] * (0.5 * (prev + nxt) - z)

    Wd = RWKV_WIDTH
    r = zm[:, 0:Wd]
    k = zm[:, Wd:2 * Wd]
    v = zm[:, 2 * Wd:3 * Wd]
    lw = zm[:, 3 * Wd:3 * Wd + 128]
    la = zm[:, 3 * Wd + 128:3 * Wd + 256]
    lg = zm[:, 3 * Wd + 256:3 * Wd + 384]
    seg = seg_ref[...]

    kk = k * kkp_ref[...]
    kk = kk / jnp.maximum(jnp.sqrt(_seg_sum(kk * kk, seg)), 1e-12)
    w_raw = _mm3(_split(jnp.tanh(lw)), _split(w2_ref[...])) + w0_ref[...]
    logw = -_sigmoid(w_raw) * 0.6065306597126334
    a = _sigmoid(_mm3(_split(la), _split(a2_ref[...])) + a0_ref[...])
    ka = ka_ref[...]
    rk = rk_ref[...]
    bonus = jnp.zeros_like(r)
    for d, (lw_o, k_o, b_o) in enumerate(((lw0_o, k0_o, b0_o), (lw1_o, k1_o, b1_o))):
        a_d = a[:, d * Wd:(d + 1) * Wd]
        k_d = k * (1.0 + (a_d - 1.0) * ka)
        lw_o[...] = logw[:, d * Wd:(d + 1) * Wd]
        k_o[...] = k_d
        b_o[...] = kk * a_d
        bonus = bonus + _seg_sum(r * k_d * rk, seg) * v
    r_o[...] = r
    v_o[...] = v
    kk_o[...] = kk
    bon_o[...] = bonus
    g_o[...] = _mm3(_split(_sigmoid(lg)), _split(g2_ref[...]))


def _rwkv_prep_call(zr, seq_len, lw):
    N, Wz = zr.shape
    tm = ROW_TILE
    bps = seq_len // tm
    Wd = RWKV_WIDTH
    nb8 = N // 8
    row_spec = pl.BlockSpec((tm, Wd), lambda i: (i, 0))
    out = jax.ShapeDtypeStruct((N, Wd), F32)
    return pl.pallas_call(
        functools.partial(_rwkv_prep_kernel, tm=tm, blocks_per_seq=bps),
        grid=(N // tm,),
        in_specs=[pl.BlockSpec((tm, Wz), lambda i: (i, 0)),
                  pl.BlockSpec((8, Wz), lambda i: (jnp.maximum(i * (tm // 8) - 1, 0), 0)),
                  pl.BlockSpec((8, Wz), lambda i: (jnp.minimum((i + 1) * (tm // 8), nb8 - 1), 0)),
                  _const_spec((1, Wz)), _const_spec((1, Wd)), _const_spec((1, Wd)), _const_spec((1, Wd)),
                  _const_spec((1, 2 * Wd)), _const_spec((1, 2 * Wd)),
                  _const_spec((128, 2 * Wd)), _const_spec((128, 2 * Wd)), _const_spec((128, Wd)),
                  _const_spec((Wd, Wd))],
        out_specs=[row_spec] * 11,
        out_shape=[out] * 11,
        compiler_params=_params(("parallel",)),
        name="rwkv_prep",
    )(zr, zr, zr, lw["mu"], lw["kk"], lw["ka"], lw["rk"], lw["w0"], lw["a0"],
      lw["w2"], lw["a2"], lw["g2"], lw["seg64"])


def _tri_masks(d):
    ti = lax.broadcasted_iota(jnp.int32, (CHUNK, CHUNK), 0)
    si = lax.broadcasted_iota(jnp.int32, (CHUNK, CHUNK), 1)
    if d == 0:
        return si <= ti, si < ti
    return si >= ti, si > ti


def _rwkv_scan_kernel(rf, vf, kkf, lwf, kf, bf, rb, vb, kkb, lwb, kb, bb, s0_ref,
                      yf_o, yb_o, st_o, s_scr, *, nc):
    i = pl.program_id(1)

    @pl.when(i == 0)
    def _():
        s_scr[...] = s0_ref[...]

    C = CHUNK
    NP = RWKV_HEADS // 2
    sls = [slice(q * LANES, (q + 1) * LANES) for q in range(NP)]
    low = lax.broadcasted_iota(jnp.int32, (C, LANES), 1) < RWKV_HEAD_DIM

    def bd(x):
        zero = jnp.zeros_like(x)
        return jnp.concatenate([jnp.where(low, x, zero), jnp.where(low, zero, x)], axis=0)

    def bd2(x):
        return bd(x[0]), bd(x[1])

    eye = jnp.concatenate([(lax.broadcasted_iota(jnp.int32, (C, C), 0)
                            == lax.broadcasted_iota(jnp.int32, (C, C), 1)).astype(F32)] * 2, axis=1)
    dirs = ((rf, vf, kkf, lwf, kf, bf), (rb, vb, kkb, lwb, kb, bb))
    pre = []
    for d, (r_ref, v_ref, kk_ref, lw_ref, k_ref, b_ref) in enumerate(dirs):
        incl, strict = _tri_masks(d)
        lw = lw_ref[...]
        b = _cumsum_rows(incl, lw)
        btot = jnp.sum(lw, axis=0, keepdims=True)
        r, v, kk, k, beta = r_ref[...], v_ref[...], kk_ref[...], k_ref[...], b_ref[...]
        nb = jnp.exp(-b)
        eb = jnp.exp(btot - b)
        gtot = jnp.exp(btot)
        ar = _split(jnp.concatenate([-kk * jnp.exp(b - lw), r * jnp.exp(b)], axis=0))
        bbar = _split(beta * nb)
        kbar = (k * nb).astype(BF16)
        hat = jnp.concatenate([beta * eb, k * eb], axis=0).astype(BF16)
        v16 = v.astype(BF16)

        ti = lax.broadcasted_iota(jnp.int32, (C, LANES), 0)
        si = lax.broadcasted_iota(jnp.int32, (C, LANES), 1) % RWKV_HEAD_DIM
        incl2, strict2 = ((si <= ti, si < ti) if d == 0 else (si >= ti, si > ti))
        g1 = [_mm3s((ar[0][:, s], ar[1][:, s]), bd2((bbar[0][:, s], bbar[1][:, s])), _NT) for s in sls]
        g2 = [_dg(ar[0][:, s], bd(kbar[:, s]), _NT) for s in sls]
        a_ab = [jnp.where(strict2, g[:C], 0.0) for g in g1]
        a_rb = [jnp.where(incl2, g[C:], 0.0).astype(BF16) for g in g1]
        a_ak = [jnp.where(strict2, g[:C], 0.0).astype(BF16) for g in g2]
        a_rk = [jnp.where(incl2, g[C:], 0.0).astype(BF16) for g in g2]
        vbd = [bd(v16[:, s]) for s in sls]
        av = [_dg(a, vv, _NN) for a, vv in zip(a_ak, vbd)]
        t = [eye + a for a in a_ab]
        p = [_split(a) for a in a_ab]
        p = [_split(_mm3s(pp, bd2(pp))) for pp in p]
        for _ in range(4):
            ts = [_split(tt) for tt in t]
            res = [_mm3s((jnp.concatenate([pp[0], tt[0]], axis=0), jnp.concatenate([pp[1], tt[1]], axis=0)),
                         bd2(pp)) for pp, tt in zip(p, ts)]
            p = [_split(rr[:C]) for rr in res]
            t = [tt + rr[C:] for tt, rr in zip(t, res)]
        t = [tt + _mm3s(_split(tt), bd2(pp)) for tt, pp in zip(t, p)]
        pre.append((ar[0], av, [_split(tt) for tt in t], a_rb, a_rk, vbd, v16, hat, gtot))

    hd = [(d, q) for d in range(2) for q in range(NP)]
    s_old = [s_scr[d, q] for d, q in hd]
    z = [_dg(pre[d][0][:, sls[q]], bd(s.astype(BF16)), _NT) for (d, q), s in zip(hd, s_old)]
    u = [_mm3s(pre[d][2][q], bd2(_split(zz[:C] + pre[d][1][q]))).astype(BF16) for (d, q), zz in zip(hd, z)]
    ys = [zz[C:] + _dg(jnp.concatenate([pre[d][3][q], pre[d][4][q]], axis=1),
                       jnp.concatenate([bd(uu), pre[d][5][q]], axis=0), _NN)
          for (d, q), zz, uu in zip(hd, z, u)]
    cross = [_dg(jnp.concatenate([uu, pre[d][6][:, sls[q]]], axis=0), pre[d][7][:, sls[q]], _TN)
             for (d, q), uu in zip(hd, u)]
    for (d, q), yy, s, cc in zip(hd, ys, s_old, cross):
        (yf_o, yb_o)[d][:, sls[q]] = yy
        s_scr[d, q] = s * pre[d][8][:, sls[q]] + jnp.where(low, cc[:C], cc[C:])

    @pl.when(i == nc - 1)
    def _():
        st_o[...] = s_scr[...]


def _rwkv_scan_call(prep, s0, B, T):
    r, v, kk, lw0, lw1, k0, k1, b0, b1 = prep
    N, Wd = r.shape
    nc = T // CHUNK
    H, Dh = RWKV_HEADS, RWKV_HEAD_DIM
    pair_shape = (2, H // 2, Dh, 2 * Dh)
    fwd = pl.BlockSpec((CHUNK, Wd), lambda b, i: (b * nc + i, 0))
    bwd = pl.BlockSpec((CHUNK, Wd), lambda b, i: (b * nc + nc - 1 - i, 0))
    st = pl.BlockSpec((None,) + pair_shape, lambda b, i: (b, 0, 0, 0, 0))
    y = jax.ShapeDtypeStruct((N, Wd), F32)
    s0p = s0.reshape(B, 2, H // 2, 2, Dh, Dh).transpose(0, 1, 2, 4, 3, 5).reshape((B,) + pair_shape)
    yf, yb, stp = pl.pallas_call(
        functools.partial(_rwkv_scan_kernel, nc=nc),
        grid=(B, nc),
        in_specs=[fwd] * 6 + [bwd] * 6 + [st],
        out_specs=[fwd, bwd, st],
        out_shape=[y, y, jax.ShapeDtypeStruct((B,) + pair_shape, F32)],
        scratch_shapes=[pltpu.VMEM(pair_shape, F32)],
        compiler_params=_params(("parallel", "arbitrary")),
        name="rwkv_scan",
    )(r, v, kk, lw0, k0, b0, r, v, kk, lw1, k1, b1, s0p)
    st_out = stp.reshape(B, 2, H // 2, Dh, 2, Dh).transpose(0, 1, 2, 4, 3, 5).reshape(B, 2, H, Dh, Dh)
    return yf, yb, st_out


def _gla_scan_kernel(xf, glf, xb, glb, a2_ref, ab_ref, s0_ref, of_o, ob_o, st_o, s_scr, *, nc):
    i = pl.program_id(1)

    @pl.when(i == 0)
    def _():
        s_scr[...] = s0_ref[...]

    Kw = GLA_K_WIDTH
    work = []
    for d, (x_ref, gl_ref) in enumerate(((xf, glf), (xb, glb))):
        incl, _ = _tri_masks(d)
        xa = (_mm3(_split(gl_ref[...]), _split(a2_ref[:, d * Kw:(d + 1) * Kw]))
              + ab_ref[:, d * Kw:(d + 1) * Kw])
        log_a = (jnp.minimum(xa, 0.0) - jnp.log1p(jnp.exp(-jnp.abs(xa)))) * (1.0 / GLA_GATE_NORMALIZER)
        b = _cumsum_rows(incl, log_a)
        bl = jnp.sum(log_a, axis=0, keepdims=True)
        q = x_ref[:, 0:Kw] * (GLA_DK ** -0.5)
        k = x_ref[:, Kw:2 * Kw]
        q_in = (q * jnp.exp(b)).astype(BF16)
        k_in = (k * jnp.exp(-b)).astype(BF16)
        k_end = (k * jnp.exp(bl - b)).astype(BF16)
        gtot = jnp.exp(bl)
        for h in range(GLA_HEADS):
            ks = slice(h * GLA_DK, (h + 1) * GLA_DK)
            v_h = x_ref[:, 2 * Kw + h * GLA_DV:2 * Kw + (h + 1) * GLA_DV].astype(BF16)
            att = jnp.where(incl, _dg(q_in[:, ks], k_in[:, ks], _NT), 0.0).astype(BF16)
            work.append((d, h, _dg(att, v_h, _NN), q_in[:, ks], gtot[:, ks], _dg(v_h, k_end[:, ks], _TN)))

    s_old = [s_scr[d, h] for d, h, *_ in work]
    outs = [o_in + _dg(qh, s.astype(BF16), _NT) for (_, _, o_in, qh, _, _), s in zip(work, s_old)]
    for (d, h, _, _, g, kv), s, o in zip(work, s_old, outs):
        (of_o, ob_o)[d][:, h * GLA_DV:(h + 1) * GLA_DV] = o
        s_scr[d, h] = s * g + kv

    @pl.when(i == nc - 1)
    def _():
        st_o[...] = s_scr[...]


def _gla_scan_call(zg, s0t, a2, ab, B, T):
    N = zg.shape[0]
    nc = T // CHUNK
    Wx = 2 * GLA_K_WIDTH + GLA_V_WIDTH
    gl_col = (Wx + GLA_V_WIDTH) // LANES
    xf = pl.BlockSpec((CHUNK, Wx), lambda b, i: (b * nc + i, 0))
    xb = pl.BlockSpec((CHUNK, Wx), lambda b, i: (b * nc + nc - 1 - i, 0))
    gf = pl.BlockSpec((CHUNK, LANES), lambda b, i: (b * nc + i, gl_col))
    gb = pl.BlockSpec((CHUNK, LANES), lambda b, i: (b * nc + nc - 1 - i, gl_col))
    of = pl.BlockSpec((CHUNK, GLA_V_WIDTH), lambda b, i: (b * nc + i, 0))
    ob = pl.BlockSpec((CHUNK, GLA_V_WIDTH), lambda b, i: (b * nc + nc - 1 - i, 0))
    st = pl.BlockSpec((None, 2, GLA_HEADS, GLA_DV, GLA_DK), lambda b, i: (b, 0, 0, 0, 0))
    o = jax.ShapeDtypeStruct((N, GLA_V_WIDTH), F32)
    return pl.pallas_call(
        functools.partial(_gla_scan_kernel, nc=nc),
        grid=(B, nc),
        in_specs=[xf, gf, xb, gb, _const_spec((LANES, 2 * GLA_K_WIDTH)), _const_spec((1, 2 * GLA_K_WIDTH)), st],
        out_specs=[of, ob, st],
        out_shape=[o, o, jax.ShapeDtypeStruct(s0t.shape, F32)],
        scratch_shapes=[pltpu.VMEM((2, GLA_HEADS, GLA_DV, GLA_DK), F32)],
        compiler_params=_params(("parallel", "arbitrary")),
        name="gla_scan",
    )(zg, zg, zg, zg, a2, ab, s0t)


def _rope(x, cos, sin):
    n = x.shape[1]
    lane = lax.broadcasted_iota(jnp.int32, x.shape, 1)
    up = pltpu.roll(x, n - 16, 1)
    dn = pltpu.roll(x, 16, 1)
    sw = jnp.where((lane % 32) < 16, up, dn)
    reps = n // LANES
    if reps > 1:
        cos = jnp.concatenate([cos] * reps, axis=1)
        sin = jnp.concatenate([sin] * reps, axis=1)
    return x * cos + sw * sin


def _attn_prep_kernel(*refs, rope):
    if rope:
        z_ref, qn_ref, kn_ref, segq_ref, segk_ref, cos_ref, sin_ref, q_o, kn_o, kr_o, ve_o = refs
    else:
        z_ref, qn_ref, kn_ref, segq_ref, segk_ref, q_o, kn_o, kr_o, ve_o = refs
    zq = z_ref[:, 0:ATT_WIDTH]
    zk = z_ref[:, ATT_WIDTH:ATT_WIDTH + KV_WIDTH]
    qh = zq * lax.rsqrt(_seg_sum(zq * zq, segq_ref[...]) * (1.0 / HEAD_DIM) + NORM_EPS) * qn_ref[...]
    kh = zk * lax.rsqrt(_seg_sum(zk * zk, segk_ref[...]) * (1.0 / HEAD_DIM) + NORM_EPS) * kn_ref[...]
    kn_o[...] = kh
    if rope:
        qh = _rope(qh, cos_ref[...], sin_ref[...])
        kh = _rope(kh, cos_ref[...], sin_ref[...])
    q_o[...] = (qh * (LOG2_E * HEAD_DIM ** -0.5)).astype(BF16)
    kr_o[...] = kh.astype(BF16)
    v = z_ref[:, ATT_WIDTH + KV_WIDTH:ATT_WIDTH + 2 * KV_WIDTH]
    group = lax.broadcasted_iota(jnp.int32, v.shape, 1) // HEAD_DIM
    for g in range(KV_HEADS):
        ve_o[:, g * LANES:(g + 1) * LANES] = jnp.where(group == g, v, 1.0).astype(BF16)


def _attn_prep_call(za, lw, seq_len, rope_tabs):
    N, Wz = za.shape
    tm = ROW_TILE
    bps = seq_len // tm
    rope = rope_tabs is not None
    in_specs = [pl.BlockSpec((tm, Wz), lambda i: (i, 0)),
                _const_spec((1, ATT_WIDTH)), _const_spec((1, KV_WIDTH)),
                _const_spec((ATT_WIDTH, ATT_WIDTH)), _const_spec((KV_WIDTH, KV_WIDTH))]
    args = [za, lw["qn"], lw["kn"], lw["seg64"], lw["seg64"][:KV_WIDTH, :KV_WIDTH]]
    if rope:
        in_specs += [pl.BlockSpec((tm, LANES), lambda i: (i % bps, 0))] * 2
        args += list(rope_tabs)
    return pl.pallas_call(
        functools.partial(_attn_prep_kernel, rope=rope),
        grid=(N // tm,),
        in_specs=in_specs,
        out_specs=[pl.BlockSpec((tm, ATT_WIDTH), lambda i: (i, 0)),
                   pl.BlockSpec((tm, KV_WIDTH), lambda i: (i, 0)),
                   pl.BlockSpec((tm, KV_WIDTH), lambda i: (i, 0)),
                   pl.BlockSpec((tm, KV_HEADS * LANES), lambda i: (i, 0))],
        out_shape=[jax.ShapeDtypeStruct((N, ATT_WIDTH), BF16),
                   jax.ShapeDtypeStruct((N, KV_WIDTH), F32),
                   jax.ShapeDtypeStruct((N, KV_WIDTH), BF16),
                   jax.ShapeDtypeStruct((N, KV_HEADS * LANES), BF16)],
        compiler_params=_params(("parallel",)),
        name="attn_prep",
    )(*args)


def _attn_kernel(*refs, cached):
    if cached:
        q_ref, k_ref, v_ref, ck_ref, cv_ref, o_ref = refs
    else:
        q_ref, k_ref, v_ref, o_ref = refs
    G = N_HEADS // KV_HEADS
    tq = q_ref.shape[0]
    for g in range(KV_HEADS):
        gs = slice(g * HEAD_DIM, (g + 1) * HEAD_DIM)
        other = slice((1 - g) * HEAD_DIM, (2 - g) * HEAD_DIM)
        kg = k_ref[:, gs]
        vg = v_ref[:, g * LANES:(g + 1) * LANES]
        if cached:
            cv = cv_ref[...]
            own = (lax.broadcasted_iota(jnp.int32, cv.shape, 1) // HEAD_DIM) == g
            ckg = ck_ref[:, gs].astype(BF16)
            cvg = jnp.where(own, cv, 1.0).astype(BF16)
        for h0 in range(g * G, (g + 1) * G, ATT_STACK):
            heads = range(h0, h0 + ATT_STACK)
            qs = jnp.concatenate([q_ref[:, h * HEAD_DIM:(h + 1) * HEAD_DIM] for h in heads], axis=0)
            s1 = _dg(qs, kg, _NT)
            m = jnp.max(s1, axis=-1, keepdims=True)
            if cached:
                s2 = _dg(qs, ckg, _NT)
                m = jnp.maximum(m, jnp.max(s2, axis=-1, keepdims=True))
            oe = _dg(jnp.exp2(s1 - m).astype(BF16), vg, _NN)
            if cached:
                oe = oe + _dg(jnp.exp2(s2 - m).astype(BF16), cvg, _NN)
            o = oe[:, gs] / oe[:, other]
            for j, h in enumerate(heads):
                o_ref[:, h * HEAD_DIM:(h + 1) * HEAD_DIM] = o[j * tq:(j + 1) * tq]


def _attn_call(q, k, vext, cache, B, T):
    N = q.shape[0]
    tq = Q_TILE
    nq = T // tq
    in_specs = [pl.BlockSpec((tq, ATT_WIDTH), lambda b, i: (b * nq + i, 0)),
                pl.BlockSpec((T, KV_WIDTH), lambda b, i: (b, 0)),
                pl.BlockSpec((T, KV_HEADS * LANES), lambda b, i: (b, 0))]
    args = [q, k, vext]
    if cache is not None:
        ck, cv, layer = cache
        P = ck.shape[2]
        cspec = pl.BlockSpec((None, None, P, KV_WIDTH), lambda b, i: (b, layer, 0, 0))
        in_specs += [cspec, cspec]
        args += [ck, cv]
    return pl.pallas_call(
        functools.partial(_attn_kernel, cached=cache is not None),
        grid=(B, nq),
        in_specs=in_specs,
        out_specs=pl.BlockSpec((tq, ATT_WIDTH), lambda b, i: (b * nq + i, 0)),
        out_shape=jax.ShapeDtypeStruct((N, ATT_WIDTH), F32),
        compiler_params=_params(("parallel", "arbitrary")),
        name="attention",
    )(*args)


def _merge_kernel(yf, yb, bon, g, oa, gf, gb, gr, mr, ma, mg, x_ref, m_ref,
                  lng, lnb, gn, nf, seg_ref, pr, pa, pg, wo, rw, rb,
                  x1_o, h2_o, ti_o, tg_o):
    seg = seg_ref[...]
    inv = 1.0 / RWKV_HEAD_DIM
    y = yf[...] + yb[...]
    mu = _seg_sum(y, seg) * inv
    yc = y - mu
    var = _seg_sum(yc * yc, seg) * inv
    o_r = (yc * lax.rsqrt(var + RWKV_GN_EPS) * lng[...] + lnb[...] + bon[...]) * g[...]

    gate_r = gr[...]
    silu_r = gate_r * _sigmoid(gate_r)
    gnv = gn[...]
    cols = []
    for h in range(GLA_HEADS):
        vs = slice(h * GLA_DV, (h + 1) * GLA_DV)
        o = gf[:, vs] + gb[:, vs]
        ms = jnp.mean(o * o, axis=-1, keepdims=True)
        cols.append(o * lax.rsqrt(ms + NORM_EPS) * gnv * silu_r[:, vs])
    o_g = jnp.concatenate(cols, axis=1)

    merged = (_sigmoid(mr[...]) * _dot(o_r.astype(BF16), pr[...])
              + _sigmoid(ma[...]) * _dot(oa[...].astype(BF16), pa[...])
              + _sigmoid(mg[...]) * _dot(o_g.astype(BF16), pg[...]))
    x1 = x_ref[...] + m_ref[2:3, :] * _dot(merged.astype(BF16), wo[...])
    x1_o[...] = x1
    hn = x1 * lax.rsqrt(jnp.mean(x1 * x1, axis=-1, keepdims=True) + NORM_EPS) * nf[...]
    h2 = hn * (1.0 + m_ref[4:5, :]) + m_ref[3:4, :]
    h2_o[...] = h2

    logits = _mm3(_split(h2), _split(rw[...])) + rb[...]
    lane = lax.broadcasted_iota(jnp.int32, logits.shape, 1)
    vals, idxs = [], []
    for _ in range(TOP_K):
        m = jnp.max(logits, axis=-1, keepdims=True)
        idx = jnp.min(jnp.where(logits == m, lane, LANES), axis=-1, keepdims=True)
        vals.append(m)
        idxs.append(idx)
        logits = jnp.where(lane == idx, -jnp.inf, logits)
    es = [jnp.exp(vv - vals[0]) for vv in vals]
    den = es[0] + es[1] + es[2] + es[3]
    ti = jnp.zeros(logits.shape, jnp.int32)
    tg = jnp.zeros(logits.shape, F32)
    for j in range(TOP_K):
        ti = jnp.where(lane == j, idxs[j], ti)
        tg = jnp.where(lane == j, es[j] / den, tg)
    ti_o[...] = ti
    tg_o[...] = tg


def _merge_call(yf, yb, bon, g, oa, gf, gb, zg, zm, x, mod, lw, rows_per_cond):
    N, D = x.shape
    tm = ROW_TILE
    Wd = RWKV_WIDTH

    def rows(w, col=0):
        return pl.BlockSpec((tm, w), lambda i: (i, col))

    in_specs = [rows(Wd)] * 4 + [rows(ATT_WIDTH), rows(GLA_V_WIDTH), rows(GLA_V_WIDTH),
                                 rows(GLA_V_WIDTH, 2), rows(D, 0), rows(D, 1), rows(D, 2), rows(D),
                                 pl.BlockSpec((None, 6, D), lambda i: ((i * tm) // rows_per_cond, 0, 0)),
                                 _const_spec((1, Wd)), _const_spec((1, Wd)), _const_spec((1, GLA_DV)),
                                 _const_spec((1, D)), _const_spec((Wd, Wd)),
                                 _const_spec((Wd, D)), _const_spec((ATT_WIDTH, D)), _const_spec((GLA_V_WIDTH, D)),
                                 _const_spec((D, D)), _const_spec((D, LANES)), _const_spec((1, LANES))]
    return pl.pallas_call(
        _merge_kernel,
        grid=(N // tm,),
        in_specs=in_specs,
        out_specs=[rows(D), rows(D), rows(LANES), rows(LANES)],
        out_shape=[jax.ShapeDtypeStruct((N, D), F32), jax.ShapeDtypeStruct((N, D), F32),
                   jax.ShapeDtypeStruct((N, LANES), jnp.int32), jax.ShapeDtypeStruct((N, LANES), F32)],
        compiler_params=_params(("parallel",)),
        name="merge",
    )(yf, yb, bon, g, oa, gf, gb, zg, zm, zm, zm, x, mod,
      lw["ln_g"], lw["ln_b"], lw["gla_norm"], lw["norm_ffn"], lw["seg64"],
      lw["p_rwkv"], lw["p_attn"], lw["p_gla"], lw["w_out"], lw["router_w"], lw["router_b"])


def _moe_kernel(be_ref, na_ref, x_ref, gate_ref, w1_ref, b1_ref, w2_ref, b2_ref, o_ref, w1_scr, w2_scr):
    i = pl.program_id(0)

    @pl.when(jnp.logical_or(i == 0, be_ref[i] != be_ref[jnp.maximum(i - 1, 0)]))
    def _():
        w1_scr[...] = w1_ref[...].astype(BF16)
        w2_scr[...] = w2_ref[...].astype(BF16)

    @pl.when(i < na_ref[0])
    def _():
        z = _dot(x_ref[...].astype(BF16), w1_scr[...]) + b1_ref[...]
        glu = jnp.minimum(z[:, :D_EXPERT], SWIGLU_LIMIT)
        lin = jnp.clip(z[:, D_EXPERT:], -SWIGLU_LIMIT, SWIGLU_LIMIT)
        act = glu * _sigmoid(SWIGLU_ALPHA * glu) * (lin + 1.0)
        o_ref[...] = (_dot(act.astype(BF16), w2_scr[...]) + b2_ref[...]) * gate_ref[...]

    @pl.when(i >= na_ref[0])
    def _():
        o_ref[...] = jnp.zeros_like(o_ref)


def _moe_call(xg, row_gate, block_exp, n_active, layer, w1, b1, w2, b2):
    R, D = xg.shape
    tb = EXPERT_BLOCK
    F2 = w1.shape[3]
    grid_spec = pltpu.PrefetchScalarGridSpec(
        num_scalar_prefetch=2,
        grid=(R // tb,),
        in_specs=[pl.BlockSpec((tb, D), lambda i, be, na: (i, 0)),
                  pl.BlockSpec((tb, 1), lambda i, be, na: (i, 0)),
                  pl.BlockSpec((None, None, D, F2), lambda i, be, na: (layer, be[i], 0, 0)),
                  pl.BlockSpec((None, 1, F2), lambda i, be, na: (be[i], 0, 0)),
                  pl.BlockSpec((None, None, F2 // 2, D), lambda i, be, na: (layer, be[i], 0, 0)),
                  pl.BlockSpec((None, 1, D), lambda i, be, na: (be[i], 0, 0))],
        out_specs=pl.BlockSpec((tb, D), lambda i, be, na: (i, 0)),
        scratch_shapes=[pltpu.VMEM((D, F2), BF16), pltpu.VMEM((F2 // 2, D), BF16)],
    )
    return pl.pallas_call(
        _moe_kernel,
        grid_spec=grid_spec,
        out_shape=jax.ShapeDtypeStruct((R, D), F32),
        compiler_params=_params(("arbitrary",)),
        name="moe_experts",
    )(block_exp, n_active, xg, row_gate, w1, b1, w2, b2)


def _combine_kernel(x1_ref, m_ref, y0_ref, y1_ref, y2_ref, y3_ref, o_ref):
    y = (y0_ref[...] + y1_ref[...]) + (y2_ref[...] + y3_ref[...])
    o_ref[...] = x1_ref[...] + m_ref[5:6, :] * y


def _combine_call(x1, mod, ys, rows_per_cond):
    N, D = x1.shape
    tm = ROW_TILE
    rows = pl.BlockSpec((tm, D), lambda i: (i, 0))
    return pl.pallas_call(
        _combine_kernel,
        grid=(N // tm,),
        in_specs=[rows, pl.BlockSpec((None, 6, D), lambda i: ((i * tm) // rows_per_cond, 0, 0))] + [rows] * TOP_K,
        out_specs=rows,
        out_shape=jax.ShapeDtypeStruct((N, D), F32),
        compiler_params=_params(("parallel",)),
        name="combine",
    )(x1, mod, *ys)


def _moe_ffn(h2, top_i, top_g, lw):
    N, D = h2.shape
    n_as = N * TOP_K
    n_blocks = -(-n_as // EXPERT_BLOCK) + N_EXPERTS
    R = n_blocks * EXPERT_BLOCK
    ex = jnp.arange(N_EXPERTS, dtype=jnp.int32)

    def lut(onehot, table):
        return jnp.sum(jnp.where(onehot, table[None, :], 0), axis=1)

    flat_e = top_i.reshape(n_as)
    flat_g = top_g.reshape(n_as)
    oh_a = flat_e[:, None] == ex[None, :]
    counts = jnp.sum(oh_a, axis=0, dtype=jnp.int32)
    starts = jnp.cumsum(counts) - counts
    padded = (counts + EXPERT_BLOCK - 1) // EXPERT_BLOCK * EXPERT_BLOCK
    pends = jnp.cumsum(padded)
    pstarts = pends - padded
    order = jnp.argsort(flat_e).astype(jnp.int32)
    rank = jnp.argsort(order).astype(jnp.int32)
    pos = rank + lut(oh_a, pstarts - starts)
    blk_start = jnp.arange(n_blocks, dtype=jnp.int32) * EXPERT_BLOCK
    block_exp = jnp.minimum(jnp.sum(pends[None, :] <= blk_start[:, None], axis=1),
                            N_EXPERTS - 1).astype(jnp.int32)
    n_active = (pends[-1] // EXPERT_BLOCK).astype(jnp.int32).reshape(1)
    oh_p = jnp.repeat(block_exp, EXPERT_BLOCK)[:, None] == ex[None, :]
    idx = jnp.arange(R, dtype=jnp.int32) - lut(oh_p, pstarts)
    valid = (idx >= 0) & (idx < lut(oh_p, counts))
    asg = order[jnp.clip(lut(oh_p, starts) + idx, 0, n_as - 1)]
    row_tok = jnp.where(valid, asg // TOP_K, 0)
    row_gate = jnp.where(valid, flat_g[asg], 0.0)
    out = _moe_call(h2[row_tok], row_gate.reshape(R, 1), block_exp, n_active,
                    lw["layer"], lw["moe_w1"], lw["moe_b1"], lw["moe_w2"], lw["moe_b2"])
    pos = pos.reshape(N, TOP_K)
    return [out[pos[:, j]] for j in range(TOP_K)]


def _seg_matrix(n, seg):
    idx = jnp.arange(n) // seg
    return (idx[:, None] == idx[None, :]).astype(BF16)


def _rope_tables(T):
    quarter = HEAD_DIM // 4
    inv_freq = ROPE_THETA ** (-jnp.arange(quarter, dtype=F32) / quarter)
    t = jnp.arange(T)
    row = (t // GRID_W).astype(F32)
    col = (t % GRID_W).astype(F32)
    ang_r = row[:, None] * inv_freq
    ang_c = col[:, None] * inv_freq
    cos = jnp.concatenate([jnp.cos(ang_r)] * 2 + [jnp.cos(ang_c)] * 2, axis=1)
    sin = jnp.concatenate([-jnp.sin(ang_r), jnp.sin(ang_r), -jnp.sin(ang_c), jnp.sin(ang_c)], axis=1)
    return jnp.concatenate([cos, cos], axis=1), jnp.concatenate([sin, sin], axis=1)


def _layer_weights(p, l):
    Wd = RWKV_WIDTH
    w_in = p["w_in"][l]
    c0 = RWKV_COLS
    aq, ak, av, gq, gk, gv, gl, gr = (c0, c0 + 512, c0 + 640, c0 + 768, c0 + 1280, c0 + 1792, c0 + 2816, c0 + 2848)
    m0 = gr + GLA_V_WIDTH
    D = D_MODEL
    w_g = jnp.concatenate([w_in[:, gq:gl], w_in[:, gr:m0], w_in[:, gl:gr],
                           jnp.zeros((D, LANES - 2 * GLA_LORA), F32)], axis=1)
    z64 = jnp.zeros((DECAY_LORA, Wd), F32)
    w2 = jnp.concatenate([jnp.concatenate([p["rwkv_w2"][l, 0], z64], axis=1),
                          jnp.concatenate([z64, p["rwkv_w2"][l, 1]], axis=1)], axis=0)
    a2 = jnp.concatenate([jnp.concatenate([p["rwkv_a2"][l, 0], z64], axis=1),
                          jnp.concatenate([z64, p["rwkv_a2"][l, 1]], axis=1)], axis=0)
    z16 = jnp.zeros((GLA_LORA, GLA_K_WIDTH), F32)
    ga2 = jnp.concatenate([jnp.concatenate([p["gla_a2"][l, 0], z16], axis=1),
                           jnp.concatenate([z16, p["gla_a2"][l, 1]], axis=1),
                           jnp.zeros((LANES - 2 * GLA_LORA, 2 * GLA_K_WIDTH), F32)], axis=0)
    return {
        "w_r": w_in[:, :c0].astype(BF16), "w_a": w_in[:, aq:gq].astype(BF16),
        "w_g": w_g.astype(BF16), "w_m": w_in[:, m0:].astype(BF16),
        "norm_mix": p["norm_mix"][l].reshape(1, D), "norm_ffn": p["norm_ffn"][l].reshape(1, D),
        "mu": p["rwkv_mu"][l].reshape(1, c0),
        "kk": p["rwkv_kk"][l].reshape(1, Wd), "ka": p["rwkv_ka"][l].reshape(1, Wd),
        "rk": p["rwkv_rk"][l].reshape(1, Wd),
        "w0": p["rwkv_w0"][l].reshape(1, 2 * Wd), "a0": p["rwkv_a0"][l].reshape(1, 2 * Wd),
        "w2": w2, "a2": a2, "g2": p["rwkv_g2"][l],
        "ln_g": p["rwkv_ln_g"][l].reshape(1, Wd), "ln_b": p["rwkv_ln_b"][l].reshape(1, Wd),
        "seg64": _seg_matrix(Wd, RWKV_HEAD_DIM),
        "qn": jnp.tile(p["attn_qn"][l], N_HEADS).reshape(1, ATT_WIDTH),
        "kn": jnp.tile(p["attn_kn"][l], KV_HEADS).reshape(1, KV_WIDTH),
        "gla_a2": ga2, "gla_ab": p["gla_ab"][l].reshape(1, 2 * GLA_K_WIDTH),
        "gla_norm": p["gla_norm"][l].reshape(1, GLA_DV),
        "p_rwkv": p["p_rwkv"][l].astype(BF16), "p_attn": p["p_attn"][l].astype(BF16),
        "p_gla": p["p_gla"][l].astype(BF16), "w_out": p["w_out"][l].astype(BF16),
        "router_w": jnp.concatenate([p["router_w"][l], jnp.zeros((D, LANES - N_EXPERTS), F32)], axis=1),
        "router_b": jnp.concatenate([p["router_b"][l], jnp.full((LANES - N_EXPERTS,), -jnp.inf, F32)]).reshape(1, LANES),
        "layer": l, "moe_w1": p["moe_w1"], "moe_b1": p["moe_b1"][l].reshape(N_EXPERTS, 1, 2 * D_EXPERT),
        "moe_w2": p["moe_w2"], "moe_b2": p["moe_b2"][l].reshape(N_EXPERTS, 1, D_MODEL),
    }


def _trunk_layer(x, mod, lw, B, T, ctx):
    N = B * T
    rpc = N if mod.shape[0] == 1 else T
    zr, za, zg, zm = (_inproj_call(x, mod, lw["norm_mix"], lw[name], rpc, lw[name].shape[1])
                      for name in ("w_r", "w_a", "w_g", "w_m"))

    r, v, kk, g, bon, lw0, lw1, k0, k1, b0, b1 = _rwkv_prep_call(zr, T, lw)
    if ctx is None:
        s0r = jnp.zeros((B, 2, RWKV_HEADS, RWKV_HEAD_DIM, RWKV_HEAD_DIM), F32)
        s0g = jnp.zeros((B, 2, GLA_HEADS, GLA_DV, GLA_DK), F32)
        cache, tabs = None, None
    else:
        s0r = ctx["rwkv"]
        s0g = jnp.swapaxes(ctx["gla"], -1, -2)
        cache, tabs = (ctx["k"], ctx["v"], ctx["layer"]), _rope_tables(T)
    yf, yb, s_r = _rwkv_scan_call((r, v, kk, lw0, lw1, k0, k1, b0, b1), s0r, B, T)
    q, kn, kr, vext = _attn_prep_call(za, lw, T, tabs)
    oa = _attn_call(q, kr, vext, cache, B, T)
    gf, gb, s_g = _gla_scan_call(zg, s0g, lw["gla_a2"], lw["gla_ab"], B, T)
    x1, h2, ti, tg = _merge_call(yf, yb, bon, g, oa, gf, gb, zg, zm, x, mod, lw, rpc)
    y4 = _moe_ffn(h2, ti[:, :TOP_K], tg[:, :TOP_K], lw)
    x2 = _combine_call(x1, mod, y4, rpc)
    return x2, (kn, za[:, ATT_WIDTH + KV_WIDTH:], s_r, jnp.swapaxes(s_g, -1, -2))


def kernel(x_prompt, x_sample, c, cache_k, cache_v, state_rwkv, state_gla, c_ctx, ada_w, ada_b, norm_mix, norm_ffn, w_in, rwkv_mu, rwkv_w0, rwkv_w2, rwkv_a0, rwkv_a2, rwkv_g2, rwkv_kk, rwkv_ka, rwkv_rk, rwkv_ln_g, rwkv_ln_b, attn_qn, attn_kn, gla_a2, gla_ab, gla_norm, p_rwkv, p_attn, p_gla, w_out, router_w, router_b, moe_w1, moe_b1, moe_w2, moe_b2):
    p = {"w_in": w_in, "norm_mix": norm_mix, "norm_ffn": norm_ffn, "rwkv_mu": rwkv_mu, "rwkv_w0": rwkv_w0,
         "rwkv_w2": rwkv_w2, "rwkv_a0": rwkv_a0, "rwkv_a2": rwkv_a2, "rwkv_g2": rwkv_g2, "rwkv_kk": rwkv_kk,
         "rwkv_ka": rwkv_ka, "rwkv_rk": rwkv_rk, "rwkv_ln_g": rwkv_ln_g, "rwkv_ln_b": rwkv_ln_b,
         "attn_qn": attn_qn, "attn_kn": attn_kn, "gla_a2": gla_a2, "gla_ab": gla_ab, "gla_norm": gla_norm,
         "p_rwkv": p_rwkv, "p_attn": p_attn, "p_gla": p_gla, "w_out": w_out, "router_w": router_w,
         "router_b": router_b, "moe_w1": moe_w1, "moe_b1": moe_b1, "moe_w2": moe_w2, "moe_b2": moe_b2}
    Bp, Tp, D = x_prompt.shape
    Bs, Ts, _ = x_sample.shape
    L = ada_w.shape[0]
    P = cache_k.shape[2]

    cond8 = jnp.concatenate([c_ctx[None, :], c, jnp.zeros((8 - 1 - Bs, D), F32)], axis=0)
    mod = _ada_call(cond8, ada_w, ada_b).reshape(L, 8, 6, D)
    weights = [_layer_weights(p, l) for l in range(L)]
    ck = cache_k.reshape(Bs, L, P, KV_WIDTH)
    cv = cache_v.reshape(Bs, L, P, KV_WIDTH)

    yp = x_prompt.reshape(Bp * Tp, D)
    ks, vs, srs, sgs = [], [], [], []
    for l in range(L):
        yp, (kn, vh, s_r, s_g) = _trunk_layer(yp, mod[l, 0:1], weights[l], Bp, Tp, None)
        ks.append(kn.reshape(Bp, Tp, KV_HEADS, HEAD_DIM))
        vs.append(vh.reshape(Bp, Tp, KV_HEADS, HEAD_DIM))
        srs.append(s_r)
        sgs.append(s_g)

    ys = x_sample.reshape(Bs * Ts, D)
    for l in range(L):
        ctx = {"k": ck, "v": cv, "layer": l, "rwkv": state_rwkv[:, l], "gla": state_gla[:, l]}
        ys, _ = _trunk_layer(ys, mod[l, 1:1 + Bs], weights[l], Bs, Ts, ctx)

    return (yp.reshape(Bp, Tp, D), ys.reshape(Bs, Ts, D),
            jnp.stack(ks, axis=1), jnp.stack(vs, axis=1),
            jnp.stack(srs, axis=1), jnp.stack(sgs, axis=1))
```

```python
import functools

import jax
import jax.numpy as jnp
from jax import lax
from jax.experimental import pallas as pl
from jax.experimental.pallas import tpu as pltpu

D_MODEL = 1024
DEPTH = 2
GRID_W = 64
NORM_EPS = 1e-6

RWKV_HEADS = 8
RWKV_HEAD_DIM = 64
RWKV_WIDTH = RWKV_HEADS * RWKV_HEAD_DIM
DECAY_LORA = 64
ICLR_LORA = 64
GATE_LORA = 128
RWKV_GN_EPS = 6.4e-4
RWKV_COLS = 3 * RWKV_WIDTH + 2 * DECAY_LORA + 2 * ICLR_LORA + GATE_LORA

N_HEADS = 8
KV_HEADS = 2
HEAD_DIM = 64
ATT_WIDTH = N_HEADS * HEAD_DIM
KV_WIDTH = KV_HEADS * HEAD_DIM
ROPE_THETA = 10000.0

GLA_HEADS = 4
GLA_DK = 128
GLA_DV = 256
GLA_K_WIDTH = GLA_HEADS * GLA_DK
GLA_V_WIDTH = GLA_HEADS * GLA_DV
GLA_LORA = 16
GLA_GATE_NORMALIZER = 16.0

N_EXPERTS = 32
TOP_K = 4
D_EXPERT = 1024
SWIGLU_ALPHA = 1.702
SWIGLU_LIMIT = 7.0
EXPERT_BLOCK = 512

LANES = 128
CHUNK = 64
ROW_TILE = 256
INPROJ_TILE = 512
Q_TILE = 128
ATT_STACK = 2
KEY_CHUNK = 512
VMEM_LIMIT = 52 * 1024 * 1024
LOG2_E = 1.4426950408889634

F32 = jnp.float32
BF16 = jnp.bfloat16
HI = lax.Precision.HIGHEST


def _dot(a, b, prec=None):
    return jnp.dot(a, b, preferred_element_type=F32, precision=prec)


def _dot_nt(a, b, prec=None):
    return lax.dot_general(a, b, (((1,), (1,)), ((), ())), preferred_element_type=F32, precision=prec)


def _dot_tn(a, b, prec=None):
    return lax.dot_general(a, b, (((0,), (0,)), ((), ())), preferred_element_type=F32, precision=prec)


_NN = (((1,), (0,)), ((), ()))
_NT = (((1,), (1,)), ((), ()))
_TN = (((0,), (0,)), ((), ()))


def _dg(a, b, dn):
    return lax.dot_general(a, b, dn, preferred_element_type=F32)


def _split(x):
    h = x.astype(BF16)
    return h, (x - h.astype(F32)).astype(BF16)


def _mm3(a, b, dn=_NN):
    return _dg(a[0], b[0], dn) + _dg(a[0], b[1], dn) + _dg(a[1], b[0], dn)


def _mm3s(a, b, dn=_NN):
    m = a[0].shape[0]
    top = _dg(jnp.concatenate([a[0], a[1]], axis=0), b[0], dn)
    return top[:m] + top[m:] + _dg(a[0], b[1], dn)


def _cumsum_rows(mask, x):
    m = jnp.where(mask, 1.0, 0.0).astype(BF16)
    h = x.astype(BF16)
    r1 = x - h.astype(F32)
    mid = r1.astype(BF16)
    lo = (r1 - mid.astype(F32)).astype(BF16)
    return _dg(m, h, _NN) + _dg(m, mid, _NN) + _dg(m, lo, _NN)


def _seg_sum(x, seg_bf16):
    xh = x.astype(BF16)
    xl = (x - xh.astype(F32)).astype(BF16)
    return _dot(xh, seg_bf16) + _dot(xl, seg_bf16)


def _sigmoid(x):
    return 1.0 / (1.0 + jnp.exp(-x))


def _params(sem):
    return pltpu.CompilerParams(dimension_semantics=sem, vmem_limit_bytes=VMEM_LIMIT)


def _const_spec(shape):
    nd = len(shape)
    return pl.BlockSpec(shape, lambda *_: (0,) * nd)


def _ada_kernel(c_ref, w_ref, b_ref, o_ref):
    c = c_ref[...]
    s = c * _sigmoid(c)
    o_ref[...] = _mm3(_split(s), _split(w_ref[...])) + b_ref[...]


def _ada_call(cond8, ada_w, ada_b):
    L, D, W = ada_w.shape
    tn = 1536
    return pl.pallas_call(
        _ada_kernel,
        grid=(L, W // tn),
        in_specs=[pl.BlockSpec((8, D), lambda l, j: (0, 0)),
                  pl.BlockSpec((None, D, tn), lambda l, j: (l, 0, j)),
                  pl.BlockSpec((None, 1, tn), lambda l, j: (l, 0, j))],
        out_specs=pl.BlockSpec((None, 8, tn), lambda l, j: (l, 0, j)),
        out_shape=jax.ShapeDtypeStruct((L, 8, W), F32),
        compiler_params=_params(("parallel", "parallel")),
        name="ada",
    )(cond8, ada_w, ada_b.reshape(L, 1, W))


def _inproj_kernel(x_ref, m_ref, g_ref, w_ref, o_ref, h_scr):
    @pl.when(pl.program_id(1) == 0)
    def _():
        x = x_ref[...]
        y = x * lax.rsqrt(jnp.mean(x * x, axis=-1, keepdims=True) + NORM_EPS) * g_ref[...]
        h = y * (1.0 + m_ref[1:2, :]) + m_ref[0:1, :]
        h_scr[...] = h.astype(BF16)

    o_ref[...] = _dot(h_scr[...], w_ref[...])


def _inproj_call(x, mod, gain, w, rows_per_cond, tn):
    N, D = x.shape
    W = w.shape[1]
    tm = INPROJ_TILE
    return pl.pallas_call(
        _inproj_kernel,
        grid=(N // tm, W // tn),
        in_specs=[pl.BlockSpec((tm, D), lambda i, j: (i, 0)),
                  pl.BlockSpec((None, 6, D), lambda i, j: ((i * tm) // rows_per_cond, 0, 0)),
                  pl.BlockSpec((1, D), lambda i, j: (0, 0)),
                  pl.BlockSpec((D, tn), lambda i, j: (0, j))],
        out_specs=pl.BlockSpec((tm, tn), lambda i, j: (i, j)),
        out_shape=jax.ShapeDtypeStruct((N, W), F32),
        scratch_shapes=[pltpu.VMEM((tm, D), BF16)],
        compiler_params=_params(("parallel", "arbitrary")),
        name="inproj",
    )(x, mod, gain, w)


def _rwkv_prep_kernel(z_ref, zp_ref, zn_ref, mu_ref, kkp_ref, ka_ref, rk_ref, w0_ref, a0_ref,
                      w2_ref, a2_ref, g2_ref, seg_ref,
                      r_o, v_o, kk_o, g_o, bon_o, lw0_o, lw1_o, k0_o, k1_o, b0_o, b1_o,
                      *, tm, blocks_per_seq):
    i = pl.program_id(0)
    z = z_ref[...]
    row = lax.broadcasted_iota(jnp.int32, (tm, 1), 0)
    first = (i % blocks_per_seq) == 0
    last = (i % blocks_per_seq) == blocks_per_seq - 1
    pz = jnp.where(first, 0.0, zp_ref[7:8, :])
    nz = jnp.where(last, 0.0, zn_ref[0:1, :])
    prev = jnp.where(row == 0, pz, pltpu.roll(z, 1, 0))
    nxt = jnp.where(row == tm - 1, nz, pltpu.roll(z, tm - 1, 0))
    zm = z + mu_ref[...] * (0.5 * (prev + nxt) - z)

    Wd = RWKV_WIDTH
    r = zm[:, 0:Wd]
    k = zm[:, Wd:2 * Wd]
    v = zm[:, 2 * Wd:3 * Wd]
    lw = zm[:, 3 * Wd:3 * Wd + 128]
    la = zm[:, 3 * Wd + 128:3 * Wd + 256]
    lg = zm[:, 3 * Wd + 256:3 * Wd + 384]
    seg = seg_ref[...]

    kk = k * kkp_ref[...]
    kk = kk / jnp.maximum(jnp.sqrt(_seg_sum(kk * kk, seg)), 1e-12)
    w_raw = _mm3(_split(jnp.tanh(lw)), _split(w2_ref[...])) + w0_ref[...]
    logw = -_sigmoid(w_raw) * 0.6065306597126334
    a = _sigmoid(_mm3(_split(la), _split(a2_ref[...])) + a0_ref[...])
    ka = ka_ref[...]
    rk = rk_ref[...]
    bonus = jnp.zeros_like(r)
    for d, (lw_o, k_o, b_o) in enumerate(((lw0_o, k0_o, b0_o), (lw1_o, k1_o, b1_o))):
        a_d = a[:, d * Wd:(d + 1) * Wd]
        k_d = k * (1.0 + (a_d - 1.0) * ka)
        lw_o[...] = logw[:, d * Wd:(d + 1) * Wd]
        k_o[...] = k_d
        b_o[...] = kk * a_d
        bonus = bonus + _seg_sum(r * k_d * rk, seg) * v
    r_o[...] = r
    v_o[...] = v
    kk_o[...] = kk
    bon_o[...] = bonus
    g_o[...] = _mm3(_split(_sigmoid(lg)), _split(g2_ref[...]))


def _rwkv_prep_call(zr, seq_len, lw):
    N, Wz = zr.shape
    tm = ROW_TILE
    bps = seq_len // tm
    Wd = RWKV_WIDTH
    nb8 = N // 8
    row_spec = pl.BlockSpec((tm, Wd), lambda i: (i, 0))
    out = jax.ShapeDtypeStruct((N, Wd), F32)
    return pl.pallas_call(
        functools.partial(_rwkv_prep_kernel, tm=tm, blocks_per_seq=bps),
        grid=(N // tm,),
        in_specs=[pl.BlockSpec((tm, Wz), lambda i: (i, 0)),
                  pl.BlockSpec((8, Wz), lambda i: (jnp.maximum(i * (tm // 8) - 1, 0), 0)),
                  pl.BlockSpec((8, Wz), lambda i: (jnp.minimum((i + 1) * (tm // 8), nb8 - 1), 0)),
                  _const_spec((1, Wz)), _const_spec((1, Wd)), _const_spec((1, Wd)), _const_spec((1, Wd)),
                  _const_spec((1, 2 * Wd)), _const_spec((1, 2 * Wd)),
                  _const_spec((128, 2 * Wd)), _const_spec((128, 2 * Wd)), _const_spec((128, Wd)),
                  _const_spec((Wd, Wd))],
        out_specs=[row_spec] * 11,
        out_shape=[out] * 11,
        compiler_params=_params(("parallel",)),
        name="rwkv_prep",
    )(zr, zr, zr, lw["mu"], lw["kk"], lw["ka"], lw["rk"], lw["w0"], lw["a0"],
      lw["w2"], lw["a2"], lw["g2"], lw["seg64"])


def _tri_masks(d):
    ti = lax.broadcasted_iota(jnp.int32, (CHUNK, CHUNK), 0)
    si = lax.broadcasted_iota(jnp.int32, (CHUNK, CHUNK), 1)
    if d == 0:
        return si <= ti, si < ti
    return si >= ti, si > ti


def _rwkv_chunk(rf, vf, kkf, lwf, kf, bf, rb, vb, kkb, lwb, kb, bb, yf_o, yb_o, s_scr):
    C = CHUNK
    NP = RWKV_HEADS // 2
    sls = [slice(q * LANES, (q + 1) * LANES) for q in range(NP)]
    low = lax.broadcasted_iota(jnp.int32, (C, LANES), 1) < RWKV_HEAD_DIM

    def bd(x):
        zero = jnp.zeros_like(x)
        return jnp.concatenate([jnp.where(low, x, zero), jnp.where(low, zero, x)], axis=0)

    def bd2(x):
        return bd(x[0]), bd(x[1])

    eye = jnp.concatenate([(lax.broadcasted_iota(jnp.int32, (C, C), 0)
                            == lax.broadcasted_iota(jnp.int32, (C, C), 1)).astype(F32)] * 2, axis=1)
    dirs = ((rf, vf, kkf, lwf, kf, bf), (rb, vb, kkb, lwb, kb, bb))
    pre = []
    for d, (r_ref, v_ref, kk_ref, lw_ref, k_ref, b_ref) in enumerate(dirs):
        incl, strict = _tri_masks(d)
        lw = lw_ref[...]
        b = _cumsum_rows(incl, lw)
        btot = jnp.sum(lw, axis=0, keepdims=True)
        r, v, kk, k, beta = r_ref[...], v_ref[...], kk_ref[...], k_ref[...], b_ref[...]
        nb = jnp.exp(-b)
        eb = jnp.exp(btot - b)
        gtot = jnp.exp(btot)
        ar = _split(jnp.concatenate([-kk * jnp.exp(b - lw), r * jnp.exp(b)], axis=0))
        bbar = _split(beta * nb)
        kbar = (k * nb).astype(BF16)
        hat = jnp.concatenate([beta * eb, k * eb], axis=0).astype(BF16)
        v16 = v.astype(BF16)

        ti = lax.broadcasted_iota(jnp.int32, (C, LANES), 0)
        si = lax.broadcasted_iota(jnp.int32, (C, LANES), 1) % RWKV_HEAD_DIM
        incl2, strict2 = ((si <= ti, si < ti) if d == 0 else (si >= ti, si > ti))
        g1 = [_mm3s((ar[0][:, s], ar[1][:, s]), bd2((bbar[0][:, s], bbar[1][:, s])), _NT) for s in sls]
        g2 = [_dg(ar[0][:, s], bd(kbar[:, s]), _NT) for s in sls]
        a_ab = [jnp.where(strict2, g[:C], 0.0) for g in g1]
        a_rb = [jnp.where(incl2, g[C:], 0.0).astype(BF16) for g in g1]
        a_ak = [jnp.where(strict2, g[:C], 0.0).astype(BF16) for g in g2]
        a_rk = [jnp.where(incl2, g[C:], 0.0).astype(BF16) for g in g2]
        vbd = [bd(v16[:, s]) for s in sls]
        av = [_dg(a, vv, _NN) for a, vv in zip(a_ak, vbd)]
        t = [eye + a for a in a_ab]
        p = [_split(a) for a in a_ab]
        p = [_split(_mm3s(pp, bd2(pp))) for pp in p]
        for _ in range(4):
            ts = [_split(tt) for tt in t]
            res = [_mm3s((jnp.concatenate([pp[0], tt[0]], axis=0), jnp.concatenate([pp[1], tt[1]], axis=0)),
                         bd2(pp)) for pp, tt in zip(p, ts)]
            p = [_split(rr[:C]) for rr in res]
            t = [tt + rr[C:] for tt, rr in zip(t, res)]
        t = [tt + _mm3s(_split(tt), bd2(pp)) for tt, pp in zip(t, p)]
        pre.append((ar[0], av, [_split(tt) for tt in t], a_rb, a_rk, vbd, v16, hat, gtot))

    hd = [(d, q) for d in range(2) for q in range(NP)]
    s_old = [s_scr[d, q] for d, q in hd]
    z = [_dg(pre[d][0][:, sls[q]], bd(s.astype(BF16)), _NT) for (d, q), s in zip(hd, s_old)]
    u = [_mm3s(pre[d][2][q], bd2(_split(zz[:C] + pre[d][1][q]))).astype(BF16) for (d, q), zz in zip(hd, z)]
    ys = [zz[C:] + _dg(jnp.concatenate([pre[d][3][q], pre[d][4][q]], axis=1),
                       jnp.concatenate([bd(uu), pre[d][5][q]], axis=0), _NN)
          for (d, q), zz, uu in zip(hd, z, u)]
    cross = [_dg(jnp.concatenate([uu, pre[d][6][:, sls[q]]], axis=0), pre[d][7][:, sls[q]], _TN)
             for (d, q), uu in zip(hd, u)]
    stores = []
    for (d, q), yy, s, cc in zip(hd, ys, s_old, cross):
        stores.append(((yf_o, yb_o)[d], (slice(None), sls[q]), yy))
        stores.append((s_scr, (d, q), s * pre[d][8][:, sls[q]] + jnp.where(low, cc[:C], cc[C:])))
    return stores


def _gla_chunk(xf, glf, xb, glb, a2_ref, ab_ref, of_o, ob_o, s_scr):
    Kw = GLA_K_WIDTH
    work = []
    for d, (x_ref, gl_ref) in enumerate(((xf, glf), (xb, glb))):
        incl, _ = _tri_masks(d)
        xa = (_mm3(_split(gl_ref[...]), _split(a2_ref[:, d * Kw:(d + 1) * Kw]))
              + ab_ref[:, d * Kw:(d + 1) * Kw])
        log_a = (jnp.minimum(xa, 0.0) - jnp.log1p(jnp.exp(-jnp.abs(xa)))) * (1.0 / GLA_GATE_NORMALIZER)
        b = _cumsum_rows(incl, log_a)
        bl = jnp.sum(log_a, axis=0, keepdims=True)
        q = x_ref[:, 0:Kw] * (GLA_DK ** -0.5)
        k = x_ref[:, Kw:2 * Kw]
        q_in = (q * jnp.exp(b)).astype(BF16)
        k_in = (k * jnp.exp(-b)).astype(BF16)
        k_end = (k * jnp.exp(bl - b)).astype(BF16)
        gtot = jnp.exp(bl)
        for h in range(GLA_HEADS):
            ks = slice(h * GLA_DK, (h + 1) * GLA_DK)
            v_h = x_ref[:, 2 * Kw + h * GLA_DV:2 * Kw + (h + 1) * GLA_DV].astype(BF16)
            att = jnp.where(incl, _dg(q_in[:, ks], k_in[:, ks], _NT), 0.0).astype(BF16)
            work.append((d, h, _dg(att, v_h, _NN), q_in[:, ks], gtot[:, ks], _dg(v_h, k_end[:, ks], _TN)))

    s_old = [s_scr[d, h] for d, h, *_ in work]
    outs = [o_in + _dg(qh, s.astype(BF16), _NT) for (_, _, o_in, qh, _, _), s in zip(work, s_old)]
    stores = []
    for (d, h, _, _, g, kv), s, o in zip(work, s_old, outs):
        stores.append(((of_o, ob_o)[d], (slice(None), slice(h * GLA_DV, (h + 1) * GLA_DV)), o))
        stores.append((s_scr, (d, h), s * g + kv))
    return stores


def _scan_kernel(rf, vf, kkf, lwf, kf, bf, rb, vb, kkb, lwb, kb, bb, s0r_ref,
                 xf, glf, xb, glb, a2_ref, ab_ref, s0g_ref,
                 yf_o, yb_o, str_o, of_o, ob_o, stg_o, sr_scr, sg_scr, *, nc):
    i = pl.program_id(1)

    @pl.when(i == 0)
    def _():
        sr_scr[...] = s0r_ref[...]
        sg_scr[...] = s0g_ref[...]

    stores = (_rwkv_chunk(rf, vf, kkf, lwf, kf, bf, rb, vb, kkb, lwb, kb, bb, yf_o, yb_o, sr_scr)
              + _gla_chunk(xf, glf, xb, glb, a2_ref, ab_ref, of_o, ob_o, sg_scr))
    for ref, idx, val in stores:
        ref[idx] = val

    @pl.when(i == nc - 1)
    def _():
        str_o[...] = sr_scr[...]
        stg_o[...] = sg_scr[...]


def _scan_call(prep, s0, zg, s0t, a2, ab, B, T):
    r, v, kk, lw0, lw1, k0, k1, b0, b1 = prep
    N, Wd = r.shape
    nc = T // CHUNK
    H, Dh = RWKV_HEADS, RWKV_HEAD_DIM
    pair_shape = (2, H // 2, Dh, 2 * Dh)
    fwd = pl.BlockSpec((CHUNK, Wd), lambda b, i: (b * nc + i, 0))
    bwd = pl.BlockSpec((CHUNK, Wd), lambda b, i: (b * nc + nc - 1 - i, 0))
    st_r = pl.BlockSpec((None,) + pair_shape, lambda b, i: (b, 0, 0, 0, 0))
    y = jax.ShapeDtypeStruct((N, Wd), F32)
    s0p = s0.reshape(B, 2, H // 2, 2, Dh, Dh).transpose(0, 1, 2, 4, 3, 5).reshape((B,) + pair_shape)
    Wx = 2 * GLA_K_WIDTH + GLA_V_WIDTH
    gl_col = (Wx + GLA_V_WIDTH) // LANES
    xf = pl.BlockSpec((CHUNK, Wx), lambda b, i: (b * nc + i, 0))
    xb = pl.BlockSpec((CHUNK, Wx), lambda b, i: (b * nc + nc - 1 - i, 0))
    gf = pl.BlockSpec((CHUNK, LANES), lambda b, i: (b * nc + i, gl_col))
    gb = pl.BlockSpec((CHUNK, LANES), lambda b, i: (b * nc + nc - 1 - i, gl_col))
    of = pl.BlockSpec((CHUNK, GLA_V_WIDTH), lambda b, i: (b * nc + i, 0))
    ob = pl.BlockSpec((CHUNK, GLA_V_WIDTH), lambda b, i: (b * nc + nc - 1 - i, 0))
    gla_shape = (2, GLA_HEADS, GLA_DV, GLA_DK)
    st_g = pl.BlockSpec((None,) + gla_shape, lambda b, i: (b, 0, 0, 0, 0))
    o = jax.ShapeDtypeStruct((N, GLA_V_WIDTH), F32)
    yf, yb, stp, gf_out, gb_out, stg = pl.pallas_call(
        functools.partial(_scan_kernel, nc=nc),
        grid=(B, nc),
        in_specs=([fwd] * 6 + [bwd] * 6 + [st_r]
                  + [xf, gf, xb, gb, _const_spec((LANES, 2 * GLA_K_WIDTH)), _const_spec((1, 2 * GLA_K_WIDTH)), st_g]),
        out_specs=[fwd, bwd, st_r, of, ob, st_g],
        out_shape=[y, y, jax.ShapeDtypeStruct((B,) + pair_shape, F32),
                   o, o, jax.ShapeDtypeStruct((B,) + gla_shape, F32)],
        scratch_shapes=[pltpu.VMEM(pair_shape, F32), pltpu.VMEM(gla_shape, F32)],
        compiler_params=_params(("parallel", "arbitrary")),
        name="scans",
    )(r, v, kk, lw0, k0, b0, r, v, kk, lw1, k1, b1, s0p, zg, zg, zg, zg, a2, ab, s0t)
    st_out = stp.reshape(B, 2, H // 2, Dh, 2, Dh).transpose(0, 1, 2, 4, 3, 5).reshape(B, 2, H, Dh, Dh)
    return yf, yb, st_out, gf_out, gb_out, stg


def _rope(x, cos, sin):
    n = x.shape[1]
    lane = lax.broadcasted_iota(jnp.int32, x.shape, 1)
    up = pltpu.roll(x, n - 16, 1)
    dn = pltpu.roll(x, 16, 1)
    sw = jnp.where((lane % 32) < 16, up, dn)
    reps = n // LANES
    if reps > 1:
        cos = jnp.concatenate([cos] * reps, axis=1)
        sin = jnp.concatenate([sin] * reps, axis=1)
    return x * cos + sw * sin


def _attn_prep_kernel(*refs, rope):
    if rope:
        z_ref, qn_ref, kn_ref, segq_ref, segk_ref, cos_ref, sin_ref, q_o, kn_o, kr_o, ve_o = refs
    else:
        z_ref, qn_ref, kn_ref, segq_ref, segk_ref, q_o, kn_o, kr_o, ve_o = refs
    zq = z_ref[:, 0:ATT_WIDTH]
    zk = z_ref[:, ATT_WIDTH:ATT_WIDTH + KV_WIDTH]
    qh = zq * lax.rsqrt(_seg_sum(zq * zq, segq_ref[...]) * (1.0 / HEAD_DIM) + NORM_EPS) * qn_ref[...]
    kh = zk * lax.rsqrt(_seg_sum(zk * zk, segk_ref[...]) * (1.0 / HEAD_DIM) + NORM_EPS) * kn_ref[...]
    kn_o[...] = kh
    if rope:
        qh = _rope(qh, cos_ref[...], sin_ref[...])
        kh = _rope(kh, cos_ref[...], sin_ref[...])
    q_o[...] = (qh * (LOG2_E * HEAD_DIM ** -0.5)).astype(BF16)
    kr_o[...] = kh.astype(BF16)
    v = z_ref[:, ATT_WIDTH + KV_WIDTH:ATT_WIDTH + 2 * KV_WIDTH]
    group = lax.broadcasted_iota(jnp.int32, v.shape, 1) // HEAD_DIM
    for g in range(KV_HEADS):
        ve_o[:, g * LANES:(g + 1) * LANES] = jnp.where(group == g, v, 1.0).astype(BF16)


def _attn_prep_call(za, lw, seq_len, rope_tabs):
    N, Wz = za.shape
    tm = ROW_TILE
    bps = seq_len // tm
    rope = rope_tabs is not None
    in_specs = [pl.BlockSpec((tm, Wz), lambda i: (i, 0)),
                _const_spec((1, ATT_WIDTH)), _const_spec((1, KV_WIDTH)),
                _const_spec((ATT_WIDTH, ATT_WIDTH)), _const_spec((KV_WIDTH, KV_WIDTH))]
    args = [za, lw["qn"], lw["kn"], lw["seg64"], lw["seg64"][:KV_WIDTH, :KV_WIDTH]]
    if rope:
        in_specs += [pl.BlockSpec((tm, LANES), lambda i: (i % bps, 0))] * 2
        args += list(rope_tabs)
    return pl.pallas_call(
        functools.partial(_attn_prep_kernel, rope=rope),
        grid=(N // tm,),
        in_specs=in_specs,
        out_specs=[pl.BlockSpec((tm, ATT_WIDTH), lambda i: (i, 0)),
                   pl.BlockSpec((tm, KV_WIDTH), lambda i: (i, 0)),
                   pl.BlockSpec((tm, KV_WIDTH), lambda i: (i, 0)),
                   pl.BlockSpec((tm, KV_HEADS * LANES), lambda i: (i, 0))],
        out_shape=[jax.ShapeDtypeStruct((N, ATT_WIDTH), BF16),
                   jax.ShapeDtypeStruct((N, KV_WIDTH), F32),
                   jax.ShapeDtypeStruct((N, KV_WIDTH), BF16),
                   jax.ShapeDtypeStruct((N, KV_HEADS * LANES), BF16)],
        compiler_params=_params(("parallel",)),
        name="attn_prep",
    )(*args)


def _attn_kernel(*refs, cached):
    if cached:
        q_ref, k_ref, v_ref, ck_ref, cv_ref, o_ref = refs
    else:
        q_ref, k_ref, v_ref, o_ref = refs
    G = N_HEADS // KV_HEADS
    tq = q_ref.shape[0]
    for g in range(KV_HEADS):
        gs = slice(g * HEAD_DIM, (g + 1) * HEAD_DIM)
        other = slice((1 - g) * HEAD_DIM, (2 - g) * HEAD_DIM)
        kg = k_ref[:, gs]
        vg = v_ref[:, g * LANES:(g + 1) * LANES]
        if cached:
            cv = cv_ref[...]
            own = (lax.broadcasted_iota(jnp.int32, cv.shape, 1) // HEAD_DIM) == g
            ckg = ck_ref[:, gs].astype(BF16)
            cvg = jnp.where(own, cv, 1.0).astype(BF16)
        for h0 in range(g * G, (g + 1) * G, ATT_STACK):
            heads = range(h0, h0 + ATT_STACK)
            qs = jnp.concatenate([q_ref[:, h * HEAD_DIM:(h + 1) * HEAD_DIM] for h in heads], axis=0)
            chunks = [(kg[c:c + KEY_CHUNK], vg[c:c + KEY_CHUNK]) for c in range(0, kg.shape[0], KEY_CHUNK)]
            if cached:
                chunks.append((ckg, cvg))
            m = None
            for kc, _ in chunks:
                mc = jnp.max(_dg(qs, kc, _NT), axis=-1, keepdims=True)
                m = mc if m is None else jnp.maximum(m, mc)
            oe = None
            for kc, vc in chunks:
                part = _dg(jnp.exp2(_dg(qs, kc, _NT) - m).astype(BF16), vc, _NN)
                oe = part if oe is None else oe + part
            o = oe[:, gs] / oe[:, other]
            for j, h in enumerate(heads):
                o_ref[:, h * HEAD_DIM:(h + 1) * HEAD_DIM] = o[j * tq:(j + 1) * tq]


def _attn_call(q, k, vext, cache, B, T):
    N = q.shape[0]
    tq = Q_TILE
    nq = T // tq
    in_specs = [pl.BlockSpec((tq, ATT_WIDTH), lambda b, i: (b * nq + i, 0)),
                pl.BlockSpec((T, KV_WIDTH), lambda b, i: (b, 0)),
                pl.BlockSpec((T, KV_HEADS * LANES), lambda b, i: (b, 0))]
    args = [q, k, vext]
    if cache is not None:
        ck, cv, layer = cache
        P = ck.shape[2]
        cspec = pl.BlockSpec((None, None, P, KV_WIDTH), lambda b, i: (b, layer, 0, 0))
        in_specs += [cspec, cspec]
        args += [ck, cv]
    return pl.pallas_call(
        functools.partial(_attn_kernel, cached=cache is not None),
        grid=(B, nq),
        in_specs=in_specs,
        out_specs=pl.BlockSpec((tq, ATT_WIDTH), lambda b, i: (b * nq + i, 0)),
        out_shape=jax.ShapeDtypeStruct((N, ATT_WIDTH), F32),
        compiler_params=_params(("parallel", "arbitrary")),
        name="attention",
    )(*args)


def _merge_kernel(yf, yb, bon, g, oa, gf, gb, gr, mr, ma, mg, x_ref, m_ref,
                  lng, lnb, gn, nf, seg_ref, pr, pa, pg, wo, rw, rb,
                  x1_o, h2_o, ti_o, tg_o):
    seg = seg_ref[...]
    inv = 1.0 / RWKV_HEAD_DIM
    y = yf[...] + yb[...]
    mu = _seg_sum(y, seg) * inv
    yc = y - mu
    var = _seg_sum(yc * yc, seg) * inv
    o_r = (yc * lax.rsqrt(var + RWKV_GN_EPS) * lng[...] + lnb[...] + bon[...]) * g[...]

    gate_r = gr[...]
    silu_r = gate_r * _sigmoid(gate_r)
    gnv = gn[...]
    cols = []
    for h in range(GLA_HEADS):
        vs = slice(h * GLA_DV, (h + 1) * GLA_DV)
        o = gf[:, vs] + gb[:, vs]
        ms = jnp.mean(o * o, axis=-1, keepdims=True)
        cols.append(o * lax.rsqrt(ms + NORM_EPS) * gnv * silu_r[:, vs])
    o_g = jnp.concatenate(cols, axis=1)

    merged = (_sigmoid(mr[...]) * _dot(o_r.astype(BF16), pr[...])
              + _sigmoid(ma[...]) * _dot(oa[...].astype(BF16), pa[...])
              + _sigmoid(mg[...]) * _dot(o_g.astype(BF16), pg[...]))
    x1 = x_ref[...] + m_ref[2:3, :] * _dot(merged.astype(BF16), wo[...])
    x1_o[...] = x1
    hn = x1 * lax.rsqrt(jnp.mean(x1 * x1, axis=-1, keepdims=True) + NORM_EPS) * nf[...]
    h2 = hn * (1.0 + m_ref[4:5, :]) + m_ref[3:4, :]
    h2_o[...] = h2

    logits = _mm3(_split(h2), _split(rw[...])) + rb[...]
    lane = lax.broadcasted_iota(jnp.int32, logits.shape, 1)
    vals, idxs = [], []
    for _ in range(TOP_K):
        m = jnp.max(logits, axis=-1, keepdims=True)
        idx = jnp.min(jnp.where(logits == m, lane, LANES), axis=-1, keepdims=True)
        vals.append(m)
        idxs.append(idx)
        logits = jnp.where(lane == idx, -jnp.inf, logits)
    es = [jnp.exp(vv - vals[0]) for vv in vals]
    den = es[0] + es[1] + es[2] + es[3]
    ti = jnp.zeros(logits.shape, jnp.int32)
    tg = jnp.zeros(logits.shape, F32)
    for j in range(TOP_K):
        ti = jnp.where(lane == j, idxs[j], ti)
        tg = jnp.where(lane == j, es[j] / den, tg)
    ti_o[...] = ti
    tg_o[...] = tg


def _merge_call(yf, yb, bon, g, oa, gf, gb, zg, zm, x, mod, lw, rows_per_cond):
    N, D = x.shape
    tm = ROW_TILE
    Wd = RWKV_WIDTH

    def rows(w, col=0):
        return pl.BlockSpec((tm, w), lambda i: (i, col))

    in_specs = [rows(Wd)] * 4 + [rows(ATT_WIDTH), rows(GLA_V_WIDTH), rows(GLA_V_WIDTH),
                                 rows(GLA_V_WIDTH, 2), rows(D, 0), rows(D, 1), rows(D, 2), rows(D),
                                 pl.BlockSpec((None, 6, D), lambda i: ((i * tm) // rows_per_cond, 0, 0)),
                                 _const_spec((1, Wd)), _const_spec((1, Wd)), _const_spec((1, GLA_DV)),
                                 _const_spec((1, D)), _const_spec((Wd, Wd)),
                                 _const_spec((Wd, D)), _const_spec((ATT_WIDTH, D)), _const_spec((GLA_V_WIDTH, D)),
                                 _const_spec((D, D)), _const_spec((D, LANES)), _const_spec((1, LANES))]
    return pl.pallas_call(
        _merge_kernel,
        grid=(N // tm,),
        in_specs=in_specs,
        out_specs=[rows(D), rows(D), rows(LANES), rows(LANES)],
        out_shape=[jax.ShapeDtypeStruct((N, D), F32), jax.ShapeDtypeStruct((N, D), F32),
                   jax.ShapeDtypeStruct((N, LANES), jnp.int32), jax.ShapeDtypeStruct((N, LANES), F32)],
        compiler_params=_params(("parallel",)),
        name="merge",
    )(yf, yb, bon, g, oa, gf, gb, zg, zm, zm, zm, x, mod,
      lw["ln_g"], lw["ln_b"], lw["gla_norm"], lw["norm_ffn"], lw["seg64"],
      lw["p_rwkv"], lw["p_attn"], lw["p_gla"], lw["w_out"], lw["router_w"], lw["router_b"])


def _moe_kernel(be_ref, na_ref, x_ref, gate_ref, w1_ref, b1_ref, w2_ref, b2_ref, o_ref, w1_scr, w2_scr):
    i = pl.program_id(0)

    @pl.when(jnp.logical_or(i == 0, be_ref[i] != be_ref[jnp.maximum(i - 1, 0)]))
    def _():
        w1_scr[...] = w1_ref[...].astype(BF16)
        w2_scr[...] = w2_ref[...].astype(BF16)

    @pl.when(i < na_ref[0])
    def _():
        z = _dot(x_ref[...].astype(BF16), w1_scr[...]) + b1_ref[...]
        glu = jnp.minimum(z[:, :D_EXPERT], SWIGLU_LIMIT)
        lin = jnp.clip(z[:, D_EXPERT:], -SWIGLU_LIMIT, SWIGLU_LIMIT)
        act = glu * _sigmoid(SWIGLU_ALPHA * glu) * (lin + 1.0)
        o_ref[...] = (_dot(act.astype(BF16), w2_scr[...]) + b2_ref[...]) * gate_ref[...]

    @pl.when(i >= na_ref[0])
    def _():
        o_ref[...] = jnp.zeros_like(o_ref)


def _moe_call(xg, row_gate, block_exp, n_active, layer, w1, b1, w2, b2):
    R, D = xg.shape
    tb = EXPERT_BLOCK
    F2 = w1.shape[3]
    grid_spec = pltpu.PrefetchScalarGridSpec(
        num_scalar_prefetch=2,
        grid=(R // tb,),
        in_specs=[pl.BlockSpec((tb, D), lambda i, be, na: (i, 0)),
                  pl.BlockSpec((tb, 1), lambda i, be, na: (i, 0)),
                  pl.BlockSpec((None, None, D, F2), lambda i, be, na: (layer, be[i], 0, 0)),
                  pl.BlockSpec((None, 1, F2), lambda i, be, na: (be[i], 0, 0)),
                  pl.BlockSpec((None, None, F2 // 2, D), lambda i, be, na: (layer, be[i], 0, 0)),
                  pl.BlockSpec((None, 1, D), lambda i, be, na: (be[i], 0, 0))],
        out_specs=pl.BlockSpec((tb, D), lambda i, be, na: (i, 0)),
        scratch_shapes=[pltpu.VMEM((D, F2), BF16), pltpu.VMEM((F2 // 2, D), BF16)],
    )
    return pl.pallas_call(
        _moe_kernel,
        grid_spec=grid_spec,
        out_shape=jax.ShapeDtypeStruct((R, D), F32),
        compiler_params=_params(("arbitrary",)),
        name="moe_experts",
    )(block_exp, n_active, xg, row_gate, w1, b1, w2, b2)


def _combine_kernel(x1_ref, m_ref, y0_ref, y1_ref, y2_ref, y3_ref, o_ref):
    y = (y0_ref[...] + y1_ref[...]) + (y2_ref[...] + y3_ref[...])
    o_ref[...] = x1_ref[...] + m_ref[5:6, :] * y


def _combine_call(x1, mod, ys, rows_per_cond):
    N, D = x1.shape
    tm = ROW_TILE
    rows = pl.BlockSpec((tm, D), lambda i: (i, 0))
    return pl.pallas_call(
        _combine_kernel,
        grid=(N // tm,),
        in_specs=([rows, pl.BlockSpec((None, 6, D), lambda i: ((i * tm) // rows_per_cond, 0, 0))]
                  + [pl.BlockSpec((None, tm, D), functools.partial(lambda i, j: (j, i, 0), j=j))
                     for j in range(TOP_K)]),
        out_specs=rows,
        out_shape=jax.ShapeDtypeStruct((N, D), F32),
        compiler_params=_params(("parallel",)),
        name="combine",
    )(x1, mod, *([ys] * TOP_K))


def _moe_ffn(h2, top_i, top_g, lw):
    N, D = h2.shape
    n_as = N * TOP_K
    n_blocks = -(-n_as // EXPERT_BLOCK) + N_EXPERTS
    R = n_blocks * EXPERT_BLOCK
    ex = jnp.arange(N_EXPERTS, dtype=jnp.int32)

    def lut(onehot, table):
        return jnp.sum(jnp.where(onehot, table[None, :], 0), axis=1)

    flat_e = top_i.reshape(n_as)
    flat_g = top_g.reshape(n_as)
    oh_a = flat_e[:, None] == ex[None, :]
    counts = jnp.sum(oh_a, axis=0, dtype=jnp.int32)
    starts = jnp.cumsum(counts) - counts
    padded = (counts + EXPERT_BLOCK - 1) // EXPERT_BLOCK * EXPERT_BLOCK
    pends = jnp.cumsum(padded)
    pstarts = pends - padded
    order = jnp.argsort(flat_e).astype(jnp.int32)
    rank = jnp.argsort(order).astype(jnp.int32)
    pos = rank + lut(oh_a, pstarts - starts)
    blk_start = jnp.arange(n_blocks, dtype=jnp.int32) * EXPERT_BLOCK
    block_exp = jnp.minimum(jnp.sum(pends[None, :] <= blk_start[:, None], axis=1),
                            N_EXPERTS - 1).astype(jnp.int32)
    n_active = (pends[-1] // EXPERT_BLOCK).astype(jnp.int32).reshape(1)
    oh_p = jnp.repeat(block_exp, EXPERT_BLOCK)[:, None] == ex[None, :]
    idx = jnp.arange(R, dtype=jnp.int32) - lut(oh_p, pstarts)
    valid = (idx >= 0) & (idx < lut(oh_p, counts))
    asg = order[jnp.clip(lut(oh_p, starts) + idx, 0, n_as - 1)]
    row_tok = jnp.where(valid, asg // TOP_K, 0)
    row_gate = jnp.where(valid, flat_g[asg], 0.0)
    out = _moe_call(h2[row_tok], row_gate.reshape(R, 1), block_exp, n_active,
                    lw["layer"], lw["moe_w1"], lw["moe_b1"], lw["moe_w2"], lw["moe_b2"])
    return out[pos.reshape(N, TOP_K).T]


def _seg_matrix(n, seg):
    idx = jnp.arange(n) // seg
    return (idx[:, None] == idx[None, :]).astype(BF16)


def _rope_tables(T):
    quarter = HEAD_DIM // 4
    inv_freq = ROPE_THETA ** (-jnp.arange(quarter, dtype=F32) / quarter)
    t = jnp.arange(T)
    row = (t // GRID_W).astype(F32)
    col = (t % GRID_W).astype(F32)
    ang_r = row[:, None] * inv_freq
    ang_c = col[:, None] * inv_freq
    cos = jnp.concatenate([jnp.cos(ang_r)] * 2 + [jnp.cos(ang_c)] * 2, axis=1)
    sin = jnp.concatenate([-jnp.sin(ang_r), jnp.sin(ang_r), -jnp.sin(ang_c), jnp.sin(ang_c)], axis=1)
    return jnp.concatenate([cos, cos], axis=1), jnp.concatenate([sin, sin], axis=1)


def _layer_weights(p, l):
    Wd = RWKV_WIDTH
    w_in = p["w_in"][l]
    c0 = RWKV_COLS
    aq, ak, av, gq, gk, gv, gl, gr = (c0, c0 + 512, c0 + 640, c0 + 768, c0 + 1280, c0 + 1792, c0 + 2816, c0 + 2848)
    m0 = gr + GLA_V_WIDTH
    D = D_MODEL
    w_g = jnp.concatenate([w_in[:, gq:gl], w_in[:, gr:m0], w_in[:, gl:gr],
                           jnp.zeros((D, LANES - 2 * GLA_LORA), F32)], axis=1)
    z64 = jnp.zeros((DECAY_LORA, Wd), F32)
    w2 = jnp.concatenate([jnp.concatenate([p["rwkv_w2"][l, 0], z64], axis=1),
                          jnp.concatenate([z64, p["rwkv_w2"][l, 1]], axis=1)], axis=0)
    a2 = jnp.concatenate([jnp.concatenate([p["rwkv_a2"][l, 0], z64], axis=1),
                          jnp.concatenate([z64, p["rwkv_a2"][l, 1]], axis=1)], axis=0)
    z16 = jnp.zeros((GLA_LORA, GLA_K_WIDTH), F32)
    ga2 = jnp.concatenate([jnp.concatenate([p["gla_a2"][l, 0], z16], axis=1),
                           jnp.concatenate([z16, p["gla_a2"][l, 1]], axis=1),
                           jnp.zeros((LANES - 2 * GLA_LORA, 2 * GLA_K_WIDTH), F32)], axis=0)
    return {
        "w_r": w_in[:, :c0].astype(BF16), "w_a": w_in[:, aq:gq].astype(BF16),
        "w_g": w_g.astype(BF16), "w_m": w_in[:, m0:].astype(BF16),
        "norm_mix": p["norm_mix"][l].reshape(1, D), "norm_ffn": p["norm_ffn"][l].reshape(1, D),
        "mu": p["rwkv_mu"][l].reshape(1, c0),
        "kk": p["rwkv_kk"][l].reshape(1, Wd), "ka": p["rwkv_ka"][l].reshape(1, Wd),
        "rk": p["rwkv_rk"][l].reshape(1, Wd),
        "w0": p["rwkv_w0"][l].reshape(1, 2 * Wd), "a0": p["rwkv_a0"][l].reshape(1, 2 * Wd),
        "w2": w2, "a2": a2, "g2": p["rwkv_g2"][l],
        "ln_g": p["rwkv_ln_g"][l].reshape(1, Wd), "ln_b": p["rwkv_ln_b"][l].reshape(1, Wd),
        "seg64": _seg_matrix(Wd, RWKV_HEAD_DIM),
        "qn": jnp.tile(p["attn_qn"][l], N_HEADS).reshape(1, ATT_WIDTH),
        "kn": jnp.tile(p["attn_kn"][l], KV_HEADS).reshape(1, KV_WIDTH),
        "gla_a2": ga2, "gla_ab": p["gla_ab"][l].reshape(1, 2 * GLA_K_WIDTH),
        "gla_norm": p["gla_norm"][l].reshape(1, GLA_DV),
        "p_rwkv": p["p_rwkv"][l].astype(BF16), "p_attn": p["p_attn"][l].astype(BF16),
        "p_gla": p["p_gla"][l].astype(BF16), "w_out": p["w_out"][l].astype(BF16),
        "router_w": jnp.concatenate([p["router_w"][l], jnp.zeros((D, LANES - N_EXPERTS), F32)], axis=1),
        "router_b": jnp.concatenate([p["router_b"][l], jnp.full((LANES - N_EXPERTS,), -jnp.inf, F32)]).reshape(1, LANES),
        "layer": l, "moe_w1": p["moe_w1"], "moe_b1": p["moe_b1"][l].reshape(N_EXPERTS, 1, 2 * D_EXPERT),
        "moe_w2": p["moe_w2"], "moe_b2": p["moe_b2"][l].reshape(N_EXPERTS, 1, D_MODEL),
    }


def _trunk_layer(x, mod, lw, B, T, ctx):
    N = B * T
    rpc = N if mod.shape[0] == 1 else T
    zr, za, zg, zm = (_inproj_call(x, mod, lw["norm_mix"], lw[name], rpc, lw[name].shape[1])
                      for name in ("w_r", "w_a", "w_g", "w_m"))

    r, v, kk, g, bon, lw0, lw1, k0, k1, b0, b1 = _rwkv_prep_call(zr, T, lw)
    if ctx is None:
        s0r = jnp.zeros((B, 2, RWKV_HEADS, RWKV_HEAD_DIM, RWKV_HEAD_DIM), F32)
        s0g = jnp.zeros((B, 2, GLA_HEADS, GLA_DV, GLA_DK), F32)
        cache, tabs = None, None
    else:
        s0r = ctx["rwkv"]
        s0g = jnp.swapaxes(ctx["gla"], -1, -2)
        cache, tabs = (ctx["k"], ctx["v"], ctx["layer"]), _rope_tables(T)
    yf, yb, s_r, gf, gb, s_g = _scan_call((r, v, kk, lw0, lw1, k0, k1, b0, b1), s0r,
                                          zg, s0g, lw["gla_a2"], lw["gla_ab"], B, T)
    q, kn, kr, vext = _attn_prep_call(za, lw, T, tabs)
    oa = _attn_call(q, kr, vext, cache, B, T)
    x1, h2, ti, tg = _merge_call(yf, yb, bon, g, oa, gf, gb, zg, zm, x, mod, lw, rpc)
    y4 = _moe_ffn(h2, ti[:, :TOP_K], tg[:, :TOP_K], lw)
    x2 = _combine_call(x1, mod, y4, rpc)
    return x2, (kn, za[:, ATT_WIDTH + KV_WIDTH:], s_r, jnp.swapaxes(s_g, -1, -2))


def kernel(x_prompt, x_sample, c, cache_k, cache_v, state_rwkv, state_gla, c_ctx, ada_w, ada_b, norm_mix, norm_ffn, w_in, rwkv_mu, rwkv_w0, rwkv_w2, rwkv_a0, rwkv_a2, rwkv_g2, rwkv_kk, rwkv_ka, rwkv_rk, rwkv_ln_g, rwkv_ln_b, attn_qn, attn_kn, gla_a2, gla_ab, gla_norm, p_rwkv, p_attn, p_gla, w_out, router_w, router_b, moe_w1, moe_b1, moe_w2, moe_b2):
    p = {"w_in": w_in, "norm_mix": norm_mix, "norm_ffn": norm_ffn, "rwkv_mu": rwkv_mu, "rwkv_w0": rwkv_w0,
         "rwkv_w2": rwkv_w2, "rwkv_a0": rwkv_a0, "rwkv_a2": rwkv_a2, "rwkv_g2": rwkv_g2, "rwkv_kk": rwkv_kk,
         "rwkv_ka": rwkv_ka, "rwkv_rk": rwkv_rk, "rwkv_ln_g": rwkv_ln_g, "rwkv_ln_b": rwkv_ln_b,
         "attn_qn": attn_qn, "attn_kn": attn_kn, "gla_a2": gla_a2, "gla_ab": gla_ab, "gla_norm": gla_norm,
         "p_rwkv": p_rwkv, "p_attn": p_attn, "p_gla": p_gla, "w_out": w_out, "router_w": router_w,
         "router_b": router_b, "moe_w1": moe_w1, "moe_b1": moe_b1, "moe_w2": moe_w2, "moe_b2": moe_b2}
    Bp, Tp, D = x_prompt.shape
    Bs, Ts, _ = x_sample.shape
    L = ada_w.shape[0]
    P = cache_k.shape[2]

    cond8 = jnp.concatenate([c_ctx[None, :], c, jnp.zeros((8 - 1 - Bs, D), F32)], axis=0)
    mod = _ada_call(cond8, ada_w, ada_b).reshape(L, 8, 6, D)
    weights = [_layer_weights(p, l) for l in range(L)]
    ck = cache_k.reshape(Bs, L, P, KV_WIDTH)
    cv = cache_v.reshape(Bs, L, P, KV_WIDTH)

    yp = x_prompt.reshape(Bp * Tp, D)
    ks, vs, srs, sgs = [], [], [], []
    for l in range(L):
        yp, (kn, vh, s_r, s_g) = _trunk_layer(yp, mod[l, 0:1], weights[l], Bp, Tp, None)
        ks.append(kn.reshape(Bp, Tp, KV_HEADS, HEAD_DIM))
        vs.append(vh.reshape(Bp, Tp, KV_HEADS, HEAD_DIM))
        srs.append(s_r)
        sgs.append(s_g)

    ys = x_sample.reshape(Bs * Ts, D)
    for l in range(L):
        ctx = {"k": ck, "v": cv, "layer": l, "rwkv": state_rwkv[:, l], "gla": state_gla[:, l]}
        ys, _ = _trunk_layer(ys, mod[l, 1:1 + Bs], weights[l], Bs, Ts, ctx)

    return (yp.reshape(Bp, Tp, D), ys.reshape(Bs, Ts, D),
            jnp.stack(ks, axis=1), jnp.stack(vs, axis=1),
            jnp.stack(srs, axis=1), jnp.stack(sgs, axis=1))
```

```python
import functools

import jax
import jax.numpy as jnp
from jax import lax
from jax.experimental import pallas as pl
from jax.experimental.pallas import tpu as pltpu

D_MODEL = 1024
DEPTH = 2
GRID_W = 64
NORM_EPS = 1e-6

RWKV_HEADS = 8
RWKV_HEAD_DIM = 64
RWKV_WIDTH = RWKV_HEADS * RWKV_HEAD_DIM
DECAY_LORA = 64
ICLR_LORA = 64
GATE_LORA = 128
RWKV_GN_EPS = 6.4e-4
RWKV_COLS = 3 * RWKV_WIDTH + 2 * DECAY_LORA + 2 * ICLR_LORA + GATE_LORA
SCAN_PACK = 6 * RWKV_WIDTH

N_HEADS = 8
KV_HEADS = 2
HEAD_DIM = 64
ATT_WIDTH = N_HEADS * HEAD_DIM
KV_WIDTH = KV_HEADS * HEAD_DIM
ROPE_THETA = 10000.0

GLA_HEADS = 4
GLA_DK = 128
GLA_DV = 256
GLA_K_WIDTH = GLA_HEADS * GLA_DK
GLA_V_WIDTH = GLA_HEADS * GLA_DV
GLA_LORA = 16
GLA_GATE_NORMALIZER = 16.0

N_EXPERTS = 32
TOP_K = 4
D_EXPERT = 1024
SWIGLU_ALPHA = 1.702
SWIGLU_LIMIT = 7.0
EXPERT_BLOCK = 512

LANES = 128
CHUNK = 64
ROW_TILE = 256
INPROJ_TILE = 512
Q_TILE = 128
ATT_STACK = 2
KEY_CHUNK = 512
VMEM_LIMIT = 52 * 1024 * 1024
LOG2_E = 1.4426950408889634

F32 = jnp.float32
BF16 = jnp.bfloat16
HI = lax.Precision.HIGHEST


def _dot(a, b, prec=None):
    return jnp.dot(a, b, preferred_element_type=F32, precision=prec)


def _dot_nt(a, b, prec=None):
    return lax.dot_general(a, b, (((1,), (1,)), ((), ())), preferred_element_type=F32, precision=prec)


def _dot_tn(a, b, prec=None):
    return lax.dot_general(a, b, (((0,), (0,)), ((), ())), preferred_element_type=F32, precision=prec)


_NN = (((1,), (0,)), ((), ()))
_NT = (((1,), (1,)), ((), ()))
_TN = (((0,), (0,)), ((), ()))


def _dg(a, b, dn):
    return lax.dot_general(a, b, dn, preferred_element_type=F32)


def _split(x):
    h = x.astype(BF16)
    return h, (x - h.astype(F32)).astype(BF16)


def _mm3(a, b, dn=_NN):
    return _dg(a[0], b[0], dn) + _dg(a[0], b[1], dn) + _dg(a[1], b[0], dn)


def _mm3s(a, b, dn=_NN):
    m = a[0].shape[0]
    top = _dg(jnp.concatenate([a[0], a[1]], axis=0), b[0], dn)
    return top[:m] + top[m:] + _dg(a[0], b[1], dn)


def _cumsum_rows(mask, x):
    m = jnp.where(mask, 1.0, 0.0).astype(BF16)
    h = x.astype(BF16)
    r1 = x - h.astype(F32)
    mid = r1.astype(BF16)
    lo = (r1 - mid.astype(F32)).astype(BF16)
    return _dg(m, h, _NN) + _dg(m, mid, _NN) + _dg(m, lo, _NN)


def _seg_sum(x, seg_bf16):
    xh = x.astype(BF16)
    xl = (x - xh.astype(F32)).astype(BF16)
    return _dot(xh, seg_bf16) + _dot(xl, seg_bf16)


def _sigmoid(x):
    return 1.0 / (1.0 + jnp.exp(-x))


def _params(sem):
    return pltpu.CompilerParams(dimension_semantics=sem, vmem_limit_bytes=VMEM_LIMIT)


def _const_spec(shape):
    nd = len(shape)
    return pl.BlockSpec(shape, lambda *_: (0,) * nd)


def _ada_kernel(c_ref, w_ref, b_ref, o_ref):
    c = c_ref[...]
    s = c * _sigmoid(c)
    o_ref[...] = _mm3(_split(s), _split(w_ref[...])) + b_ref[...]


def _ada_call(cond8, ada_w, ada_b):
    L, D, W = ada_w.shape
    tn = 1536
    return pl.pallas_call(
        _ada_kernel,
        grid=(L, W // tn),
        in_specs=[pl.BlockSpec((8, D), lambda l, j: (0, 0)),
                  pl.BlockSpec((None, D, tn), lambda l, j: (l, 0, j)),
                  pl.BlockSpec((None, 1, tn), lambda l, j: (l, 0, j))],
        out_specs=pl.BlockSpec((None, 8, tn), lambda l, j: (l, 0, j)),
        out_shape=jax.ShapeDtypeStruct((L, 8, W), F32),
        compiler_params=_params(("parallel", "parallel")),
        name="ada",
    )(cond8, ada_w, ada_b.reshape(L, 1, W))


def _inproj_kernel(x_ref, m_ref, g_ref, w_ref, o_ref, h_scr):
    @pl.when(pl.program_id(1) == 0)
    def _():
        x = x_ref[...]
        y = x * lax.rsqrt(jnp.mean(x * x, axis=-1, keepdims=True) + NORM_EPS) * g_ref[...]
        h = y * (1.0 + m_ref[1:2, :]) + m_ref[0:1, :]
        h_scr[...] = h.astype(BF16)

    o_ref[...] = _dot(h_scr[...], w_ref[...])


def _inproj_call(x, mod, gain, w, rows_per_cond, tn):
    N, D = x.shape
    W = w.shape[1]
    tm = INPROJ_TILE
    return pl.pallas_call(
        _inproj_kernel,
        grid=(N // tm, W // tn),
        in_specs=[pl.BlockSpec((tm, D), lambda i, j: (i, 0)),
                  pl.BlockSpec((None, 6, D), lambda i, j: ((i * tm) // rows_per_cond, 0, 0)),
                  pl.BlockSpec((1, D), lambda i, j: (0, 0)),
                  pl.BlockSpec((D, tn), lambda i, j: (0, j))],
        out_specs=pl.BlockSpec((tm, tn), lambda i, j: (i, j)),
        out_shape=jax.ShapeDtypeStruct((N, W), F32),
        scratch_shapes=[pltpu.VMEM((tm, D), BF16)],
        compiler_params=_params(("parallel", "arbitrary")),
        name="inproj",
    )(x, mod, gain, w)


def _rwkv_prep_kernel(z_ref, zp_ref, zn_ref, mu_ref, kkp_ref, ka_ref, rk_ref, w0_ref, a0_ref,
                      w2_ref, a2_ref, g2_ref, seg_ref,
                      fw_o, bw_o, g_o, bon_o, *, tm, blocks_per_seq):
    i = pl.program_id(0)
    z = z_ref[...]
    row = lax.broadcasted_iota(jnp.int32, (tm, 1), 0)
    first = (i % blocks_per_seq) == 0
    last = (i % blocks_per_seq) == blocks_per_seq - 1
    pz = jnp.where(first, 0.0, zp_ref[7:8, :])
    nz = jnp.where(last, 0.0, zn_ref[0:1, :])
    prev = jnp.where(row == 0, pz, pltpu.roll(z, 1, 0))
    nxt = jnp.where(row == tm - 1, nz, pltpu.roll(z, tm - 1, 0))
    zm = z + mu_ref[...] * (0.5 * (prev + nxt) - z)

    Wd = RWKV_WIDTH
    r = zm[:, 0:Wd]
    k = zm[:, Wd:2 * Wd]
    v = zm[:, 2 * Wd:3 * Wd]
    lw = zm[:, 3 * Wd:3 * Wd + 128]
    la = zm[:, 3 * Wd + 128:3 * Wd + 256]
    lg = zm[:, 3 * Wd + 256:3 * Wd + 384]
    seg = seg_ref[...]

    kk = k * kkp_ref[...]
    kk = kk / jnp.maximum(jnp.sqrt(_seg_sum(kk * kk, seg)), 1e-12)
    w_raw = _mm3(_split(jnp.tanh(lw)), _split(w2_ref[...])) + w0_ref[...]
    logw = -_sigmoid(w_raw) * 0.6065306597126334
    a = _sigmoid(_mm3(_split(la), _split(a2_ref[...])) + a0_ref[...])
    ka = ka_ref[...]
    rk = rk_ref[...]
    bonus = jnp.zeros_like(r)
    for d, dir_o in enumerate((fw_o, bw_o)):
        a_d = a[:, d * Wd:(d + 1) * Wd]
        k_d = k * (1.0 + (a_d - 1.0) * ka)
        dir_o[:, 0:Wd] = r
        dir_o[:, Wd:2 * Wd] = v
        dir_o[:, 2 * Wd:3 * Wd] = kk
        dir_o[:, 3 * Wd:4 * Wd] = logw[:, d * Wd:(d + 1) * Wd]
        dir_o[:, 4 * Wd:5 * Wd] = k_d
        dir_o[:, 5 * Wd:6 * Wd] = kk * a_d
        bonus = bonus + _seg_sum(r * k_d * rk, seg) * v
    bon_o[...] = bonus
    g_o[...] = _mm3(_split(_sigmoid(lg)), _split(g2_ref[...]))


def _rwkv_prep_call(zr, seq_len, lw):
    N, Wz = zr.shape
    tm = ROW_TILE
    bps = seq_len // tm
    Wd = RWKV_WIDTH
    nb8 = N // 8
    row_spec = pl.BlockSpec((tm, Wd), lambda i: (i, 0))
    out = jax.ShapeDtypeStruct((N, Wd), F32)
    return pl.pallas_call(
        functools.partial(_rwkv_prep_kernel, tm=tm, blocks_per_seq=bps),
        grid=(N // tm,),
        in_specs=[pl.BlockSpec((tm, Wz), lambda i: (i, 0)),
                  pl.BlockSpec((8, Wz), lambda i: (jnp.maximum(i * (tm // 8) - 1, 0), 0)),
                  pl.BlockSpec((8, Wz), lambda i: (jnp.minimum((i + 1) * (tm // 8), nb8 - 1), 0)),
                  _const_spec((1, Wz)), _const_spec((1, Wd)), _const_spec((1, Wd)), _const_spec((1, Wd)),
                  _const_spec((1, 2 * Wd)), _const_spec((1, 2 * Wd)),
                  _const_spec((128, 2 * Wd)), _const_spec((128, 2 * Wd)), _const_spec((128, Wd)),
                  _const_spec((Wd, Wd))],
        out_specs=[pl.BlockSpec((tm, SCAN_PACK), lambda i: (i, 0))] * 2 + [row_spec] * 2,
        out_shape=[jax.ShapeDtypeStruct((N, SCAN_PACK), F32)] * 2 + [out] * 2,
        compiler_params=_params(("parallel",)),
        name="rwkv_prep",
    )(zr, zr, zr, lw["mu"], lw["kk"], lw["ka"], lw["rk"], lw["w0"], lw["a0"],
      lw["w2"], lw["a2"], lw["g2"], lw["seg64"])


def _tri_masks(d):
    ti = lax.broadcasted_iota(jnp.int32, (CHUNK, CHUNK), 0)
    si = lax.broadcasted_iota(jnp.int32, (CHUNK, CHUNK), 1)
    if d == 0:
        return si <= ti, si < ti
    return si >= ti, si > ti


def _rwkv_chunk(fw_ref, bw_ref, yf_o, yb_o, s_scr):
    C = CHUNK
    NP = RWKV_HEADS // 2
    sls = [slice(q * LANES, (q + 1) * LANES) for q in range(NP)]
    low = lax.broadcasted_iota(jnp.int32, (C, LANES), 1) < RWKV_HEAD_DIM

    def bd(x):
        zero = jnp.zeros_like(x)
        return jnp.concatenate([jnp.where(low, x, zero), jnp.where(low, zero, x)], axis=0)

    def bd2(x):
        return bd(x[0]), bd(x[1])

    eye = jnp.concatenate([(lax.broadcasted_iota(jnp.int32, (C, C), 0)
                            == lax.broadcasted_iota(jnp.int32, (C, C), 1)).astype(F32)] * 2, axis=1)
    Wd = RWKV_WIDTH
    pre = []
    for d, x_ref in enumerate((fw_ref, bw_ref)):
        incl, strict = _tri_masks(d)
        r, v, kk, lw, k, beta = (x_ref[:, j * Wd:(j + 1) * Wd] for j in range(6))
        b = _cumsum_rows(incl, lw)
        btot = jnp.sum(lw, axis=0, keepdims=True)
        nb = jnp.exp(-b)
        eb = jnp.exp(btot - b)
        gtot = jnp.exp(btot)
        ar = _split(jnp.concatenate([-kk * jnp.exp(b - lw), r * jnp.exp(b)], axis=0))
        bbar = _split(beta * nb)
        kbar = (k * nb).astype(BF16)
        hat = jnp.concatenate([beta * eb, k * eb], axis=0).astype(BF16)
        v16 = v.astype(BF16)

        ti = lax.broadcasted_iota(jnp.int32, (C, LANES), 0)
        si = lax.broadcasted_iota(jnp.int32, (C, LANES), 1) % RWKV_HEAD_DIM
        incl2, strict2 = ((si <= ti, si < ti) if d == 0 else (si >= ti, si > ti))
        g1 = [_mm3s((ar[0][:, s], ar[1][:, s]), bd2((bbar[0][:, s], bbar[1][:, s])), _NT) for s in sls]
        g2 = [_dg(ar[0][:, s], bd(kbar[:, s]), _NT) for s in sls]
        a_ab = [jnp.where(strict2, g[:C], 0.0) for g in g1]
        a_rb = [jnp.where(incl2, g[C:], 0.0).astype(BF16) for g in g1]
        a_ak = [jnp.where(strict2, g[:C], 0.0).astype(BF16) for g in g2]
        a_rk = [jnp.where(incl2, g[C:], 0.0).astype(BF16) for g in g2]
        vbd = [bd(v16[:, s]) for s in sls]
        av = [_dg(a, vv, _NN) for a, vv in zip(a_ak, vbd)]
        t = [eye + a for a in a_ab]
        p = [_split(a) for a in a_ab]
        p = [_split(_mm3s(pp, bd2(pp))) for pp in p]
        for _ in range(4):
            ts = [_split(tt) for tt in t]
            res = [_mm3s((jnp.concatenate([pp[0], tt[0]], axis=0), jnp.concatenate([pp[1], tt[1]], axis=0)),
                         bd2(pp)) for pp, tt in zip(p, ts)]
            p = [_split(rr[:C]) for rr in res]
            t = [tt + rr[C:] for tt, rr in zip(t, res)]
        t = [tt + _mm3s(_split(tt), bd2(pp)) for tt, pp in zip(t, p)]
        pre.append((ar[0], av, [_split(tt) for tt in t], a_rb, a_rk, vbd, v16, hat, gtot))

    hd = [(d, q) for d in range(2) for q in range(NP)]
    s_old = [s_scr[d, q] for d, q in hd]
    z = [_dg(pre[d][0][:, sls[q]], bd(s.astype(BF16)), _NT) for (d, q), s in zip(hd, s_old)]
    u = [_mm3s(pre[d][2][q], bd2(_split(zz[:C] + pre[d][1][q]))).astype(BF16) for (d, q), zz in zip(hd, z)]
    ys = [zz[C:] + _dg(jnp.concatenate([pre[d][3][q], pre[d][4][q]], axis=1),
                       jnp.concatenate([bd(uu), pre[d][5][q]], axis=0), _NN)
          for (d, q), zz, uu in zip(hd, z, u)]
    cross = [_dg(jnp.concatenate([uu, pre[d][6][:, sls[q]]], axis=0), pre[d][7][:, sls[q]], _TN)
             for (d, q), uu in zip(hd, u)]
    stores = []
    for (d, q), yy, s, cc in zip(hd, ys, s_old, cross):
        stores.append(((yf_o, yb_o)[d], (slice(None), sls[q]), yy))
        stores.append((s_scr, (d, q), s * pre[d][8][:, sls[q]] + jnp.where(low, cc[:C], cc[C:])))
    return stores


def _gla_chunk(xf, xb, a2_ref, ab_ref, of_o, ob_o, s_scr):
    Kw = GLA_K_WIDTH
    lora = 2 * Kw + 2 * GLA_V_WIDTH
    work = []
    for d, x_ref in enumerate((xf, xb)):
        incl, _ = _tri_masks(d)
        xa = (_mm3(_split(x_ref[:, lora:lora + LANES]), _split(a2_ref[:, d * Kw:(d + 1) * Kw]))
              + ab_ref[:, d * Kw:(d + 1) * Kw])
        log_a = (jnp.minimum(xa, 0.0) - jnp.log1p(jnp.exp(-jnp.abs(xa)))) * (1.0 / GLA_GATE_NORMALIZER)
        b = _cumsum_rows(incl, log_a)
        bl = jnp.sum(log_a, axis=0, keepdims=True)
        q = x_ref[:, 0:Kw] * (GLA_DK ** -0.5)
        k = x_ref[:, Kw:2 * Kw]
        q_in = (q * jnp.exp(b)).astype(BF16)
        k_in = (k * jnp.exp(-b)).astype(BF16)
        k_end = (k * jnp.exp(bl - b)).astype(BF16)
        gtot = jnp.exp(bl)
        for h in range(GLA_HEADS):
            ks = slice(h * GLA_DK, (h + 1) * GLA_DK)
            v_h = x_ref[:, 2 * Kw + h * GLA_DV:2 * Kw + (h + 1) * GLA_DV].astype(BF16)
            att = jnp.where(incl, _dg(q_in[:, ks], k_in[:, ks], _NT), 0.0).astype(BF16)
            work.append((d, h, _dg(att, v_h, _NN), q_in[:, ks], gtot[:, ks], _dg(v_h, k_end[:, ks], _TN)))

    s_old = [s_scr[d, h] for d, h, *_ in work]
    outs = [o_in + _dg(qh, s.astype(BF16), _NT) for (_, _, o_in, qh, _, _), s in zip(work, s_old)]
    stores = []
    for (d, h, _, _, g, kv), s, o in zip(work, s_old, outs):
        stores.append(((of_o, ob_o)[d], (slice(None), slice(h * GLA_DV, (h + 1) * GLA_DV)), o))
        stores.append((s_scr, (d, h), s * g + kv))
    return stores


def _scan_kernel(fw_ref, bw_ref, s0r_ref, xf, xb, a2_ref, ab_ref, s0g_ref,
                 yf_o, yb_o, str_o, of_o, ob_o, stg_o, sr_scr, sg_scr, *, nc):
    i = pl.program_id(1)

    @pl.when(i == 0)
    def _():
        sr_scr[...] = s0r_ref[...]
        sg_scr[...] = s0g_ref[...]

    stores = (_rwkv_chunk(fw_ref, bw_ref, yf_o, yb_o, sr_scr)
              + _gla_chunk(xf, xb, a2_ref, ab_ref, of_o, ob_o, sg_scr))
    for ref, idx, val in stores:
        ref[idx] = val

    @pl.when(i == nc - 1)
    def _():
        str_o[...] = sr_scr[...]
        stg_o[...] = sg_scr[...]


def _scan_call(prep, s0, zg, s0t, a2, ab, B, T):
    fw, bw = prep
    N, Wd = fw.shape[0], RWKV_WIDTH
    Wg = zg.shape[1]
    nc = T // CHUNK
    H, Dh = RWKV_HEADS, RWKV_HEAD_DIM
    pair_shape = (2, H // 2, Dh, 2 * Dh)
    fwd = pl.BlockSpec((CHUNK, Wd), lambda b, i: (b * nc + i, 0))
    bwd = pl.BlockSpec((CHUNK, Wd), lambda b, i: (b * nc + nc - 1 - i, 0))
    st_r = pl.BlockSpec((None,) + pair_shape, lambda b, i: (b, 0, 0, 0, 0))
    y = jax.ShapeDtypeStruct((N, Wd), F32)
    s0p = s0.reshape(B, 2, H // 2, 2, Dh, Dh).transpose(0, 1, 2, 4, 3, 5).reshape((B,) + pair_shape)
    pf = pl.BlockSpec((CHUNK, SCAN_PACK), lambda b, i: (b * nc + i, 0))
    pb = pl.BlockSpec((CHUNK, SCAN_PACK), lambda b, i: (b * nc + nc - 1 - i, 0))
    xf = pl.BlockSpec((CHUNK, Wg), lambda b, i: (b * nc + i, 0))
    xb = pl.BlockSpec((CHUNK, Wg), lambda b, i: (b * nc + nc - 1 - i, 0))
    of = pl.BlockSpec((CHUNK, GLA_V_WIDTH), lambda b, i: (b * nc + i, 0))
    ob = pl.BlockSpec((CHUNK, GLA_V_WIDTH), lambda b, i: (b * nc + nc - 1 - i, 0))
    gla_shape = (2, GLA_HEADS, GLA_DV, GLA_DK)
    st_g = pl.BlockSpec((None,) + gla_shape, lambda b, i: (b, 0, 0, 0, 0))
    o = jax.ShapeDtypeStruct((N, GLA_V_WIDTH), F32)
    yf, yb, stp, gf_out, gb_out, stg = pl.pallas_call(
        functools.partial(_scan_kernel, nc=nc),
        grid=(B, nc),
        in_specs=[pf, pb, st_r,
                  xf, xb, _const_spec((LANES, 2 * GLA_K_WIDTH)), _const_spec((1, 2 * GLA_K_WIDTH)), st_g],
        out_specs=[fwd, bwd, st_r, of, ob, st_g],
        out_shape=[y, y, jax.ShapeDtypeStruct((B,) + pair_shape, F32),
                   o, o, jax.ShapeDtypeStruct((B,) + gla_shape, F32)],
        scratch_shapes=[pltpu.VMEM(pair_shape, F32), pltpu.VMEM(gla_shape, F32)],
        compiler_params=_params(("parallel", "arbitrary")),
        name="scans",
    )(fw, bw, s0p, zg, zg, a2, ab, s0t)
    st_out = stp.reshape(B, 2, H // 2, Dh, 2, Dh).transpose(0, 1, 2, 4, 3, 5).reshape(B, 2, H, Dh, Dh)
    return yf, yb, st_out, gf_out, gb_out, stg


def _rope(x, cos, sin):
    n = x.shape[1]
    lane = lax.broadcasted_iota(jnp.int32, x.shape, 1)
    up = pltpu.roll(x, n - 16, 1)
    dn = pltpu.roll(x, 16, 1)
    sw = jnp.where((lane % 32) < 16, up, dn)
    reps = n // LANES
    if reps > 1:
        cos = jnp.concatenate([cos] * reps, axis=1)
        sin = jnp.concatenate([sin] * reps, axis=1)
    return x * cos + sw * sin


def _attn_prep_kernel(*refs, rope):
    if rope:
        z_ref, qn_ref, kn_ref, segq_ref, segk_ref, cos_ref, sin_ref, q_o, kn_o, kr_o, ve_o = refs
    else:
        z_ref, qn_ref, kn_ref, segq_ref, segk_ref, q_o, kn_o, kr_o, ve_o = refs
    zq = z_ref[:, 0:ATT_WIDTH]
    zk = z_ref[:, ATT_WIDTH:ATT_WIDTH + KV_WIDTH]
    qh = zq * lax.rsqrt(_seg_sum(zq * zq, segq_ref[...]) * (1.0 / HEAD_DIM) + NORM_EPS) * qn_ref[...]
    kh = zk * lax.rsqrt(_seg_sum(zk * zk, segk_ref[...]) * (1.0 / HEAD_DIM) + NORM_EPS) * kn_ref[...]
    kn_o[...] = kh
    if rope:
        qh = _rope(qh, cos_ref[...], sin_ref[...])
        kh = _rope(kh, cos_ref[...], sin_ref[...])
    q_o[...] = (qh * (LOG2_E * HEAD_DIM ** -0.5)).astype(BF16)
    kr_o[...] = kh.astype(BF16)
    v = z_ref[:, ATT_WIDTH + KV_WIDTH:ATT_WIDTH + 2 * KV_WIDTH]
    group = lax.broadcasted_iota(jnp.int32, v.shape, 1) // HEAD_DIM
    for g in range(KV_HEADS):
        ve_o[:, g * LANES:(g + 1) * LANES] = jnp.where(group == g, v, 1.0).astype(BF16)


def _attn_prep_call(za, lw, seq_len, rope_tabs):
    N, Wz = za.shape
    tm = ROW_TILE
    bps = seq_len // tm
    rope = rope_tabs is not None
    in_specs = [pl.BlockSpec((tm, Wz), lambda i: (i, 0)),
                _const_spec((1, ATT_WIDTH)), _const_spec((1, KV_WIDTH)),
                _const_spec((ATT_WIDTH, ATT_WIDTH)), _const_spec((KV_WIDTH, KV_WIDTH))]
    args = [za, lw["qn"], lw["kn"], lw["seg64"], lw["seg64"][:KV_WIDTH, :KV_WIDTH]]
    if rope:
        in_specs += [pl.BlockSpec((tm, LANES), lambda i: (i % bps, 0))] * 2
        args += list(rope_tabs)
    return pl.pallas_call(
        functools.partial(_attn_prep_kernel, rope=rope),
        grid=(N // tm,),
        in_specs=in_specs,
        out_specs=[pl.BlockSpec((tm, ATT_WIDTH), lambda i: (i, 0)),
                   pl.BlockSpec((tm, KV_WIDTH), lambda i: (i, 0)),
                   pl.BlockSpec((tm, KV_WIDTH), lambda i: (i, 0)),
                   pl.BlockSpec((tm, KV_HEADS * LANES), lambda i: (i, 0))],
        out_shape=[jax.ShapeDtypeStruct((N, ATT_WIDTH), BF16),
                   jax.ShapeDtypeStruct((N, KV_WIDTH), F32),
                   jax.ShapeDtypeStruct((N, KV_WIDTH), BF16),
                   jax.ShapeDtypeStruct((N, KV_HEADS * LANES), BF16)],
        compiler_params=_params(("parallel",)),
        name="attn_prep",
    )(*args)


def _attn_kernel(*refs, cached):
    if cached:
        q_ref, k_ref, v_ref, ck_ref, cv_ref, o_ref = refs
    else:
        q_ref, k_ref, v_ref, o_ref = refs
    G = N_HEADS // KV_HEADS
    tq = q_ref.shape[0]
    for g in range(KV_HEADS):
        gs = slice(g * HEAD_DIM, (g + 1) * HEAD_DIM)
        other = slice((1 - g) * HEAD_DIM, (2 - g) * HEAD_DIM)
        kg = k_ref[:, gs]
        vg = v_ref[:, g * LANES:(g + 1) * LANES]
        if cached:
            cv = cv_ref[...]
            own = (lax.broadcasted_iota(jnp.int32, cv.shape, 1) // HEAD_DIM) == g
            ckg = ck_ref[:, gs].astype(BF16)
            cvg = jnp.where(own, cv, 1.0).astype(BF16)
        for h0 in range(g * G, (g + 1) * G, ATT_STACK):
            heads = range(h0, h0 + ATT_STACK)
            qs = jnp.concatenate([q_ref[:, h * HEAD_DIM:(h + 1) * HEAD_DIM] for h in heads], axis=0)
            chunks = [(kg[c:c + KEY_CHUNK], vg[c:c + KEY_CHUNK]) for c in range(0, kg.shape[0], KEY_CHUNK)]
            if cached:
                chunks.append((ckg, cvg))
            m = None
            for kc, _ in chunks:
                mc = jnp.max(_dg(qs, kc, _NT), axis=-1, keepdims=True)
                m = mc if m is None else jnp.maximum(m, mc)
            oe = None
            for kc, vc in chunks:
                part = _dg(jnp.exp2(_dg(qs, kc, _NT) - m).astype(BF16), vc, _NN)
                oe = part if oe is None else oe + part
            o = oe[:, gs] / oe[:, other]
            for j, h in enumerate(heads):
                o_ref[:, h * HEAD_DIM:(h + 1) * HEAD_DIM] = o[j * tq:(j + 1) * tq]


def _attn_call(q, k, vext, cache, B, T):
    N = q.shape[0]
    tq = Q_TILE
    nq = T // tq
    in_specs = [pl.BlockSpec((tq, ATT_WIDTH), lambda b, i: (b * nq + i, 0)),
                pl.BlockSpec((T, KV_WIDTH), lambda b, i: (b, 0)),
                pl.BlockSpec((T, KV_HEADS * LANES), lambda b, i: (b, 0))]
    args = [q, k, vext]
    if cache is not None:
        ck, cv, layer = cache
        P = ck.shape[2]
        cspec = pl.BlockSpec((None, None, P, KV_WIDTH), lambda b, i: (b, layer, 0, 0))
        in_specs += [cspec, cspec]
        args += [ck, cv]
    return pl.pallas_call(
        functools.partial(_attn_kernel, cached=cache is not None),
        grid=(B, nq),
        in_specs=in_specs,
        out_specs=pl.BlockSpec((tq, ATT_WIDTH), lambda b, i: (b * nq + i, 0)),
        out_shape=jax.ShapeDtypeStruct((N, ATT_WIDTH), F32),
        compiler_params=_params(("parallel", "arbitrary")),
        name="attention",
    )(*args)


def _merge_kernel(yf, yb, bon, g, oa, gf, gb, gr, mr, ma, mg, x_ref, m_ref,
                  lng, lnb, gn, nf, seg_ref, pr, pa, pg, wo, rw, rb,
                  x1_o, h2_o, ti_o, tg_o):
    seg = seg_ref[...]
    inv = 1.0 / RWKV_HEAD_DIM
    y = yf[...] + yb[...]
    mu = _seg_sum(y, seg) * inv
    yc = y - mu
    var = _seg_sum(yc * yc, seg) * inv
    o_r = (yc * lax.rsqrt(var + RWKV_GN_EPS) * lng[...] + lnb[...] + bon[...]) * g[...]

    gate_r = gr[...]
    silu_r = gate_r * _sigmoid(gate_r)
    gnv = gn[...]
    cols = []
    for h in range(GLA_HEADS):
        vs = slice(h * GLA_DV, (h + 1) * GLA_DV)
        o = gf[:, vs] + gb[:, vs]
        ms = jnp.mean(o * o, axis=-1, keepdims=True)
        cols.append(o * lax.rsqrt(ms + NORM_EPS) * gnv * silu_r[:, vs])
    o_g = jnp.concatenate(cols, axis=1)

    merged = (_sigmoid(mr[...]) * _dot(o_r.astype(BF16), pr[...])
              + _sigmoid(ma[...]) * _dot(oa[...].astype(BF16), pa[...])
              + _sigmoid(mg[...]) * _dot(o_g.astype(BF16), pg[...]))
    x1 = x_ref[...] + m_ref[2:3, :] * _dot(merged.astype(BF16), wo[...])
    x1_o[...] = x1
    hn = x1 * lax.rsqrt(jnp.mean(x1 * x1, axis=-1, keepdims=True) + NORM_EPS) * nf[...]
    h2 = hn * (1.0 + m_ref[4:5, :]) + m_ref[3:4, :]
    h2_o[...] = h2

    logits = _mm3(_split(h2), _split(rw[...])) + rb[...]
    lane = lax.broadcasted_iota(jnp.int32, logits.shape, 1)
    vals, idxs = [], []
    for _ in range(TOP_K):
        m = jnp.max(logits, axis=-1, keepdims=True)
        idx = jnp.min(jnp.where(logits == m, lane, LANES), axis=-1, keepdims=True)
        vals.append(m)
        idxs.append(idx)
        logits = jnp.where(lane == idx, -jnp.inf, logits)
    es = [jnp.exp(vv - vals[0]) for vv in vals]
    den = es[0] + es[1] + es[2] + es[3]
    ti = jnp.zeros(logits.shape, jnp.int32)
    tg = jnp.zeros(logits.shape, F32)
    for j in range(TOP_K):
        ti = jnp.where(lane == j, idxs[j], ti)
        tg = jnp.where(lane == j, es[j] / den, tg)
    ti_o[...] = ti
    tg_o[...] = tg


def _merge_call(yf, yb, bon, g, oa, gf, gb, zg, zm, x, mod, lw, rows_per_cond):
    N, D = x.shape
    tm = ROW_TILE
    Wd = RWKV_WIDTH

    def rows(w, col=0):
        return pl.BlockSpec((tm, w), lambda i: (i, col))

    in_specs = [rows(Wd)] * 4 + [rows(ATT_WIDTH), rows(GLA_V_WIDTH), rows(GLA_V_WIDTH),
                                 rows(GLA_V_WIDTH, 2), rows(D, 0), rows(D, 1), rows(D, 2), rows(D),
                                 pl.BlockSpec((None, 6, D), lambda i: ((i * tm) // rows_per_cond, 0, 0)),
                                 _const_spec((1, Wd)), _const_spec((1, Wd)), _const_spec((1, GLA_DV)),
                                 _const_spec((1, D)), _const_spec((Wd, Wd)),
                                 _const_spec((Wd, D)), _const_spec((ATT_WIDTH, D)), _const_spec((GLA_V_WIDTH, D)),
                                 _const_spec((D, D)), _const_spec((D, LANES)), _const_spec((1, LANES))]
    return pl.pallas_call(
        _merge_kernel,
        grid=(N // tm,),
        in_specs=in_specs,
        out_specs=[rows(D), rows(D), rows(LANES), rows(LANES)],
        out_shape=[jax.ShapeDtypeStruct((N, D), F32), jax.ShapeDtypeStruct((N, D), F32),
                   jax.ShapeDtypeStruct((N, LANES), jnp.int32), jax.ShapeDtypeStruct((N, LANES), F32)],
        compiler_params=_params(("parallel",)),
        name="merge",
    )(yf, yb, bon, g, oa, gf, gb, zg, zm, zm, zm, x, mod,
      lw["ln_g"], lw["ln_b"], lw["gla_norm"], lw["norm_ffn"], lw["seg64"],
      lw["p_rwkv"], lw["p_attn"], lw["p_gla"], lw["w_out"], lw["router_w"], lw["router_b"])


def _moe_kernel(be_ref, na_ref, x_ref, gate_ref, w1_ref, b1_ref, w2_ref, b2_ref, o_ref, w1_scr, w2_scr):
    i = pl.program_id(0)

    @pl.when(jnp.logical_or(i == 0, be_ref[i] != be_ref[jnp.maximum(i - 1, 0)]))
    def _():
        w1_scr[...] = w1_ref[...].astype(BF16)
        w2_scr[...] = w2_ref[...].astype(BF16)

    @pl.when(i < na_ref[0])
    def _():
        z = _dot(x_ref[...].astype(BF16), w1_scr[...]) + b1_ref[...]
        glu = jnp.minimum(z[:, :D_EXPERT], SWIGLU_LIMIT)
        lin = jnp.clip(z[:, D_EXPERT:], -SWIGLU_LIMIT, SWIGLU_LIMIT)
        act = glu * _sigmoid(SWIGLU_ALPHA * glu) * (lin + 1.0)
        o_ref[...] = (_dot(act.astype(BF16), w2_scr[...]) + b2_ref[...]) * gate_ref[...]

    @pl.when(i >= na_ref[0])
    def _():
        o_ref[...] = jnp.zeros_like(o_ref)


def _moe_call(xg, row_gate, block_exp, n_active, layer, w1, b1, w2, b2):
    R, D = xg.shape
    tb = EXPERT_BLOCK
    F2 = w1.shape[3]
    grid_spec = pltpu.PrefetchScalarGridSpec(
        num_scalar_prefetch=2,
        grid=(R // tb,),
        in_specs=[pl.BlockSpec((tb, D), lambda i, be, na: (i, 0)),
                  pl.BlockSpec((tb, 1), lambda i, be, na: (i, 0)),
                  pl.BlockSpec((None, None, D, F2), lambda i, be, na: (layer, be[i], 0, 0)),
                  pl.BlockSpec((None, 1, F2), lambda i, be, na: (be[i], 0, 0)),
                  pl.BlockSpec((None, None, F2 // 2, D), lambda i, be, na: (layer, be[i], 0, 0)),
                  pl.BlockSpec((None, 1, D), lambda i, be, na: (be[i], 0, 0))],
        out_specs=pl.BlockSpec((tb, D), lambda i, be, na: (i, 0)),
        scratch_shapes=[pltpu.VMEM((D, F2), BF16), pltpu.VMEM((F2 // 2, D), BF16)],
    )
    return pl.pallas_call(
        _moe_kernel,
        grid_spec=grid_spec,
        out_shape=jax.ShapeDtypeStruct((R, D), F32),
        compiler_params=_params(("arbitrary",)),
        name="moe_experts",
    )(block_exp, n_active, xg, row_gate, w1, b1, w2, b2)


def _combine_kernel(x1_ref, m_ref, y0_ref, y1_ref, y2_ref, y3_ref, o_ref):
    y = (y0_ref[...] + y1_ref[...]) + (y2_ref[...] + y3_ref[...])
    o_ref[...] = x1_ref[...] + m_ref[5:6, :] * y


def _combine_call(x1, mod, ys, rows_per_cond):
    N, D = x1.shape
    tm = ROW_TILE
    rows = pl.BlockSpec((tm, D), lambda i: (i, 0))
    return pl.pallas_call(
        _combine_kernel,
        grid=(N // tm,),
        in_specs=([rows, pl.BlockSpec((None, 6, D), lambda i: ((i * tm) // rows_per_cond, 0, 0))]
                  + [pl.BlockSpec((None, tm, D), functools.partial(lambda i, j: (j, i, 0), j=j))
                     for j in range(TOP_K)]),
        out_specs=rows,
        out_shape=jax.ShapeDtypeStruct((N, D), F32),
        compiler_params=_params(("parallel",)),
        name="combine",
    )(x1, mod, *([ys] * TOP_K))


def _moe_ffn(h2, top_i, top_g, lw):
    N, D = h2.shape
    n_as = N * TOP_K
    n_blocks = -(-n_as // EXPERT_BLOCK) + N_EXPERTS
    R = n_blocks * EXPERT_BLOCK
    ex = jnp.arange(N_EXPERTS, dtype=jnp.int32)

    def lut(onehot, table):
        return jnp.sum(jnp.where(onehot, table[None, :], 0), axis=1)

    flat_e = top_i.reshape(n_as)
    flat_g = top_g.reshape(n_as)
    oh_a = flat_e[:, None] == ex[None, :]
    counts = jnp.sum(oh_a, axis=0, dtype=jnp.int32)
    starts = jnp.cumsum(counts) - counts
    padded = (counts + EXPERT_BLOCK - 1) // EXPERT_BLOCK * EXPERT_BLOCK
    pends = jnp.cumsum(padded)
    pstarts = pends - padded
    order = jnp.argsort(flat_e).astype(jnp.int32)
    rank = jnp.argsort(order).astype(jnp.int32)
    pos = rank + lut(oh_a, pstarts - starts)
    blk_start = jnp.arange(n_blocks, dtype=jnp.int32) * EXPERT_BLOCK
    block_exp = jnp.minimum(jnp.sum(pends[None, :] <= blk_start[:, None], axis=1),
                            N_EXPERTS - 1).astype(jnp.int32)
    n_active = (pends[-1] // EXPERT_BLOCK).astype(jnp.int32).reshape(1)
    oh_p = jnp.repeat(block_exp, EXPERT_BLOCK)[:, None] == ex[None, :]
    idx = jnp.arange(R, dtype=jnp.int32) - lut(oh_p, pstarts)
    valid = (idx >= 0) & (idx < lut(oh_p, counts))
    asg = order[jnp.clip(lut(oh_p, starts) + idx, 0, n_as - 1)]
    row_tok = jnp.where(valid, asg // TOP_K, 0)
    row_gate = jnp.where(valid, flat_g[asg], 0.0)
    out = _moe_call(h2[row_tok], row_gate.reshape(R, 1), block_exp, n_active,
                    lw["layer"], lw["moe_w1"], lw["moe_b1"], lw["moe_w2"], lw["moe_b2"])
    return out[pos.reshape(N, TOP_K).T.reshape(n_as)].reshape(TOP_K, N, D)


def _seg_matrix(n, seg):
    idx = jnp.arange(n) // seg
    return (idx[:, None] == idx[None, :]).astype(BF16)


def _rope_tables(T):
    quarter = HEAD_DIM // 4
    inv_freq = ROPE_THETA ** (-jnp.arange(quarter, dtype=F32) / quarter)
    t = jnp.arange(T)
    row = (t // GRID_W).astype(F32)
    col = (t % GRID_W).astype(F32)
    ang_r = row[:, None] * inv_freq
    ang_c = col[:, None] * inv_freq
    cos = jnp.concatenate([jnp.cos(ang_r)] * 2 + [jnp.cos(ang_c)] * 2, axis=1)
    sin = jnp.concatenate([-jnp.sin(ang_r), jnp.sin(ang_r), -jnp.sin(ang_c), jnp.sin(ang_c)], axis=1)
    return jnp.concatenate([cos, cos], axis=1), jnp.concatenate([sin, sin], axis=1)


def _layer_weights(p, l):
    Wd = RWKV_WIDTH
    w_in = p["w_in"][l]
    c0 = RWKV_COLS
    aq, ak, av, gq, gk, gv, gl, gr = (c0, c0 + 512, c0 + 640, c0 + 768, c0 + 1280, c0 + 1792, c0 + 2816, c0 + 2848)
    m0 = gr + GLA_V_WIDTH
    D = D_MODEL
    w_g = jnp.concatenate([w_in[:, gq:gl], w_in[:, gr:m0], w_in[:, gl:gr],
                           jnp.zeros((D, LANES - 2 * GLA_LORA), F32)], axis=1)
    z64 = jnp.zeros((DECAY_LORA, Wd), F32)
    w2 = jnp.concatenate([jnp.concatenate([p["rwkv_w2"][l, 0], z64], axis=1),
                          jnp.concatenate([z64, p["rwkv_w2"][l, 1]], axis=1)], axis=0)
    a2 = jnp.concatenate([jnp.concatenate([p["rwkv_a2"][l, 0], z64], axis=1),
                          jnp.concatenate([z64, p["rwkv_a2"][l, 1]], axis=1)], axis=0)
    z16 = jnp.zeros((GLA_LORA, GLA_K_WIDTH), F32)
    ga2 = jnp.concatenate([jnp.concatenate([p["gla_a2"][l, 0], z16], axis=1),
                           jnp.concatenate([z16, p["gla_a2"][l, 1]], axis=1),
                           jnp.zeros((LANES - 2 * GLA_LORA, 2 * GLA_K_WIDTH), F32)], axis=0)
    return {
        "w_r": w_in[:, :c0].astype(BF16), "w_a": w_in[:, aq:gq].astype(BF16),
        "w_g": w_g.astype(BF16), "w_m": w_in[:, m0:].astype(BF16),
        "norm_mix": p["norm_mix"][l].reshape(1, D), "norm_ffn": p["norm_ffn"][l].reshape(1, D),
        "mu": p["rwkv_mu"][l].reshape(1, c0),
        "kk": p["rwkv_kk"][l].reshape(1, Wd), "ka": p["rwkv_ka"][l].reshape(1, Wd),
        "rk": p["rwkv_rk"][l].reshape(1, Wd),
        "w0": p["rwkv_w0"][l].reshape(1, 2 * Wd), "a0": p["rwkv_a0"][l].reshape(1, 2 * Wd),
        "w2": w2, "a2": a2, "g2": p["rwkv_g2"][l],
        "ln_g": p["rwkv_ln_g"][l].reshape(1, Wd), "ln_b": p["rwkv_ln_b"][l].reshape(1, Wd),
        "seg64": _seg_matrix(Wd, RWKV_HEAD_DIM),
        "qn": jnp.tile(p["attn_qn"][l], N_HEADS).reshape(1, ATT_WIDTH),
        "kn": jnp.tile(p["attn_kn"][l], KV_HEADS).reshape(1, KV_WIDTH),
        "gla_a2": ga2, "gla_ab": p["gla_ab"][l].reshape(1, 2 * GLA_K_WIDTH),
        "gla_norm": p["gla_norm"][l].reshape(1, GLA_DV),
        "p_rwkv": p["p_rwkv"][l].astype(BF16), "p_attn": p["p_attn"][l].astype(BF16),
        "p_gla": p["p_gla"][l].astype(BF16), "w_out": p["w_out"][l].astype(BF16),
        "router_w": jnp.concatenate([p["router_w"][l], jnp.zeros((D, LANES - N_EXPERTS), F32)], axis=1),
        "router_b": jnp.concatenate([p["router_b"][l], jnp.full((LANES - N_EXPERTS,), -jnp.inf, F32)]).reshape(1, LANES),
        "layer": l, "moe_w1": p["moe_w1"], "moe_b1": p["moe_b1"][l].reshape(N_EXPERTS, 1, 2 * D_EXPERT),
        "moe_w2": p["moe_w2"], "moe_b2": p["moe_b2"][l].reshape(N_EXPERTS, 1, D_MODEL),
    }


def _trunk_layer(x, mod, lw, B, T, ctx):
    N = B * T
    rpc = N if mod.shape[0] == 1 else T
    zr, za, zg, zm = (_inproj_call(x, mod, lw["norm_mix"], lw[name], rpc, lw[name].shape[1])
                      for name in ("w_r", "w_a", "w_g", "w_m"))

    fw, bw, g, bon = _rwkv_prep_call(zr, T, lw)
    if ctx is None:
        s0r = jnp.zeros((B, 2, RWKV_HEADS, RWKV_HEAD_DIM, RWKV_HEAD_DIM), F32)
        s0g = jnp.zeros((B, 2, GLA_HEADS, GLA_DV, GLA_DK), F32)
        cache, tabs = None, None
    else:
        s0r = ctx["rwkv"]
        s0g = jnp.swapaxes(ctx["gla"], -1, -2)
        cache, tabs = (ctx["k"], ctx["v"], ctx["layer"]), _rope_tables(T)
    yf, yb, s_r, gf, gb, s_g = _scan_call((fw, bw), s0r, zg, s0g, lw["gla_a2"], lw["gla_ab"], B, T)
    q, kn, kr, vext = _attn_prep_call(za, lw, T, tabs)
    oa = _attn_call(q, kr, vext, cache, B, T)
    x1, h2, ti, tg = _merge_call(yf, yb, bon, g, oa, gf, gb, zg, zm, x, mod, lw, rpc)
    y4 = _moe_ffn(h2, ti[:, :TOP_K], tg[:, :TOP_K], lw)
    x2 = _combine_call(x1, mod, y4, rpc)
    return x2, (kn, za[:, ATT_WIDTH + KV_WIDTH:], s_r, jnp.swapaxes(s_g, -1, -2))


def kernel(x_prompt, x_sample, c, cache_k, cache_v, state_rwkv, state_gla, c_ctx, ada_w, ada_b, norm_mix, norm_ffn, w_in, rwkv_mu, rwkv_w0, rwkv_w2, rwkv_a0, rwkv_a2, rwkv_g2, rwkv_kk, rwkv_ka, rwkv_rk, rwkv_ln_g, rwkv_ln_b, attn_qn, attn_kn, gla_a2, gla_ab, gla_norm, p_rwkv, p_attn, p_gla, w_out, router_w, router_b, moe_w1, moe_b1, moe_w2, moe_b2):
    p = {"w_in": w_in, "norm_mix": norm_mix, "norm_ffn": norm_ffn, "rwkv_mu": rwkv_mu, "rwkv_w0": rwkv_w0,
         "rwkv_w2": rwkv_w2, "rwkv_a0": rwkv_a0, "rwkv_a2": rwkv_a2, "rwkv_g2": rwkv_g2, "rwkv_kk": rwkv_kk,
         "rwkv_ka": rwkv_ka, "rwkv_rk": rwkv_rk, "rwkv_ln_g": rwkv_ln_g, "rwkv_ln_b": rwkv_ln_b,
         "attn_qn": attn_qn, "attn_kn": attn_kn, "gla_a2": gla_a2, "gla_ab": gla_ab, "gla_norm": gla_norm,
         "p_rwkv": p_rwkv, "p_attn": p_attn, "p_gla": p_gla, "w_out": w_out, "router_w": router_w,
         "router_b": router_b, "moe_w1": moe_w1, "moe_b1": moe_b1, "moe_w2": moe_w2, "moe_b2": moe_b2}
    Bp, Tp, D = x_prompt.shape
    Bs, Ts, _ = x_sample.shape
    L = ada_w.shape[0]
    P = cache_k.shape[2]

    cond8 = jnp.concatenate([c_ctx[None, :], c, jnp.zeros((8 - 1 - Bs, D), F32)], axis=0)
    mod = _ada_call(cond8, ada_w, ada_b).reshape(L, 8, 6, D)
    weights = [_layer_weights(p, l) for l in range(L)]
    ck = cache_k.reshape(Bs, L, P, KV_WIDTH)
    cv = cache_v.reshape(Bs, L, P, KV_WIDTH)

    yp = x_prompt.reshape(Bp * Tp, D)
    ks, vs, srs, sgs = [], [], [], []
    for l in range(L):
        yp, (kn, vh, s_r, s_g) = _trunk_layer(yp, mod[l, 0:1], weights[l], Bp, Tp, None)
        ks.append(kn.reshape(Bp, Tp, KV_HEADS, HEAD_DIM))
        vs.append(vh.reshape(Bp, Tp, KV_HEADS, HEAD_DIM))
        srs.append(s_r)
        sgs.append(s_g)

    ys = x_sample.reshape(Bs * Ts, D)
    for l in range(L):
        ctx = {"k": ck, "v": cv, "layer": l, "rwkv": state_rwkv[:, l], "gla": state_gla[:, l]}
        ys, _ = _trunk_layer(ys, mod[l, 1:1 + Bs], weights[l], Bs, Ts, ctx)

    return (yp.reshape(Bp, Tp, D), ys.reshape(Bs, Ts, D),
            jnp.stack(ks, axis=1), jnp.stack(vs, axis=1),
            jnp.stack(srs, axis=1), jnp.stack(sgs, axis=1))
```

```python
import functools

import jax
import jax.numpy as jnp
from jax import lax
from jax.experimental import pallas as pl
from jax.experimental.pallas import tpu as pltpu

D_MODEL = 1024
DEPTH = 2
GRID_W = 64
NORM_EPS = 1e-6

RWKV_HEADS = 8
RWKV_HEAD_DIM = 64
RWKV_WIDTH = RWKV_HEADS * RWKV_HEAD_DIM
DECAY_LORA = 64
ICLR_LORA = 64
GATE_LORA = 128
RWKV_GN_EPS = 6.4e-4
RWKV_COLS = 3 * RWKV_WIDTH + 2 * DECAY_LORA + 2 * ICLR_LORA + GATE_LORA
SCAN_PACK = 6 * RWKV_WIDTH

N_HEADS = 8
KV_HEADS = 2
HEAD_DIM = 64
ATT_WIDTH = N_HEADS * HEAD_DIM
KV_WIDTH = KV_HEADS * HEAD_DIM
ROPE_THETA = 10000.0

GLA_HEADS = 4
GLA_DK = 128
GLA_DV = 256
GLA_K_WIDTH = GLA_HEADS * GLA_DK
GLA_V_WIDTH = GLA_HEADS * GLA_DV
GLA_LORA = 16
GLA_GATE_NORMALIZER = 16.0

N_EXPERTS = 32
TOP_K = 4
D_EXPERT = 1024
SWIGLU_ALPHA = 1.702
SWIGLU_LIMIT = 7.0
EXPERT_BLOCK = 512

LANES = 128
CHUNK = 64
SCAN_BATCH = 2
ROW_TILE = 256
INPROJ_TILE = 512
Q_TILE = 128
ATT_STACK = 2
KEY_CHUNK = 512
VMEM_LIMIT = 52 * 1024 * 1024
LOG2_E = 1.4426950408889634

F32 = jnp.float32
BF16 = jnp.bfloat16
HI = lax.Precision.HIGHEST


def _dot(a, b, prec=None):
    return jnp.dot(a, b, preferred_element_type=F32, precision=prec)


def _dot_nt(a, b, prec=None):
    return lax.dot_general(a, b, (((1,), (1,)), ((), ())), preferred_element_type=F32, precision=prec)


def _dot_tn(a, b, prec=None):
    return lax.dot_general(a, b, (((0,), (0,)), ((), ())), preferred_element_type=F32, precision=prec)


_NN = (((1,), (0,)), ((), ()))
_NT = (((1,), (1,)), ((), ()))
_TN = (((0,), (0,)), ((), ()))


def _dg(a, b, dn):
    return lax.dot_general(a, b, dn, preferred_element_type=F32)


def _split(x):
    h = x.astype(BF16)
    return h, (x - h.astype(F32)).astype(BF16)


def _mm3(a, b, dn=_NN):
    return _dg(a[0], b[0], dn) + _dg(a[0], b[1], dn) + _dg(a[1], b[0], dn)


def _mm3s(a, b, dn=_NN):
    m = a[0].shape[0]
    top = _dg(jnp.concatenate([a[0], a[1]], axis=0), b[0], dn)
    return top[:m] + top[m:] + _dg(a[0], b[1], dn)


def _cumsum_rows(mask, x):
    m = jnp.where(mask, 1.0, 0.0).astype(BF16)
    h = x.astype(BF16)
    r1 = x - h.astype(F32)
    mid = r1.astype(BF16)
    lo = (r1 - mid.astype(F32)).astype(BF16)
    return _dg(m, h, _NN) + _dg(m, mid, _NN) + _dg(m, lo, _NN)


def _seg_sum(x, seg_bf16):
    xh = x.astype(BF16)
    xl = (x - xh.astype(F32)).astype(BF16)
    return _dot(xh, seg_bf16) + _dot(xl, seg_bf16)


def _sigmoid(x):
    return 1.0 / (1.0 + jnp.exp(-x))


def _params(sem):
    return pltpu.CompilerParams(dimension_semantics=sem, vmem_limit_bytes=VMEM_LIMIT)


def _const_spec(shape):
    nd = len(shape)
    return pl.BlockSpec(shape, lambda *_: (0,) * nd)


def _ada_kernel(c_ref, w_ref, b_ref, o_ref):
    c = c_ref[...]
    s = c * _sigmoid(c)
    o_ref[...] = _mm3(_split(s), _split(w_ref[...])) + b_ref[...]


def _ada_call(cond8, ada_w, ada_b):
    L, D, W = ada_w.shape
    tn = 1536
    return pl.pallas_call(
        _ada_kernel,
        grid=(L, W // tn),
        in_specs=[pl.BlockSpec((8, D), lambda l, j: (0, 0)),
                  pl.BlockSpec((None, D, tn), lambda l, j: (l, 0, j)),
                  pl.BlockSpec((None, 1, tn), lambda l, j: (l, 0, j))],
        out_specs=pl.BlockSpec((None, 8, tn), lambda l, j: (l, 0, j)),
        out_shape=jax.ShapeDtypeStruct((L, 8, W), F32),
        compiler_params=_params(("parallel", "parallel")),
        name="ada",
    )(cond8, ada_w, ada_b.reshape(L, 1, W))


def _inproj_kernel(x_ref, m_ref, g_ref, w_ref, o_ref, h_scr):
    @pl.when(pl.program_id(1) == 0)
    def _():
        x = x_ref[...]
        y = x * lax.rsqrt(jnp.mean(x * x, axis=-1, keepdims=True) + NORM_EPS) * g_ref[...]
        h = y * (1.0 + m_ref[1:2, :]) + m_ref[0:1, :]
        h_scr[...] = h.astype(BF16)

    o_ref[...] = _dot(h_scr[...], w_ref[...])


def _inproj_call(x, mod, gain, w, rows_per_cond, tn):
    N, D = x.shape
    W = w.shape[1]
    tm = INPROJ_TILE
    assert N % tm == 0 and rows_per_cond % tm == 0, (N, rows_per_cond, tm)
    return pl.pallas_call(
        _inproj_kernel,
        grid=(N // tm, W // tn),
        in_specs=[pl.BlockSpec((tm, D), lambda i, j: (i, 0)),
                  pl.BlockSpec((None, 6, D), lambda i, j: ((i * tm) // rows_per_cond, 0, 0)),
                  pl.BlockSpec((1, D), lambda i, j: (0, 0)),
                  pl.BlockSpec((D, tn), lambda i, j: (0, j))],
        out_specs=pl.BlockSpec((tm, tn), lambda i, j: (i, j)),
        out_shape=jax.ShapeDtypeStruct((N, W), F32),
        scratch_shapes=[pltpu.VMEM((tm, D), BF16)],
        compiler_params=_params(("parallel", "arbitrary")),
        name="inproj",
    )(x, mod, gain, w)


def _rwkv_prep_kernel(z_ref, zp_ref, zn_ref, mu_ref, kkp_ref, ka_ref, rk_ref, w0_ref, a0_ref,
                      w2_ref, a2_ref, g2_ref, seg_ref,
                      fw_o, bw_o, g_o, bon_o, *, tm, blocks_per_seq):
    i = pl.program_id(0)
    z = z_ref[...]
    row = lax.broadcasted_iota(jnp.int32, (tm, 1), 0)
    first = (i % blocks_per_seq) == 0
    last = (i % blocks_per_seq) == blocks_per_seq - 1
    pz = jnp.where(first, 0.0, zp_ref[7:8, :])
    nz = jnp.where(last, 0.0, zn_ref[0:1, :])
    prev = jnp.where(row == 0, pz, pltpu.roll(z, 1, 0))
    nxt = jnp.where(row == tm - 1, nz, pltpu.roll(z, tm - 1, 0))
    zm = z + mu_ref[...] * (0.5 * (prev + nxt) - z)

    Wd = RWKV_WIDTH
    r = zm[:, 0:Wd]
    k = zm[:, Wd:2 * Wd]
    v = zm[:, 2 * Wd:3 * Wd]
    lw = zm[:, 3 * Wd:3 * Wd + 128]
    la = zm[:, 3 * Wd + 128:3 * Wd + 256]
    lg = zm[:, 3 * Wd + 256:3 * Wd + 384]
    seg = seg_ref[...]

    kk = k * kkp_ref[...]
    kk = kk / jnp.maximum(jnp.sqrt(_seg_sum(kk * kk, seg)), 1e-12)
    w_raw = _mm3(_split(jnp.tanh(lw)), _split(w2_ref[...])) + w0_ref[...]
    logw = -_sigmoid(w_raw) * 0.6065306597126334
    a = _sigmoid(_mm3(_split(la), _split(a2_ref[...])) + a0_ref[...])
    ka = ka_ref[...]
    rk = rk_ref[...]
    bonus = jnp.zeros_like(r)
    for d, dir_o in enumerate((fw_o, bw_o)):
        a_d = a[:, d * Wd:(d + 1) * Wd]
        k_d = k * (1.0 + (a_d - 1.0) * ka)
        dir_o[:, 0:Wd] = r
        dir_o[:, Wd:2 * Wd] = v
        dir_o[:, 2 * Wd:3 * Wd] = kk
        dir_o[:, 3 * Wd:4 * Wd] = logw[:, d * Wd:(d + 1) * Wd]
        dir_o[:, 4 * Wd:5 * Wd] = k_d
        dir_o[:, 5 * Wd:6 * Wd] = kk * a_d
        bonus = bonus + _seg_sum(r * k_d * rk, seg) * v
    bon_o[...] = bonus
    g_o[...] = _mm3(_split(_sigmoid(lg)), _split(g2_ref[...]))


def _rwkv_prep_call(zr, seq_len, lw):
    N, Wz = zr.shape
    tm = ROW_TILE
    bps = seq_len // tm
    Wd = RWKV_WIDTH
    nb8 = N // 8
    row_spec = pl.BlockSpec((tm, Wd), lambda i: (i, 0))
    out = jax.ShapeDtypeStruct((N, Wd), F32)
    return pl.pallas_call(
        functools.partial(_rwkv_prep_kernel, tm=tm, blocks_per_seq=bps),
        grid=(N // tm,),
        in_specs=[pl.BlockSpec((tm, Wz), lambda i: (i, 0)),
                  pl.BlockSpec((8, Wz), lambda i: (jnp.maximum(i * (tm // 8) - 1, 0), 0)),
                  pl.BlockSpec((8, Wz), lambda i: (jnp.minimum((i + 1) * (tm // 8), nb8 - 1), 0)),
                  _const_spec((1, Wz)), _const_spec((1, Wd)), _const_spec((1, Wd)), _const_spec((1, Wd)),
                  _const_spec((1, 2 * Wd)), _const_spec((1, 2 * Wd)),
                  _const_spec((128, 2 * Wd)), _const_spec((128, 2 * Wd)), _const_spec((128, Wd)),
                  _const_spec((Wd, Wd))],
        out_specs=[pl.BlockSpec((tm, SCAN_PACK), lambda i: (i, 0))] * 2 + [row_spec] * 2,
        out_shape=[jax.ShapeDtypeStruct((N, SCAN_PACK), F32)] * 2 + [out] * 2,
        compiler_params=_params(("parallel",)),
        name="rwkv_prep",
    )(zr, zr, zr, lw["mu"], lw["kk"], lw["ka"], lw["rk"], lw["w0"], lw["a0"],
      lw["w2"], lw["a2"], lw["g2"], lw["seg64"])


def _tri_masks(d):
    ti = lax.broadcasted_iota(jnp.int32, (CHUNK, CHUNK), 0)
    si = lax.broadcasted_iota(jnp.int32, (CHUNK, CHUNK), 1)
    if d == 0:
        return si <= ti, si < ti
    return si >= ti, si > ti


def _rwkv_chunk(fw_ref, bw_ref, yf_o, yb_o, s_scr):
    C = CHUNK
    NP = RWKV_HEADS // 2
    sls = [slice(q * LANES, (q + 1) * LANES) for q in range(NP)]
    low = lax.broadcasted_iota(jnp.int32, (C, LANES), 1) < RWKV_HEAD_DIM

    def bd(x):
        zero = jnp.zeros_like(x)
        return jnp.concatenate([jnp.where(low, x, zero), jnp.where(low, zero, x)], axis=0)

    def bd2(x):
        return bd(x[0]), bd(x[1])

    eye = jnp.concatenate([(lax.broadcasted_iota(jnp.int32, (C, C), 0)
                            == lax.broadcasted_iota(jnp.int32, (C, C), 1)).astype(F32)] * 2, axis=1)
    Wd = RWKV_WIDTH
    pre = []
    for d, x_ref in enumerate((fw_ref, bw_ref)):
        incl, strict = _tri_masks(d)
        r, v, kk, lw, k, beta = (x_ref[:, j * Wd:(j + 1) * Wd] for j in range(6))
        b = _cumsum_rows(incl, lw)
        btot = jnp.sum(lw, axis=0, keepdims=True)
        nb = jnp.exp(-b)
        eb = jnp.exp(btot - b)
        gtot = jnp.exp(btot)
        ar = _split(jnp.concatenate([-kk * jnp.exp(b - lw), r * jnp.exp(b)], axis=0))
        bbar = _split(beta * nb)
        kbar = (k * nb).astype(BF16)
        hat = jnp.concatenate([beta * eb, k * eb], axis=0).astype(BF16)
        v16 = v.astype(BF16)

        ti = lax.broadcasted_iota(jnp.int32, (C, LANES), 0)
        si = lax.broadcasted_iota(jnp.int32, (C, LANES), 1) % RWKV_HEAD_DIM
        incl2, strict2 = ((si <= ti, si < ti) if d == 0 else (si >= ti, si > ti))
        g1 = [_mm3s((ar[0][:, s], ar[1][:, s]), bd2((bbar[0][:, s], bbar[1][:, s])), _NT) for s in sls]
        g2 = [_dg(ar[0][:, s], bd(kbar[:, s]), _NT) for s in sls]
        a_ab = [jnp.where(strict2, g[:C], 0.0) for g in g1]
        a_rb = [jnp.where(incl2, g[C:], 0.0).astype(BF16) for g in g1]
        a_ak = [jnp.where(strict2, g[:C], 0.0).astype(BF16) for g in g2]
        a_rk = [jnp.where(incl2, g[C:], 0.0).astype(BF16) for g in g2]
        vbd = [bd(v16[:, s]) for s in sls]
        av = [_dg(a, vv, _NN) for a, vv in zip(a_ak, vbd)]
        t = [eye + a for a in a_ab]
        p = [_split(a) for a in a_ab]
        p = [_split(_mm3s(pp, bd2(pp))) for pp in p]
        for _ in range(4):
            ts = [_split(tt) for tt in t]
            res = [_mm3s((jnp.concatenate([pp[0], tt[0]], axis=0), jnp.concatenate([pp[1], tt[1]], axis=0)),
                         bd2(pp)) for pp, tt in zip(p, ts)]
            p = [_split(rr[:C]) for rr in res]
            t = [tt + rr[C:] for tt, rr in zip(t, res)]
        t = [tt + _mm3s(_split(tt), bd2(pp)) for tt, pp in zip(t, p)]
        pre.append((ar[0], av, [_split(tt) for tt in t], a_rb, a_rk, vbd, v16, hat, gtot))

    hd = [(d, q) for d in range(2) for q in range(NP)]
    s_old = [s_scr[d, q] for d, q in hd]
    z = [_dg(pre[d][0][:, sls[q]], bd(s.astype(BF16)), _NT) for (d, q), s in zip(hd, s_old)]
    u = [_mm3s(pre[d][2][q], bd2(_split(zz[:C] + pre[d][1][q]))).astype(BF16) for (d, q), zz in zip(hd, z)]
    ys = [zz[C:] + _dg(jnp.concatenate([pre[d][3][q], pre[d][4][q]], axis=1),
                       jnp.concatenate([bd(uu), pre[d][5][q]], axis=0), _NN)
          for (d, q), zz, uu in zip(hd, z, u)]
    cross = [_dg(jnp.concatenate([uu, pre[d][6][:, sls[q]]], axis=0), pre[d][7][:, sls[q]], _TN)
             for (d, q), uu in zip(hd, u)]
    stores = []
    for (d, q), yy, s, cc in zip(hd, ys, s_old, cross):
        stores.append(((yf_o, yb_o)[d], (slice(None), sls[q]), yy))
        stores.append((s_scr, (d, q), s * pre[d][8][:, sls[q]] + jnp.where(low, cc[:C], cc[C:])))
    return stores


def _gla_chunk(xf, xb, a2_ref, ab_ref, of_o, ob_o, s_scr):
    Kw = GLA_K_WIDTH
    lora = 2 * Kw + 2 * GLA_V_WIDTH
    work = []
    for d, x_ref in enumerate((xf, xb)):
        incl, _ = _tri_masks(d)
        xa = (_mm3(_split(x_ref[:, lora:lora + LANES]), _split(a2_ref[:, d * Kw:(d + 1) * Kw]))
              + ab_ref[:, d * Kw:(d + 1) * Kw])
        log_a = (jnp.minimum(xa, 0.0) - jnp.log1p(jnp.exp(-jnp.abs(xa)))) * (1.0 / GLA_GATE_NORMALIZER)
        b = _cumsum_rows(incl, log_a)
        bl = jnp.sum(log_a, axis=0, keepdims=True)
        q = x_ref[:, 0:Kw] * (GLA_DK ** -0.5)
        k = x_ref[:, Kw:2 * Kw]
        q_in = (q * jnp.exp(b)).astype(BF16)
        k_in = (k * jnp.exp(-b)).astype(BF16)
        k_end = (k * jnp.exp(bl - b)).astype(BF16)
        gtot = jnp.exp(bl)
        for h in range(GLA_HEADS):
            ks = slice(h * GLA_DK, (h + 1) * GLA_DK)
            v_h = x_ref[:, 2 * Kw + h * GLA_DV:2 * Kw + (h + 1) * GLA_DV].astype(BF16)
            att = jnp.where(incl, _dg(q_in[:, ks], k_in[:, ks], _NT), 0.0).astype(BF16)
            work.append((d, h, _dg(att, v_h, _NN), q_in[:, ks], gtot[:, ks], _dg(v_h, k_end[:, ks], _TN)))

    s_old = [s_scr[d, h] for d, h, *_ in work]
    outs = [o_in + _dg(qh, s.astype(BF16), _NT) for (_, _, o_in, qh, _, _), s in zip(work, s_old)]
    stores = []
    for (d, h, _, _, g, kv), s, o in zip(work, s_old, outs):
        stores.append(((of_o, ob_o)[d], (slice(None), slice(h * GLA_DV, (h + 1) * GLA_DV)), o))
        stores.append((s_scr, (d, h), s * g + kv))
    return stores


def _scan_kernel(fw_ref, bw_ref, s0r_ref, xf, xb, a2_ref, ab_ref, s0g_ref,
                 yf_o, yb_o, str_o, of_o, ob_o, stg_o, sr_scr, sg_scr, *, nc):
    i = pl.program_id(1)

    @pl.when(i == 0)
    def _():
        sr_scr[...] = s0r_ref[...]
        sg_scr[...] = s0g_ref[...]

    stores = []
    for bb in range(SCAN_BATCH):
        stores += _rwkv_chunk(fw_ref.at[bb], bw_ref.at[bb], yf_o.at[bb], yb_o.at[bb], sr_scr.at[bb])
        stores += _gla_chunk(xf.at[bb], xb.at[bb], a2_ref, ab_ref, of_o.at[bb], ob_o.at[bb], sg_scr.at[bb])
    for ref, idx, val in stores:
        ref[idx] = val

    @pl.when(i == nc - 1)
    def _():
        str_o[...] = sr_scr[...]
        stg_o[...] = sg_scr[...]


def _scan_call(prep, s0, zg, s0t, a2, ab, B, T):
    fw, bw = prep
    N, Wd = fw.shape[0], RWKV_WIDTH
    Wg = zg.shape[1]
    nc = T // CHUNK
    BB = SCAN_BATCH
    H, Dh = RWKV_HEADS, RWKV_HEAD_DIM
    pair_shape = (2, H // 2, Dh, 2 * Dh)
    gla_shape = (2, GLA_HEADS, GLA_DV, GLA_DK)

    def rows(width, backward):
        if backward:
            return pl.BlockSpec((BB, CHUNK, width), lambda b, i: (b, nc - 1 - i, 0))
        return pl.BlockSpec((BB, CHUNK, width), lambda b, i: (b, i, 0))

    st_r = pl.BlockSpec((BB,) + pair_shape, lambda b, i: (b, 0, 0, 0, 0))
    st_g = pl.BlockSpec((BB,) + gla_shape, lambda b, i: (b, 0, 0, 0, 0))
    y = jax.ShapeDtypeStruct((B, T, Wd), F32)
    o = jax.ShapeDtypeStruct((B, T, GLA_V_WIDTH), F32)
    s0p = s0.reshape(B, 2, H // 2, 2, Dh, Dh).transpose(0, 1, 2, 4, 3, 5).reshape((B,) + pair_shape)
    zg3 = zg.reshape(B, T, Wg)
    yf, yb, stp, gf_out, gb_out, stg = pl.pallas_call(
        functools.partial(_scan_kernel, nc=nc),
        grid=(B // BB, nc),
        in_specs=[rows(SCAN_PACK, False), rows(SCAN_PACK, True), st_r,
                  rows(Wg, False), rows(Wg, True),
                  _const_spec((LANES, 2 * GLA_K_WIDTH)), _const_spec((1, 2 * GLA_K_WIDTH)), st_g],
        out_specs=[rows(Wd, False), rows(Wd, True), st_r,
                   rows(GLA_V_WIDTH, False), rows(GLA_V_WIDTH, True), st_g],
        out_shape=[y, y, jax.ShapeDtypeStruct((B,) + pair_shape, F32),
                   o, o, jax.ShapeDtypeStruct((B,) + gla_shape, F32)],
        scratch_shapes=[pltpu.VMEM((BB,) + pair_shape, F32), pltpu.VMEM((BB,) + gla_shape, F32)],
        compiler_params=_params(("parallel", "arbitrary")),
        name="scans",
    )(fw.reshape(B, T, SCAN_PACK), bw.reshape(B, T, SCAN_PACK), s0p, zg3, zg3, a2, ab, s0t)
    st_out = stp.reshape(B, 2, H // 2, Dh, 2, Dh).transpose(0, 1, 2, 4, 3, 5).reshape(B, 2, H, Dh, Dh)
    return (yf.reshape(N, Wd), yb.reshape(N, Wd), st_out,
            gf_out.reshape(N, GLA_V_WIDTH), gb_out.reshape(N, GLA_V_WIDTH), stg)


def _rope(x, cos, sin):
    n = x.shape[1]
    lane = lax.broadcasted_iota(jnp.int32, x.shape, 1)
    up = pltpu.roll(x, n - 16, 1)
    dn = pltpu.roll(x, 16, 1)
    sw = jnp.where((lane % 32) < 16, up, dn)
    reps = n // LANES
    if reps > 1:
        cos = jnp.concatenate([cos] * reps, axis=1)
        sin = jnp.concatenate([sin] * reps, axis=1)
    return x * cos + sw * sin


def _attn_prep_kernel(*refs, rope):
    if rope:
        z_ref, qn_ref, kn_ref, segq_ref, segk_ref, cos_ref, sin_ref, q_o, kn_o, kr_o, ve_o = refs
    else:
        z_ref, qn_ref, kn_ref, segq_ref, segk_ref, q_o, kn_o, kr_o, ve_o = refs
    zq = z_ref[:, 0:ATT_WIDTH]
    zk = z_ref[:, ATT_WIDTH:ATT_WIDTH + KV_WIDTH]
    qh = zq * lax.rsqrt(_seg_sum(zq * zq, segq_ref[...]) * (1.0 / HEAD_DIM) + NORM_EPS) * qn_ref[...]
    kh = zk * lax.rsqrt(_seg_sum(zk * zk, segk_ref[...]) * (1.0 / HEAD_DIM) + NORM_EPS) * kn_ref[...]
    kn_o[...] = kh
    if rope:
        qh = _rope(qh, cos_ref[...], sin_ref[...])
        kh = _rope(kh, cos_ref[...], sin_ref[...])
    q_o[...] = (qh * (LOG2_E * HEAD_DIM ** -0.5)).astype(BF16)
    kr_o[...] = kh.astype(BF16)
    v = z_ref[:, ATT_WIDTH + KV_WIDTH:ATT_WIDTH + 2 * KV_WIDTH]
    group = lax.broadcasted_iota(jnp.int32, v.shape, 1) // HEAD_DIM
    for g in range(KV_HEADS):
        ve_o[:, g * LANES:(g + 1) * LANES] = jnp.where(group == g, v, 1.0).astype(BF16)


def _attn_prep_call(za, lw, seq_len, rope_tabs):
    N, Wz = za.shape
    tm = ROW_TILE
    bps = seq_len // tm
    rope = rope_tabs is not None
    in_specs = [pl.BlockSpec((tm, Wz), lambda i: (i, 0)),
                _const_spec((1, ATT_WIDTH)), _const_spec((1, KV_WIDTH)),
                _const_spec((ATT_WIDTH, ATT_WIDTH)), _const_spec((KV_WIDTH, KV_WIDTH))]
    args = [za, lw["qn"], lw["kn"], lw["seg64"], lw["seg64"][:KV_WIDTH, :KV_WIDTH]]
    if rope:
        in_specs += [pl.BlockSpec((tm, LANES), lambda i: (i % bps, 0))] * 2
        args += list(rope_tabs)
    return pl.pallas_call(
        functools.partial(_attn_prep_kernel, rope=rope),
        grid=(N // tm,),
        in_specs=in_specs,
        out_specs=[pl.BlockSpec((tm, ATT_WIDTH), lambda i: (i, 0)),
                   pl.BlockSpec((tm, KV_WIDTH), lambda i: (i, 0)),
                   pl.BlockSpec((tm, KV_WIDTH), lambda i: (i, 0)),
                   pl.BlockSpec((tm, KV_HEADS * LANES), lambda i: (i, 0))],
        out_shape=[jax.ShapeDtypeStruct((N, ATT_WIDTH), BF16),
                   jax.ShapeDtypeStruct((N, KV_WIDTH), F32),
                   jax.ShapeDtypeStruct((N, KV_WIDTH), BF16),
                   jax.ShapeDtypeStruct((N, KV_HEADS * LANES), BF16)],
        compiler_params=_params(("parallel",)),
        name="attn_prep",
    )(*args)


def _attn_kernel(*refs, cached):
    if cached:
        q_ref, k_ref, v_ref, ck_ref, cv_ref, o_ref = refs
    else:
        q_ref, k_ref, v_ref, o_ref = refs
    G = N_HEADS // KV_HEADS
    tq = q_ref.shape[0]
    for g in range(KV_HEADS):
        gs = slice(g * HEAD_DIM, (g + 1) * HEAD_DIM)
        other = slice((1 - g) * HEAD_DIM, (2 - g) * HEAD_DIM)
        kg = k_ref[:, gs]
        vg = v_ref[:, g * LANES:(g + 1) * LANES]
        if cached:
            cv = cv_ref[...]
            own = (lax.broadcasted_iota(jnp.int32, cv.shape, 1) // HEAD_DIM) == g
            ckg = ck_ref[:, gs].astype(BF16)
            cvg = jnp.where(own, cv, 1.0).astype(BF16)
        for h0 in range(g * G, (g + 1) * G, ATT_STACK):
            heads = range(h0, h0 + ATT_STACK)
            qs = jnp.concatenate([q_ref[:, h * HEAD_DIM:(h + 1) * HEAD_DIM] for h in heads], axis=0)
            chunks = [(kg[c:c + KEY_CHUNK], vg[c:c + KEY_CHUNK]) for c in range(0, kg.shape[0], KEY_CHUNK)]
            if cached:
                chunks.append((ckg, cvg))
            m = None
            for kc, _ in chunks:
                mc = jnp.max(_dg(qs, kc, _NT), axis=-1, keepdims=True)
                m = mc if m is None else jnp.maximum(m, mc)
            oe = None
            for kc, vc in chunks:
                part = _dg(jnp.exp2(_dg(qs, kc, _NT) - m).astype(BF16), vc, _NN)
                oe = part if oe is None else oe + part
            o = oe[:, gs] / oe[:, other]
            for j, h in enumerate(heads):
                o_ref[:, h * HEAD_DIM:(h + 1) * HEAD_DIM] = o[j * tq:(j + 1) * tq]


def _attn_call(q, k, vext, cache, B, T):
    N = q.shape[0]
    tq = Q_TILE
    nq = T // tq
    in_specs = [pl.BlockSpec((tq, ATT_WIDTH), lambda b, i: (b * nq + i, 0)),
                pl.BlockSpec((T, KV_WIDTH), lambda b, i: (b, 0)),
                pl.BlockSpec((T, KV_HEADS * LANES), lambda b, i: (b, 0))]
    args = [q, k, vext]
    if cache is not None:
        ck, cv, layer = cache
        P = ck.shape[2]
        cspec = pl.BlockSpec((None, None, P, KV_WIDTH), lambda b, i: (b, layer, 0, 0))
        in_specs += [cspec, cspec]
        args += [ck, cv]
    return pl.pallas_call(
        functools.partial(_attn_kernel, cached=cache is not None),
        grid=(B, nq),
        in_specs=in_specs,
        out_specs=pl.BlockSpec((tq, ATT_WIDTH), lambda b, i: (b * nq + i, 0)),
        out_shape=jax.ShapeDtypeStruct((N, ATT_WIDTH), F32),
        compiler_params=_params(("parallel", "arbitrary")),
        name="attention",
    )(*args)


def _merge_kernel(yf, yb, bon, g, oa, gf, gb, gr, mr, ma, mg, x_ref, m_ref,
                  lng, lnb, gn, nf, seg_ref, pr, pa, pg, wo, rw, rb,
                  x1_o, h2_o, ti_o, tg_o):
    seg = seg_ref[...]
    inv = 1.0 / RWKV_HEAD_DIM
    y = yf[...] + yb[...]
    mu = _seg_sum(y, seg) * inv
    yc = y - mu
    var = _seg_sum(yc * yc, seg) * inv
    o_r = (yc * lax.rsqrt(var + RWKV_GN_EPS) * lng[...] + lnb[...] + bon[...]) * g[...]

    gate_r = gr[...]
    silu_r = gate_r * _sigmoid(gate_r)
    gnv = gn[...]
    cols = []
    for h in range(GLA_HEADS):
        vs = slice(h * GLA_DV, (h + 1) * GLA_DV)
        o = gf[:, vs] + gb[:, vs]
        ms = jnp.mean(o * o, axis=-1, keepdims=True)
        cols.append(o * lax.rsqrt(ms + NORM_EPS) * gnv * silu_r[:, vs])
    o_g = jnp.concatenate(cols, axis=1)

    merged = (_sigmoid(mr[...]) * _dot(o_r.astype(BF16), pr[...])
              + _sigmoid(ma[...]) * _dot(oa[...].astype(BF16), pa[...])
              + _sigmoid(mg[...]) * _dot(o_g.astype(BF16), pg[...]))
    x1 = x_ref[...] + m_ref[2:3, :] * _dot(merged.astype(BF16), wo[...])
    x1_o[...] = x1
    hn = x1 * lax.rsqrt(jnp.mean(x1 * x1, axis=-1, keepdims=True) + NORM_EPS) * nf[...]
    h2 = hn * (1.0 + m_ref[4:5, :]) + m_ref[3:4, :]
    h2_o[...] = h2

    logits = _mm3(_split(h2), _split(rw[...])) + rb[...]
    lane = lax.broadcasted_iota(jnp.int32, logits.shape, 1)
    vals, idxs = [], []
    for _ in range(TOP_K):
        m = jnp.max(logits, axis=-1, keepdims=True)
        idx = jnp.min(jnp.where(logits == m, lane, LANES), axis=-1, keepdims=True)
        vals.append(m)
        idxs.append(idx)
        logits = jnp.where(lane == idx, -jnp.inf, logits)
    es = [jnp.exp(vv - vals[0]) for vv in vals]
    den = es[0] + es[1] + es[2] + es[3]
    ti = jnp.zeros(logits.shape, jnp.int32)
    tg = jnp.zeros(logits.shape, F32)
    for j in range(TOP_K):
        ti = jnp.where(lane == j, idxs[j], ti)
        tg = jnp.where(lane == j, es[j] / den, tg)
    ti_o[...] = ti
    tg_o[...] = tg


def _merge_call(yf, yb, bon, g, oa, gf, gb, zg, zm, x, mod, lw, rows_per_cond):
    N, D = x.shape
    tm = ROW_TILE
    Wd = RWKV_WIDTH

    def rows(w, col=0):
        return pl.BlockSpec((tm, w), lambda i: (i, col))

    in_specs = [rows(Wd)] * 4 + [rows(ATT_WIDTH), rows(GLA_V_WIDTH), rows(GLA_V_WIDTH),
                                 rows(GLA_V_WIDTH, 2), rows(D, 0), rows(D, 1), rows(D, 2), rows(D),
                                 pl.BlockSpec((None, 6, D), lambda i: ((i * tm) // rows_per_cond, 0, 0)),
                                 _const_spec((1, Wd)), _const_spec((1, Wd)), _const_spec((1, GLA_DV)),
                                 _const_spec((1, D)), _const_spec((Wd, Wd)),
                                 _const_spec((Wd, D)), _const_spec((ATT_WIDTH, D)), _const_spec((GLA_V_WIDTH, D)),
                                 _const_spec((D, D)), _const_spec((D, LANES)), _const_spec((1, LANES))]
    return pl.pallas_call(
        _merge_kernel,
        grid=(N // tm,),
        in_specs=in_specs,
        out_specs=[rows(D), rows(D), rows(LANES), rows(LANES)],
        out_shape=[jax.ShapeDtypeStruct((N, D), F32), jax.ShapeDtypeStruct((N, D), F32),
                   jax.ShapeDtypeStruct((N, LANES), jnp.int32), jax.ShapeDtypeStruct((N, LANES), F32)],
        compiler_params=_params(("parallel",)),
        name="merge",
    )(yf, yb, bon, g, oa, gf, gb, zg, zm, zm, zm, x, mod,
      lw["ln_g"], lw["ln_b"], lw["gla_norm"], lw["norm_ffn"], lw["seg64"],
      lw["p_rwkv"], lw["p_attn"], lw["p_gla"], lw["w_out"], lw["router_w"], lw["router_b"])


def _moe_kernel(be_ref, na_ref, x_ref, gate_ref, w1_ref, b1_ref, w2_ref, b2_ref, o_ref, w1_scr, w2_scr):
    i = pl.program_id(0)

    @pl.when(jnp.logical_or(i == 0, be_ref[i] != be_ref[jnp.maximum(i - 1, 0)]))
    def _():
        w1_scr[...] = w1_ref[...].astype(BF16)
        w2_scr[...] = w2_ref[...].astype(BF16)

    @pl.when(i < na_ref[0])
    def _():
        z = _dot(x_ref[...].astype(BF16), w1_scr[...]) + b1_ref[...]
        glu = jnp.minimum(z[:, :D_EXPERT], SWIGLU_LIMIT)
        lin = jnp.clip(z[:, D_EXPERT:], -SWIGLU_LIMIT, SWIGLU_LIMIT)
        act = glu * _sigmoid(SWIGLU_ALPHA * glu) * (lin + 1.0)
        o_ref[...] = (_dot(act.astype(BF16), w2_scr[...]) + b2_ref[...]) * gate_ref[...]

    @pl.when(i >= na_ref[0])
    def _():
        o_ref[...] = jnp.zeros_like(o_ref)


def _moe_call(xg, row_gate, block_exp, n_active, layer, w1, b1, w2, b2):
    R, D = xg.shape
    tb = EXPERT_BLOCK
    F2 = w1.shape[3]
    grid_spec = pltpu.PrefetchScalarGridSpec(
        num_scalar_prefetch=2,
        grid=(R // tb,),
        in_specs=[pl.BlockSpec((tb, D), lambda i, be, na: (i, 0)),
                  pl.BlockSpec((tb, 1), lambda i, be, na: (i, 0)),
                  pl.BlockSpec((None, None, D, F2), lambda i, be, na: (layer, be[i], 0, 0)),
                  pl.BlockSpec((None, 1, F2), lambda i, be, na: (be[i], 0, 0)),
                  pl.BlockSpec((None, None, F2 // 2, D), lambda i, be, na: (layer, be[i], 0, 0)),
                  pl.BlockSpec((None, 1, D), lambda i, be, na: (be[i], 0, 0))],
        out_specs=pl.BlockSpec((tb, D), lambda i, be, na: (i, 0)),
        scratch_shapes=[pltpu.VMEM((D, F2), BF16), pltpu.VMEM((F2 // 2, D), BF16)],
    )
    return pl.pallas_call(
        _moe_kernel,
        grid_spec=grid_spec,
        out_shape=jax.ShapeDtypeStruct((R, D), F32),
        compiler_params=_params(("arbitrary",)),
        name="moe_experts",
    )(block_exp, n_active, xg, row_gate, w1, b1, w2, b2)


def _combine_kernel(x1_ref, m_ref, y0_ref, y1_ref, y2_ref, y3_ref, o_ref):
    y = (y0_ref[...] + y1_ref[...]) + (y2_ref[...] + y3_ref[...])
    o_ref[...] = x1_ref[...] + m_ref[5:6, :] * y


def _combine_call(x1, mod, ys, rows_per_cond):
    N, D = x1.shape
    tm = ROW_TILE
    rows = pl.BlockSpec((tm, D), lambda i: (i, 0))
    return pl.pallas_call(
        _combine_kernel,
        grid=(N // tm,),
        in_specs=([rows, pl.BlockSpec((None, 6, D), lambda i: ((i * tm) // rows_per_cond, 0, 0))]
                  + [pl.BlockSpec((None, tm, D), functools.partial(lambda i, j: (j, i, 0), j=j))
                     for j in range(TOP_K)]),
        out_specs=rows,
        out_shape=jax.ShapeDtypeStruct((N, D), F32),
        compiler_params=_params(("parallel",)),
        name="combine",
    )(x1, mod, *([ys] * TOP_K))


def _moe_ffn(h2, top_i, top_g, lw):
    N, D = h2.shape
    n_as = N * TOP_K
    n_blocks = -(-n_as // EXPERT_BLOCK) + N_EXPERTS
    R = n_blocks * EXPERT_BLOCK
    ex = jnp.arange(N_EXPERTS, dtype=jnp.int32)

    def lut(onehot, table):
        return jnp.sum(jnp.where(onehot, table[None, :], 0), axis=1)

    flat_e = top_i.reshape(n_as)
    flat_g = top_g.reshape(n_as)
    oh_a = flat_e[:, None] == ex[None, :]
    counts = jnp.sum(oh_a, axis=0, dtype=jnp.int32)
    starts = jnp.cumsum(counts) - counts
    padded = (counts + EXPERT_BLOCK - 1) // EXPERT_BLOCK * EXPERT_BLOCK
    pends = jnp.cumsum(padded)
    pstarts = pends - padded
    order = jnp.argsort(flat_e).astype(jnp.int32)
    rank = jnp.argsort(order).astype(jnp.int32)
    pos = rank + lut(oh_a, pstarts - starts)
    blk_start = jnp.arange(n_blocks, dtype=jnp.int32) * EXPERT_BLOCK
    block_exp = jnp.minimum(jnp.sum(pends[None, :] <= blk_start[:, None], axis=1),
                            N_EXPERTS - 1).astype(jnp.int32)
    n_active = (pends[-1] // EXPERT_BLOCK).astype(jnp.int32).reshape(1)
    oh_p = jnp.repeat(block_exp, EXPERT_BLOCK)[:, None] == ex[None, :]
    idx = jnp.arange(R, dtype=jnp.int32) - lut(oh_p, pstarts)
    valid = (idx >= 0) & (idx < lut(oh_p, counts))
    asg = order[jnp.clip(lut(oh_p, starts) + idx, 0, n_as - 1)]
    row_tok = jnp.where(valid, asg // TOP_K, 0)
    row_gate = jnp.where(valid, flat_g[asg], 0.0)
    out = _moe_call(h2[row_tok], row_gate.reshape(R, 1), block_exp, n_active,
                    lw["layer"], lw["moe_w1"], lw["moe_b1"], lw["moe_w2"], lw["moe_b2"])
    return out[pos.reshape(N, TOP_K).T.reshape(n_as)].reshape(TOP_K, N, D)


def _seg_matrix(n, seg):
    idx = jnp.arange(n) // seg
    return (idx[:, None] == idx[None, :]).astype(BF16)


def _rope_tables(T):
    quarter = HEAD_DIM // 4
    inv_freq = ROPE_THETA ** (-jnp.arange(quarter, dtype=F32) / quarter)
    t = jnp.arange(T)
    row = (t // GRID_W).astype(F32)
    col = (t % GRID_W).astype(F32)
    ang_r = row[:, None] * inv_freq
    ang_c = col[:, None] * inv_freq
    cos = jnp.concatenate([jnp.cos(ang_r)] * 2 + [jnp.cos(ang_c)] * 2, axis=1)
    sin = jnp.concatenate([-jnp.sin(ang_r), jnp.sin(ang_r), -jnp.sin(ang_c), jnp.sin(ang_c)], axis=1)
    return jnp.concatenate([cos, cos], axis=1), jnp.concatenate([sin, sin], axis=1)


def _layer_weights(p, l):
    Wd = RWKV_WIDTH
    w_in = p["w_in"][l]
    c0 = RWKV_COLS
    aq, ak, av, gq, gk, gv, gl, gr = (c0, c0 + 512, c0 + 640, c0 + 768, c0 + 1280, c0 + 1792, c0 + 2816, c0 + 2848)
    m0 = gr + GLA_V_WIDTH
    D = D_MODEL
    w_g = jnp.concatenate([w_in[:, gq:gl], w_in[:, gr:m0], w_in[:, gl:gr],
                           jnp.zeros((D, LANES - 2 * GLA_LORA), F32)], axis=1)
    z64 = jnp.zeros((DECAY_LORA, Wd), F32)
    w2 = jnp.concatenate([jnp.concatenate([p["rwkv_w2"][l, 0], z64], axis=1),
                          jnp.concatenate([z64, p["rwkv_w2"][l, 1]], axis=1)], axis=0)
    a2 = jnp.concatenate([jnp.concatenate([p["rwkv_a2"][l, 0], z64], axis=1),
                          jnp.concatenate([z64, p["rwkv_a2"][l, 1]], axis=1)], axis=0)
    z16 = jnp.zeros((GLA_LORA, GLA_K_WIDTH), F32)
    ga2 = jnp.concatenate([jnp.concatenate([p["gla_a2"][l, 0], z16], axis=1),
                           jnp.concatenate([z16, p["gla_a2"][l, 1]], axis=1),
                           jnp.zeros((LANES - 2 * GLA_LORA, 2 * GLA_K_WIDTH), F32)], axis=0)
    return {
        "w_r": w_in[:, :c0].astype(BF16), "w_a": w_in[:, aq:gq].astype(BF16),
        "w_g": w_g.astype(BF16), "w_m": w_in[:, m0:].astype(BF16),
        "norm_mix": p["norm_mix"][l].reshape(1, D), "norm_ffn": p["norm_ffn"][l].reshape(1, D),
        "mu": p["rwkv_mu"][l].reshape(1, c0),
        "kk": p["rwkv_kk"][l].reshape(1, Wd), "ka": p["rwkv_ka"][l].reshape(1, Wd),
        "rk": p["rwkv_rk"][l].reshape(1, Wd),
        "w0": p["rwkv_w0"][l].reshape(1, 2 * Wd), "a0": p["rwkv_a0"][l].reshape(1, 2 * Wd),
        "w2": w2, "a2": a2, "g2": p["rwkv_g2"][l],
        "ln_g": p["rwkv_ln_g"][l].reshape(1, Wd), "ln_b": p["rwkv_ln_b"][l].reshape(1, Wd),
        "seg64": _seg_matrix(Wd, RWKV_HEAD_DIM),
        "qn": jnp.tile(p["attn_qn"][l], N_HEADS).reshape(1, ATT_WIDTH),
        "kn": jnp.tile(p["attn_kn"][l], KV_HEADS).reshape(1, KV_WIDTH),
        "gla_a2": ga2, "gla_ab": p["gla_ab"][l].reshape(1, 2 * GLA_K_WIDTH),
        "gla_norm": p["gla_norm"][l].reshape(1, GLA_DV),
        "p_rwkv": p["p_rwkv"][l].astype(BF16), "p_attn": p["p_attn"][l].astype(BF16),
        "p_gla": p["p_gla"][l].astype(BF16), "w_out": p["w_out"][l].astype(BF16),
        "router_w": jnp.concatenate([p["router_w"][l], jnp.zeros((D, LANES - N_EXPERTS), F32)], axis=1),
        "router_b": jnp.concatenate([p["router_b"][l], jnp.full((LANES - N_EXPERTS,), -jnp.inf, F32)]).reshape(1, LANES),
        "layer": l, "moe_w1": p["moe_w1"], "moe_b1": p["moe_b1"][l].reshape(N_EXPERTS, 1, 2 * D_EXPERT),
        "moe_w2": p["moe_w2"], "moe_b2": p["moe_b2"][l].reshape(N_EXPERTS, 1, D_MODEL),
    }


def _trunk_layer(x, mod, lw, B, T, ctx):
    N = B * T
    rpc = N if mod.shape[0] == 1 else T
    zr, za, zg, zm = (_inproj_call(x, mod, lw["norm_mix"], lw[name], rpc, lw[name].shape[1])
                      for name in ("w_r", "w_a", "w_g", "w_m"))

    fw, bw, g, bon = _rwkv_prep_call(zr, T, lw)
    if ctx is None:
        s0r = jnp.zeros((B, 2, RWKV_HEADS, RWKV_HEAD_DIM, RWKV_HEAD_DIM), F32)
        s0g = jnp.zeros((B, 2, GLA_HEADS, GLA_DV, GLA_DK), F32)
        cache, tabs = None, None
    else:
        s0r = ctx["rwkv"]
        s0g = jnp.swapaxes(ctx["gla"], -1, -2)
        cache, tabs = (ctx["k"], ctx["v"], ctx["layer"]), _rope_tables(T)
    yf, yb, s_r, gf, gb, s_g = _scan_call((fw, bw), s0r, zg, s0g, lw["gla_a2"], lw["gla_ab"], B, T)
    q, kn, kr, vext = _attn_prep_call(za, lw, T, tabs)
    oa = _attn_call(q, kr, vext, cache, B, T)
    x1, h2, ti, tg = _merge_call(yf, yb, bon, g, oa, gf, gb, zg, zm, x, mod, lw, rpc)
    y4 = _moe_ffn(h2, ti[:, :TOP_K], tg[:, :TOP_K], lw)
    x2 = _combine_call(x1, mod, y4, rpc)
    return x2, (kn, za[:, ATT_WIDTH + KV_WIDTH:], s_r, jnp.swapaxes(s_g, -1, -2))


def kernel(x_prompt, x_sample, c, cache_k, cache_v, state_rwkv, state_gla, c_ctx, ada_w, ada_b, norm_mix, norm_ffn, w_in, rwkv_mu, rwkv_w0, rwkv_w2, rwkv_a0, rwkv_a2, rwkv_g2, rwkv_kk, rwkv_ka, rwkv_rk, rwkv_ln_g, rwkv_ln_b, attn_qn, attn_kn, gla_a2, gla_ab, gla_norm, p_rwkv, p_attn, p_gla, w_out, router_w, router_b, moe_w1, moe_b1, moe_w2, moe_b2):
    p = {"w_in": w_in, "norm_mix": norm_mix, "norm_ffn": norm_ffn, "rwkv_mu": rwkv_mu, "rwkv_w0": rwkv_w0,
         "rwkv_w2": rwkv_w2, "rwkv_a0": rwkv_a0, "rwkv_a2": rwkv_a2, "rwkv_g2": rwkv_g2, "rwkv_kk": rwkv_kk,
         "rwkv_ka": rwkv_ka, "rwkv_rk": rwkv_rk, "rwkv_ln_g": rwkv_ln_g, "rwkv_ln_b": rwkv_ln_b,
         "attn_qn": attn_qn, "attn_kn": attn_kn, "gla_a2": gla_a2, "gla_ab": gla_ab, "gla_norm": gla_norm,
         "p_rwkv": p_rwkv, "p_attn": p_attn, "p_gla": p_gla, "w_out": w_out, "router_w": router_w,
         "router_b": router_b, "moe_w1": moe_w1, "moe_b1": moe_b1, "moe_w2": moe_w2, "moe_b2": moe_b2}
    Bp, Tp, D = x_prompt.shape
    Bs, Ts, _ = x_sample.shape
    L = ada_w.shape[0]
    P = cache_k.shape[2]

    cond8 = jnp.concatenate([c_ctx[None, :], c, jnp.zeros((8 - 1 - Bs, D), F32)], axis=0)
    mod = _ada_call(cond8, ada_w, ada_b).reshape(L, 8, 6, D)
    weights = [_layer_weights(p, l) for l in range(L)]
    ck = cache_k.reshape(Bs, L, P, KV_WIDTH)
    cv = cache_v.reshape(Bs, L, P, KV_WIDTH)

    yp = x_prompt.reshape(Bp * Tp, D)
    ks, vs, srs, sgs = [], [], [], []
    for l in range(L):
        yp, (kn, vh, s_r, s_g) = _trunk_layer(yp, mod[l, 0:1], weights[l], Bp, Tp, None)
        ks.append(kn.reshape(Bp, Tp, KV_HEADS, HEAD_DIM))
        vs.append(vh.reshape(Bp, Tp, KV_HEADS, HEAD_DIM))
        srs.append(s_r)
        sgs.append(s_g)

    ys = x_sample.reshape(Bs * Ts, D)
    for l in range(L):
        ctx = {"k": ck, "v": cv, "layer": l, "rwkv": state_rwkv[:, l], "gla": state_gla[:, l]}
        ys, _ = _trunk_layer(ys, mod[l, 1:1 + Bs], weights[l], Bs, Ts, ctx)

    return (yp.reshape(Bp, Tp, D), ys.reshape(Bs, Ts, D),
            jnp.stack(ks, axis=1), jnp.stack(vs, axis=1),
            jnp.stack(srs, axis=1), jnp.stack(sgs, axis=1))
```

```python
import functools

import jax
import jax.numpy as jnp
from jax import lax
from jax.experimental import pallas as pl
from jax.experimental.pallas import tpu as pltpu

D_MODEL = 1024
DEPTH = 2
GRID_W = 64
NORM_EPS = 1e-6

RWKV_HEADS = 8
RWKV_HEAD_DIM = 64
RWKV_WIDTH = RWKV_HEADS * RWKV_HEAD_DIM
DECAY_LORA = 64
ICLR_LORA = 64
GATE_LORA = 128
RWKV_GN_EPS = 6.4e-4
RWKV_COLS = 3 * RWKV_WIDTH + 2 * DECAY_LORA + 2 * ICLR_LORA + GATE_LORA
SCAN_PACK = 6 * RWKV_WIDTH

N_HEADS = 8
KV_HEADS = 2
HEAD_DIM = 64
ATT_WIDTH = N_HEADS * HEAD_DIM
KV_WIDTH = KV_HEADS * HEAD_DIM
ROPE_THETA = 10000.0

GLA_HEADS = 4
GLA_DK = 128
GLA_DV = 256
GLA_K_WIDTH = GLA_HEADS * GLA_DK
GLA_V_WIDTH = GLA_HEADS * GLA_DV
GLA_LORA = 16
GLA_GATE_NORMALIZER = 16.0

N_EXPERTS = 32
TOP_K = 4
D_EXPERT = 1024
SWIGLU_ALPHA = 1.702
SWIGLU_LIMIT = 7.0
EXPERT_BLOCK = 512

LANES = 128
CHUNK = 64
SCAN_BATCH = 2
ROW_TILE = 256
INPROJ_TILE = 512
Q_TILE = 128
ATT_STACK = 2
KEY_CHUNK = 512
VMEM_LIMIT = 52 * 1024 * 1024
LOG2_E = 1.4426950408889634

F32 = jnp.float32
BF16 = jnp.bfloat16
HI = lax.Precision.HIGHEST


def _dot(a, b, prec=None):
    return jnp.dot(a, b, preferred_element_type=F32, precision=prec)


def _dot_nt(a, b, prec=None):
    return lax.dot_general(a, b, (((1,), (1,)), ((), ())), preferred_element_type=F32, precision=prec)


def _dot_tn(a, b, prec=None):
    return lax.dot_general(a, b, (((0,), (0,)), ((), ())), preferred_element_type=F32, precision=prec)


_NN = (((1,), (0,)), ((), ()))
_NT = (((1,), (1,)), ((), ()))
_TN = (((0,), (0,)), ((), ()))


def _dg(a, b, dn):
    return lax.dot_general(a, b, dn, preferred_element_type=F32)


def _split(x):
    h = x.astype(BF16)
    return h, (x - h.astype(F32)).astype(BF16)


def _mm3(a, b, dn=_NN):
    return _dg(a[0], b[0], dn) + _dg(a[0], b[1], dn) + _dg(a[1], b[0], dn)


def _mm3s(a, b, dn=_NN):
    m = a[0].shape[0]
    top = _dg(jnp.concatenate([a[0], a[1]], axis=0), b[0], dn)
    return top[:m] + top[m:] + _dg(a[0], b[1], dn)


def _cumsum_rows(mask, x):
    m = jnp.where(mask, 1.0, 0.0).astype(BF16)
    h = x.astype(BF16)
    r1 = x - h.astype(F32)
    mid = r1.astype(BF16)
    lo = (r1 - mid.astype(F32)).astype(BF16)
    return _dg(m, h, _NN) + _dg(m, mid, _NN) + _dg(m, lo, _NN)


def _seg_sum(x, seg_bf16):
    xh = x.astype(BF16)
    xl = (x - xh.astype(F32)).astype(BF16)
    return _dot(xh, seg_bf16) + _dot(xl, seg_bf16)


def _sigmoid(x):
    return 1.0 / (1.0 + jnp.exp(-x))


def _params(sem):
    return pltpu.CompilerParams(dimension_semantics=sem, vmem_limit_bytes=VMEM_LIMIT)


def _const_spec(shape):
    nd = len(shape)
    return pl.BlockSpec(shape, lambda *_: (0,) * nd)


def _ada_kernel(c_ref, w_ref, b_ref, o_ref):
    c = c_ref[...]
    s = c * _sigmoid(c)
    o_ref[...] = _mm3(_split(s), _split(w_ref[...])) + b_ref[...]


def _ada_call(cond8, ada_w, ada_b):
    L, D, W = ada_w.shape
    tn = 1536
    return pl.pallas_call(
        _ada_kernel,
        grid=(L, W // tn),
        in_specs=[pl.BlockSpec((8, D), lambda l, j: (0, 0)),
                  pl.BlockSpec((None, D, tn), lambda l, j: (l, 0, j)),
                  pl.BlockSpec((None, 1, tn), lambda l, j: (l, 0, j))],
        out_specs=pl.BlockSpec((None, 8, tn), lambda l, j: (l, 0, j)),
        out_shape=jax.ShapeDtypeStruct((L, 8, W), F32),
        compiler_params=_params(("parallel", "parallel")),
        name="ada",
    )(cond8, ada_w, ada_b.reshape(L, 1, W))


def _inproj_kernel(x_ref, m_ref, g_ref, w_ref, o_ref, h_scr):
    @pl.when(pl.program_id(1) == 0)
    def _():
        x = x_ref[...]
        y = x * lax.rsqrt(jnp.mean(x * x, axis=-1, keepdims=True) + NORM_EPS) * g_ref[...]
        h = y * (1.0 + m_ref[1:2, :]) + m_ref[0:1, :]
        h_scr[...] = h.astype(BF16)

    o_ref[...] = _dot(h_scr[...], w_ref[...])


def _inproj_call(x, mod, gain, w, rows_per_cond, tn):
    N, D = x.shape
    W = w.shape[1]
    tm = INPROJ_TILE
    assert N % tm == 0 and rows_per_cond % tm == 0, (N, rows_per_cond, tm)
    return pl.pallas_call(
        _inproj_kernel,
        grid=(N // tm, W // tn),
        in_specs=[pl.BlockSpec((tm, D), lambda i, j: (i, 0)),
                  pl.BlockSpec((None, 6, D), lambda i, j: ((i * tm) // rows_per_cond, 0, 0)),
                  pl.BlockSpec((1, D), lambda i, j: (0, 0)),
                  pl.BlockSpec((D, tn), lambda i, j: (0, j))],
        out_specs=pl.BlockSpec((tm, tn), lambda i, j: (i, j)),
        out_shape=jax.ShapeDtypeStruct((N, W), F32),
        scratch_shapes=[pltpu.VMEM((tm, D), BF16)],
        compiler_params=_params(("parallel", "arbitrary")),
        name="inproj",
    )(x, mod, gain, w)


def _rwkv_prep_kernel(z_ref, zp_ref, zn_ref, mu_ref, kkp_ref, ka_ref, rk_ref, w0_ref, a0_ref,
                      w2_ref, a2_ref, g2_ref, seg_ref,
                      fw_o, bw_o, g_o, bon_o, *, tm, blocks_per_seq):
    i = pl.program_id(0)
    z = z_ref[...]
    row = lax.broadcasted_iota(jnp.int32, (tm, 1), 0)
    first = (i % blocks_per_seq) == 0
    last = (i % blocks_per_seq) == blocks_per_seq - 1
    pz = jnp.where(first, 0.0, zp_ref[7:8, :])
    nz = jnp.where(last, 0.0, zn_ref[0:1, :])
    prev = jnp.where(row == 0, pz, pltpu.roll(z, 1, 0))
    nxt = jnp.where(row == tm - 1, nz, pltpu.roll(z, tm - 1, 0))
    zm = z + mu_ref[...] * (0.5 * (prev + nxt) - z)

    Wd = RWKV_WIDTH
    r = zm[:, 0:Wd]
    k = zm[:, Wd:2 * Wd]
    v = zm[:, 2 * Wd:3 * Wd]
    lw = zm[:, 3 * Wd:3 * Wd + 128]
    la = zm[:, 3 * Wd + 128:3 * Wd + 256]
    lg = zm[:, 3 * Wd + 256:3 * Wd + 384]
    seg = seg_ref[...]

    kk = k * kkp_ref[...]
    kk = kk / jnp.maximum(jnp.sqrt(_seg_sum(kk * kk, seg)), 1e-12)
    w_raw = _mm3(_split(jnp.tanh(lw)), _split(w2_ref[...])) + w0_ref[...]
    logw = -_sigmoid(w_raw) * 0.6065306597126334
    a = _sigmoid(_mm3(_split(la), _split(a2_ref[...])) + a0_ref[...])
    ka = ka_ref[...]
    rk = rk_ref[...]
    bonus = jnp.zeros_like(r)
    for d, dir_o in enumerate((fw_o, bw_o)):
        a_d = a[:, d * Wd:(d + 1) * Wd]
        k_d = k * (1.0 + (a_d - 1.0) * ka)
        dir_o[:, 0:Wd] = r
        dir_o[:, Wd:2 * Wd] = v
        dir_o[:, 2 * Wd:3 * Wd] = kk
        dir_o[:, 3 * Wd:4 * Wd] = logw[:, d * Wd:(d + 1) * Wd]
        dir_o[:, 4 * Wd:5 * Wd] = k_d
        dir_o[:, 5 * Wd:6 * Wd] = kk * a_d
        bonus = bonus + _seg_sum(r * k_d * rk, seg) * v
    bon_o[...] = bonus
    g_o[...] = _mm3(_split(_sigmoid(lg)), _split(g2_ref[...]))


def _rwkv_prep_call(zr, seq_len, lw):
    N, Wz = zr.shape
    tm = ROW_TILE
    bps = seq_len // tm
    Wd = RWKV_WIDTH
    nb8 = N // 8
    row_spec = pl.BlockSpec((tm, Wd), lambda i: (i, 0))
    out = jax.ShapeDtypeStruct((N, Wd), F32)
    return pl.pallas_call(
        functools.partial(_rwkv_prep_kernel, tm=tm, blocks_per_seq=bps),
        grid=(N // tm,),
        in_specs=[pl.BlockSpec((tm, Wz), lambda i: (i, 0)),
                  pl.BlockSpec((8, Wz), lambda i: (jnp.maximum(i * (tm // 8) - 1, 0), 0)),
                  pl.BlockSpec((8, Wz), lambda i: (jnp.minimum((i + 1) * (tm // 8), nb8 - 1), 0)),
                  _const_spec((1, Wz)), _const_spec((1, Wd)), _const_spec((1, Wd)), _const_spec((1, Wd)),
                  _const_spec((1, 2 * Wd)), _const_spec((1, 2 * Wd)),
                  _const_spec((128, 2 * Wd)), _const_spec((128, 2 * Wd)), _const_spec((128, Wd)),
                  _const_spec((Wd, Wd))],
        out_specs=[pl.BlockSpec((tm, SCAN_PACK), lambda i: (i, 0))] * 2 + [row_spec] * 2,
        out_shape=[jax.ShapeDtypeStruct((N, SCAN_PACK), F32)] * 2 + [out] * 2,
        compiler_params=_params(("parallel",)),
        name="rwkv_prep",
    )(zr, zr, zr, lw["mu"], lw["kk"], lw["ka"], lw["rk"], lw["w0"], lw["a0"],
      lw["w2"], lw["a2"], lw["g2"], lw["seg64"])


def _tri_masks(d):
    ti = lax.broadcasted_iota(jnp.int32, (CHUNK, CHUNK), 0)
    si = lax.broadcasted_iota(jnp.int32, (CHUNK, CHUNK), 1)
    if d == 0:
        return si <= ti, si < ti
    return si >= ti, si > ti


def _rwkv_chunks(seqs):
    C = CHUNK
    NP = RWKV_HEADS // 2
    Wd = RWKV_WIDTH
    sls = [slice(q * LANES, (q + 1) * LANES) for q in range(NP)]
    ti = lax.broadcasted_iota(jnp.int32, (C, LANES), 0)
    si = lax.broadcasted_iota(jnp.int32, (C, LANES), 1) % RWKV_HEAD_DIM
    low = lax.broadcasted_iota(jnp.int32, (C, LANES), 1) < RWKV_HEAD_DIM
    incl2 = (si <= ti, si >= ti)
    strict2 = (si < ti, si > ti)
    eye = (si == ti).astype(F32)

    def bd(x):
        zero = jnp.zeros_like(x)
        return jnp.concatenate([jnp.where(low, x, zero), jnp.where(low, zero, x)], axis=0)

    def bd2(x):
        return bd(x[0]), bd(x[1])

    base = []
    for x_ref, d, _, _ in seqs:
        incl, _ = _tri_masks(d)
        r, v, kk, lw, k, beta = (x_ref[:, j * Wd:(j + 1) * Wd] for j in range(6))
        b = _cumsum_rows(incl, lw)
        btot = jnp.sum(lw, axis=0, keepdims=True)
        nb = jnp.exp(-b)
        eb = jnp.exp(btot - b)
        base.append(dict(
            ar=_split(jnp.concatenate([-kk * jnp.exp(b - lw), r * jnp.exp(b)], axis=0)),
            bbar=_split(beta * nb), kbar=(k * nb).astype(BF16),
            hat=jnp.concatenate([beta * eb, k * eb], axis=0).astype(BF16),
            v16=v.astype(BF16), gtot=jnp.exp(btot)))

    units = [(n, q) for n in range(len(seqs)) for q in range(NP)]
    dirs = [seqs[n][1] for n, _ in units]

    def col(name, part=None):
        vals = [base[n][name] for n, _ in units]
        if part is not None:
            vals = [x[part] for x in vals]
        return [x[:, sls[q]] for x, (_, q) in zip(vals, units)]

    ar_h, ar_l, v16, hat = col("ar", 0), col("ar", 1), col("v16"), col("hat")
    gtot = col("gtot")
    g1 = [_mm3s((a, b), bd2((c, e)), _NT) for a, b, c, e in zip(ar_h, ar_l, col("bbar", 0), col("bbar", 1))]
    g2 = [_dg(a, bd(kb), _NT) for a, kb in zip(ar_h, col("kbar"))]
    a_ab = [jnp.where(strict2[d], g[:C], 0.0) for g, d in zip(g1, dirs)]
    a_rb = [jnp.where(incl2[d], g[C:], 0.0).astype(BF16) for g, d in zip(g1, dirs)]
    a_ak = [jnp.where(strict2[d], g[:C], 0.0).astype(BF16) for g, d in zip(g2, dirs)]
    a_rk = [jnp.where(incl2[d], g[C:], 0.0).astype(BF16) for g, d in zip(g2, dirs)]
    vbd = [bd(x) for x in v16]
    av = [_dg(a, vv, _NN) for a, vv in zip(a_ak, vbd)]
    t = [eye + a for a in a_ab]
    p = [_split(a) for a in a_ab]
    p = [_split(_mm3s(pp, bd2(pp))) for pp in p]
    for _ in range(4):
        ts = [_split(tt) for tt in t]
        res = [_mm3s((jnp.concatenate([pp[0], tt[0]], axis=0), jnp.concatenate([pp[1], tt[1]], axis=0)),
                     bd2(pp)) for pp, tt in zip(p, ts)]
        p = [_split(rr[:C]) for rr in res]
        t = [tt + rr[C:] for tt, rr in zip(t, res)]
    t = [_split(tt + _mm3s(_split(tt), bd2(pp))) for tt, pp in zip(t, p)]

    s_old = [seqs[n][3][q] for n, q in units]
    z = [_dg(a, bd(s.astype(BF16)), _NT) for a, s in zip(ar_h, s_old)]
    u = [_mm3s(tt, bd2(_split(zz[:C] + aa))).astype(BF16) for tt, zz, aa in zip(t, z, av)]
    ys = [zz[C:] + _dg(jnp.concatenate([rb, rk], axis=1), jnp.concatenate([bd(uu), vv], axis=0), _NN)
          for zz, rb, rk, uu, vv in zip(z, a_rb, a_rk, u, vbd)]
    cross = [_dg(jnp.concatenate([uu, vv], axis=0), hh, _TN)
             for uu, vv, hh in zip(u, v16, hat)]
    stores = []
    for (n, q), yy, s, g, cc in zip(units, ys, s_old, gtot, cross):
        stores.append((seqs[n][2], (slice(None), sls[q]), yy))
        stores.append((seqs[n][3], (q,), s * g + jnp.where(low, cc[:C], cc[C:])))
    return stores


def _gla_chunks(seqs, a2_ref, ab_ref):
    Kw = GLA_K_WIDTH
    lora = 2 * Kw + 2 * GLA_V_WIDTH
    base = []
    for x_ref, d, _, _ in seqs:
        incl, _ = _tri_masks(d)
        xa = (_mm3(_split(x_ref[:, lora:lora + LANES]), _split(a2_ref[:, d * Kw:(d + 1) * Kw]))
              + ab_ref[:, d * Kw:(d + 1) * Kw])
        log_a = (jnp.minimum(xa, 0.0) - jnp.log1p(jnp.exp(-jnp.abs(xa)))) * (1.0 / GLA_GATE_NORMALIZER)
        b = _cumsum_rows(incl, log_a)
        bl = jnp.sum(log_a, axis=0, keepdims=True)
        q = x_ref[:, 0:Kw] * (GLA_DK ** -0.5)
        k = x_ref[:, Kw:2 * Kw]
        base.append(((q * jnp.exp(b)).astype(BF16), (k * jnp.exp(-b)).astype(BF16),
                     (k * jnp.exp(bl - b)).astype(BF16), jnp.exp(bl), incl))

    units = [(n, h) for n in range(len(seqs)) for h in range(GLA_HEADS)]
    ks = [slice(h * GLA_DK, (h + 1) * GLA_DK) for _, h in units]
    vs = [slice(2 * Kw + h * GLA_DV, 2 * Kw + (h + 1) * GLA_DV) for _, h in units]
    q_in = [base[n][0][:, c] for (n, _), c in zip(units, ks)]
    v_h = [seqs[n][0][:, c].astype(BF16) for (n, _), c in zip(units, vs)]
    att = [jnp.where(base[n][4], _dg(qq, base[n][1][:, c], _NT), 0.0).astype(BF16)
           for (n, _), qq, c in zip(units, q_in, ks)]
    o_in = [_dg(a, vv, _NN) for a, vv in zip(att, v_h)]
    kv = [_dg(vv, base[n][2][:, c], _TN) for (n, _), vv, c in zip(units, v_h, ks)]
    s_old = [seqs[n][3][h] for n, h in units]
    outs = [oo + _dg(qq, s.astype(BF16), _NT) for oo, qq, s in zip(o_in, q_in, s_old)]
    stores = []
    for (n, h), c, s, o, x in zip(units, ks, s_old, outs, kv):
        stores.append((seqs[n][2], (slice(None), slice(h * GLA_DV, (h + 1) * GLA_DV)), o))
        stores.append((seqs[n][3], (h,), s * base[n][3][:, c] + x))
    return stores


def _scan_kernel(fw_ref, bw_ref, s0r_ref, xf, xb, a2_ref, ab_ref, s0g_ref,
                 yf_o, yb_o, str_o, of_o, ob_o, stg_o, sr_scr, sg_scr, *, nc):
    i = pl.program_id(1)

    @pl.when(i == 0)
    def _():
        sr_scr[...] = s0r_ref[...]
        sg_scr[...] = s0g_ref[...]

    both = [(bb, d) for bb in range(SCAN_BATCH) for d in range(2)]
    stores = (_rwkv_chunks([((fw_ref, bw_ref)[d].at[bb], d, (yf_o, yb_o)[d].at[bb], sr_scr.at[bb, d])
                            for bb, d in both])
              + _gla_chunks([((xf, xb)[d].at[bb], d, (of_o, ob_o)[d].at[bb], sg_scr.at[bb, d])
                             for bb, d in both], a2_ref, ab_ref))
    for ref, idx, val in stores:
        ref[idx] = val

    @pl.when(i == nc - 1)
    def _():
        str_o[...] = sr_scr[...]
        stg_o[...] = sg_scr[...]


def _scan_call(prep, s0, zg, s0t, a2, ab, B, T):
    fw, bw = prep
    N, Wd = fw.shape[0], RWKV_WIDTH
    Wg = zg.shape[1]
    nc = T // CHUNK
    BB = SCAN_BATCH
    H, Dh = RWKV_HEADS, RWKV_HEAD_DIM
    pair_shape = (2, H // 2, Dh, 2 * Dh)
    gla_shape = (2, GLA_HEADS, GLA_DV, GLA_DK)

    def rows(width, backward):
        if backward:
            return pl.BlockSpec((BB, CHUNK, width), lambda b, i: (b, nc - 1 - i, 0))
        return pl.BlockSpec((BB, CHUNK, width), lambda b, i: (b, i, 0))

    st_r = pl.BlockSpec((BB,) + pair_shape, lambda b, i: (b, 0, 0, 0, 0))
    st_g = pl.BlockSpec((BB,) + gla_shape, lambda b, i: (b, 0, 0, 0, 0))
    y = jax.ShapeDtypeStruct((B, T, Wd), F32)
    o = jax.ShapeDtypeStruct((B, T, GLA_V_WIDTH), F32)
    s0p = s0.reshape(B, 2, H // 2, 2, Dh, Dh).transpose(0, 1, 2, 4, 3, 5).reshape((B,) + pair_shape)
    zg3 = zg.reshape(B, T, Wg)
    yf, yb, stp, gf_out, gb_out, stg = pl.pallas_call(
        functools.partial(_scan_kernel, nc=nc),
        grid=(B // BB, nc),
        in_specs=[rows(SCAN_PACK, False), rows(SCAN_PACK, True), st_r,
                  rows(Wg, False), rows(Wg, True),
                  _const_spec((LANES, 2 * GLA_K_WIDTH)), _const_spec((1, 2 * GLA_K_WIDTH)), st_g],
        out_specs=[rows(Wd, False), rows(Wd, True), st_r,
                   rows(GLA_V_WIDTH, False), rows(GLA_V_WIDTH, True), st_g],
        out_shape=[y, y, jax.ShapeDtypeStruct((B,) + pair_shape, F32),
                   o, o, jax.ShapeDtypeStruct((B,) + gla_shape, F32)],
        scratch_shapes=[pltpu.VMEM((BB,) + pair_shape, F32), pltpu.VMEM((BB,) + gla_shape, F32)],
        compiler_params=_params(("parallel", "arbitrary")),
        name="scans",
    )(fw.reshape(B, T, SCAN_PACK), bw.reshape(B, T, SCAN_PACK), s0p, zg3, zg3, a2, ab, s0t)
    st_out = stp.reshape(B, 2, H // 2, Dh, 2, Dh).transpose(0, 1, 2, 4, 3, 5).reshape(B, 2, H, Dh, Dh)
    return (yf.reshape(N, Wd), yb.reshape(N, Wd), st_out,
            gf_out.reshape(N, GLA_V_WIDTH), gb_out.reshape(N, GLA_V_WIDTH), stg)


def _rope(x, cos, sin):
    n = x.shape[1]
    lane = lax.broadcasted_iota(jnp.int32, x.shape, 1)
    up = pltpu.roll(x, n - 16, 1)
    dn = pltpu.roll(x, 16, 1)
    sw = jnp.where((lane % 32) < 16, up, dn)
    reps = n // LANES
    if reps > 1:
        cos = jnp.concatenate([cos] * reps, axis=1)
        sin = jnp.concatenate([sin] * reps, axis=1)
    return x * cos + sw * sin


def _attn_prep_kernel(*refs, rope):
    if rope:
        z_ref, qn_ref, kn_ref, segq_ref, segk_ref, cos_ref, sin_ref, q_o, kn_o, kr_o, ve_o = refs
    else:
        z_ref, qn_ref, kn_ref, segq_ref, segk_ref, q_o, kn_o, kr_o, ve_o = refs
    zq = z_ref[:, 0:ATT_WIDTH]
    zk = z_ref[:, ATT_WIDTH:ATT_WIDTH + KV_WIDTH]
    qh = zq * lax.rsqrt(_seg_sum(zq * zq, segq_ref[...]) * (1.0 / HEAD_DIM) + NORM_EPS) * qn_ref[...]
    kh = zk * lax.rsqrt(_seg_sum(zk * zk, segk_ref[...]) * (1.0 / HEAD_DIM) + NORM_EPS) * kn_ref[...]
    kn_o[...] = kh
    if rope:
        qh = _rope(qh, cos_ref[...], sin_ref[...])
        kh = _rope(kh, cos_ref[...], sin_ref[...])
    q_o[...] = (qh * (LOG2_E * HEAD_DIM ** -0.5)).astype(BF16)
    kr_o[...] = kh.astype(BF16)
    v = z_ref[:, ATT_WIDTH + KV_WIDTH:ATT_WIDTH + 2 * KV_WIDTH]
    group = lax.broadcasted_iota(jnp.int32, v.shape, 1) // HEAD_DIM
    for g in range(KV_HEADS):
        ve_o[:, g * LANES:(g + 1) * LANES] = jnp.where(group == g, v, 1.0).astype(BF16)


def _attn_prep_call(za, lw, seq_len, rope_tabs):
    N, Wz = za.shape
    tm = ROW_TILE
    bps = seq_len // tm
    rope = rope_tabs is not None
    in_specs = [pl.BlockSpec((tm, Wz), lambda i: (i, 0)),
                _const_spec((1, ATT_WIDTH)), _const_spec((1, KV_WIDTH)),
                _const_spec((ATT_WIDTH, ATT_WIDTH)), _const_spec((KV_WIDTH, KV_WIDTH))]
    args = [za, lw["qn"], lw["kn"], lw["seg64"], lw["seg64"][:KV_WIDTH, :KV_WIDTH]]
    if rope:
        in_specs += [pl.BlockSpec((tm, LANES), lambda i: (i % bps, 0))] * 2
        args += list(rope_tabs)
    return pl.pallas_call(
        functools.partial(_attn_prep_kernel, rope=rope),
        grid=(N // tm,),
        in_specs=in_specs,
        out_specs=[pl.BlockSpec((tm, ATT_WIDTH), lambda i: (i, 0)),
                   pl.BlockSpec((tm, KV_WIDTH), lambda i: (i, 0)),
                   pl.BlockSpec((tm, KV_WIDTH), lambda i: (i, 0)),
                   pl.BlockSpec((tm, KV_HEADS * LANES), lambda i: (i, 0))],
        out_shape=[jax.ShapeDtypeStruct((N, ATT_WIDTH), BF16),
                   jax.ShapeDtypeStruct((N, KV_WIDTH), F32),
                   jax.ShapeDtypeStruct((N, KV_WIDTH), BF16),
                   jax.ShapeDtypeStruct((N, KV_HEADS * LANES), BF16)],
        compiler_params=_params(("parallel",)),
        name="attn_prep",
    )(*args)


def _attn_kernel(*refs, cached):
    if cached:
        q_ref, k_ref, v_ref, ck_ref, cv_ref, o_ref = refs
    else:
        q_ref, k_ref, v_ref, o_ref = refs
    G = N_HEADS // KV_HEADS
    tq = q_ref.shape[0]
    for g in range(KV_HEADS):
        gs = slice(g * HEAD_DIM, (g + 1) * HEAD_DIM)
        other = slice((1 - g) * HEAD_DIM, (2 - g) * HEAD_DIM)
        kg = k_ref[:, gs]
        vg = v_ref[:, g * LANES:(g + 1) * LANES]
        if cached:
            cv = cv_ref[...]
            own = (lax.broadcasted_iota(jnp.int32, cv.shape, 1) // HEAD_DIM) == g
            ckg = ck_ref[:, gs].astype(BF16)
            cvg = jnp.where(own, cv, 1.0).astype(BF16)
        for h0 in range(g * G, (g + 1) * G, ATT_STACK):
            heads = range(h0, h0 + ATT_STACK)
            qs = jnp.concatenate([q_ref[:, h * HEAD_DIM:(h + 1) * HEAD_DIM] for h in heads], axis=0)
            chunks = [(kg[c:c + KEY_CHUNK], vg[c:c + KEY_CHUNK]) for c in range(0, kg.shape[0], KEY_CHUNK)]
            if cached:
                chunks.append((ckg, cvg))
            m = None
            for kc, _ in chunks:
                mc = jnp.max(_dg(qs, kc, _NT), axis=-1, keepdims=True)
                m = mc if m is None else jnp.maximum(m, mc)
            oe = None
            for kc, vc in chunks:
                part = _dg(jnp.exp2(_dg(qs, kc, _NT) - m).astype(BF16), vc, _NN)
                oe = part if oe is None else oe + part
            o = oe[:, gs] / oe[:, other]
            for j, h in enumerate(heads):
                o_ref[:, h * HEAD_DIM:(h + 1) * HEAD_DIM] = o[j * tq:(j + 1) * tq]


def _attn_call(q, k, vext, cache, B, T):
    N = q.shape[0]
    tq = Q_TILE
    nq = T // tq
    in_specs = [pl.BlockSpec((tq, ATT_WIDTH), lambda b, i: (b * nq + i, 0)),
                pl.BlockSpec((T, KV_WIDTH), lambda b, i: (b, 0)),
                pl.BlockSpec((T, KV_HEADS * LANES), lambda b, i: (b, 0))]
    args = [q, k, vext]
    if cache is not None:
        ck, cv, layer = cache
        P = ck.shape[2]
        cspec = pl.BlockSpec((None, None, P, KV_WIDTH), lambda b, i: (b, layer, 0, 0))
        in_specs += [cspec, cspec]
        args += [ck, cv]
    return pl.pallas_call(
        functools.partial(_attn_kernel, cached=cache is not None),
        grid=(B, nq),
        in_specs=in_specs,
        out_specs=pl.BlockSpec((tq, ATT_WIDTH), lambda b, i: (b * nq + i, 0)),
        out_shape=jax.ShapeDtypeStruct((N, ATT_WIDTH), F32),
        compiler_params=_params(("parallel", "arbitrary")),
        name="attention",
    )(*args)


def _merge_kernel(yf, yb, bon, g, oa, gf, gb, gr, mr, ma, mg, x_ref, m_ref,
                  lng, lnb, gn, nf, seg_ref, pr, pa, pg, wo, rw, rb,
                  x1_o, h2_o, ti_o, tg_o):
    seg = seg_ref[...]
    inv = 1.0 / RWKV_HEAD_DIM
    y = yf[...] + yb[...]
    mu = _seg_sum(y, seg) * inv
    yc = y - mu
    var = _seg_sum(yc * yc, seg) * inv
    o_r = (yc * lax.rsqrt(var + RWKV_GN_EPS) * lng[...] + lnb[...] + bon[...]) * g[...]

    gate_r = gr[...]
    silu_r = gate_r * _sigmoid(gate_r)
    gnv = gn[...]
    cols = []
    for h in range(GLA_HEADS):
        vs = slice(h * GLA_DV, (h + 1) * GLA_DV)
        o = gf[:, vs] + gb[:, vs]
        ms = jnp.mean(o * o, axis=-1, keepdims=True)
        cols.append(o * lax.rsqrt(ms + NORM_EPS) * gnv * silu_r[:, vs])
    o_g = jnp.concatenate(cols, axis=1)

    merged = (_sigmoid(mr[...]) * _dot(o_r.astype(BF16), pr[...])
              + _sigmoid(ma[...]) * _dot(oa[...].astype(BF16), pa[...])
              + _sigmoid(mg[...]) * _dot(o_g.astype(BF16), pg[...]))
    x1 = x_ref[...] + m_ref[2:3, :] * _dot(merged.astype(BF16), wo[...])
    x1_o[...] = x1
    hn = x1 * lax.rsqrt(jnp.mean(x1 * x1, axis=-1, keepdims=True) + NORM_EPS) * nf[...]
    h2 = hn * (1.0 + m_ref[4:5, :]) + m_ref[3:4, :]
    h2_o[...] = h2

    logits = _mm3(_split(h2), _split(rw[...])) + rb[...]
    lane = lax.broadcasted_iota(jnp.int32, logits.shape, 1)
    vals, idxs = [], []
    for _ in range(TOP_K):
        m = jnp.max(logits, axis=-1, keepdims=True)
        idx = jnp.min(jnp.where(logits == m, lane, LANES), axis=-1, keepdims=True)
        vals.append(m)
        idxs.append(idx)
        logits = jnp.where(lane == idx, -jnp.inf, logits)
    es = [jnp.exp(vv - vals[0]) for vv in vals]
    den = es[0] + es[1] + es[2] + es[3]
    ti = jnp.zeros(logits.shape, jnp.int32)
    tg = jnp.zeros(logits.shape, F32)
    for j in range(TOP_K):
        ti = jnp.where(lane == j, idxs[j], ti)
        tg = jnp.where(lane == j, es[j] / den, tg)
    ti_o[...] = ti
    tg_o[...] = tg


def _merge_call(yf, yb, bon, g, oa, gf, gb, zg, zm, x, mod, lw, rows_per_cond):
    N, D = x.shape
    tm = ROW_TILE
    Wd = RWKV_WIDTH

    def rows(w, col=0):
        return pl.BlockSpec((tm, w), lambda i: (i, col))

    in_specs = [rows(Wd)] * 4 + [rows(ATT_WIDTH), rows(GLA_V_WIDTH), rows(GLA_V_WIDTH),
                                 rows(GLA_V_WIDTH, 2), rows(D, 0), rows(D, 1), rows(D, 2), rows(D),
                                 pl.BlockSpec((None, 6, D), lambda i: ((i * tm) // rows_per_cond, 0, 0)),
                                 _const_spec((1, Wd)), _const_spec((1, Wd)), _const_spec((1, GLA_DV)),
                                 _const_spec((1, D)), _const_spec((Wd, Wd)),
                                 _const_spec((Wd, D)), _const_spec((ATT_WIDTH, D)), _const_spec((GLA_V_WIDTH, D)),
                                 _const_spec((D, D)), _const_spec((D, LANES)), _const_spec((1, LANES))]
    return pl.pallas_call(
        _merge_kernel,
        grid=(N // tm,),
        in_specs=in_specs,
        out_specs=[rows(D), rows(D), rows(LANES), rows(LANES)],
        out_shape=[jax.ShapeDtypeStruct((N, D), F32), jax.ShapeDtypeStruct((N, D), F32),
                   jax.ShapeDtypeStruct((N, LANES), jnp.int32), jax.ShapeDtypeStruct((N, LANES), F32)],
        compiler_params=_params(("parallel",)),
        name="merge",
    )(yf, yb, bon, g, oa, gf, gb, zg, zm, zm, zm, x, mod,
      lw["ln_g"], lw["ln_b"], lw["gla_norm"], lw["norm_ffn"], lw["seg64"],
      lw["p_rwkv"], lw["p_attn"], lw["p_gla"], lw["w_out"], lw["router_w"], lw["router_b"])


def _moe_kernel(be_ref, na_ref, x_ref, gate_ref, w1_ref, b1_ref, w2_ref, b2_ref, o_ref, w1_scr, w2_scr):
    i = pl.program_id(0)

    @pl.when(jnp.logical_or(i == 0, be_ref[i] != be_ref[jnp.maximum(i - 1, 0)]))
    def _():
        w1_scr[...] = w1_ref[...].astype(BF16)
        w2_scr[...] = w2_ref[...].astype(BF16)

    @pl.when(i < na_ref[0])
    def _():
        z = _dot(x_ref[...].astype(BF16), w1_scr[...]) + b1_ref[...]
        glu = jnp.minimum(z[:, :D_EXPERT], SWIGLU_LIMIT)
        lin = jnp.clip(z[:, D_EXPERT:], -SWIGLU_LIMIT, SWIGLU_LIMIT)
        act = glu * _sigmoid(SWIGLU_ALPHA * glu) * (lin + 1.0)
        o_ref[...] = (_dot(act.astype(BF16), w2_scr[...]) + b2_ref[...]) * gate_ref[...]

    @pl.when(i >= na_ref[0])
    def _():
        o_ref[...] = jnp.zeros_like(o_ref)


def _moe_call(xg, row_gate, block_exp, n_active, layer, w1, b1, w2, b2):
    R, D = xg.shape
    tb = EXPERT_BLOCK
    F2 = w1.shape[3]
    grid_spec = pltpu.PrefetchScalarGridSpec(
        num_scalar_prefetch=2,
        grid=(R // tb,),
        in_specs=[pl.BlockSpec((tb, D), lambda i, be, na: (i, 0)),
                  pl.BlockSpec((tb, 1), lambda i, be, na: (i, 0)),
                  pl.BlockSpec((None, None, D, F2), lambda i, be, na: (layer, be[i], 0, 0)),
                  pl.BlockSpec((None, 1, F2), lambda i, be, na: (be[i], 0, 0)),
                  pl.BlockSpec((None, None, F2 // 2, D), lambda i, be, na: (layer, be[i], 0, 0)),
                  pl.BlockSpec((None, 1, D), lambda i, be, na: (be[i], 0, 0))],
        out_specs=pl.BlockSpec((tb, D), lambda i, be, na: (i, 0)),
        scratch_shapes=[pltpu.VMEM((D, F2), BF16), pltpu.VMEM((F2 // 2, D), BF16)],
    )
    return pl.pallas_call(
        _moe_kernel,
        grid_spec=grid_spec,
        out_shape=jax.ShapeDtypeStruct((R, D), F32),
        compiler_params=_params(("arbitrary",)),
        name="moe_experts",
    )(block_exp, n_active, xg, row_gate, w1, b1, w2, b2)


def _combine_kernel(x1_ref, m_ref, y0_ref, y1_ref, y2_ref, y3_ref, o_ref):
    y = (y0_ref[...] + y1_ref[...]) + (y2_ref[...] + y3_ref[...])
    o_ref[...] = x1_ref[...] + m_ref[5:6, :] * y


def _combine_call(x1, mod, ys, rows_per_cond):
    N, D = x1.shape
    tm = ROW_TILE
    rows = pl.BlockSpec((tm, D), lambda i: (i, 0))
    return pl.pallas_call(
        _combine_kernel,
        grid=(N // tm,),
        in_specs=([rows, pl.BlockSpec((None, 6, D), lambda i: ((i * tm) // rows_per_cond, 0, 0))]
                  + [pl.BlockSpec((None, tm, D), functools.partial(lambda i, j: (j, i, 0), j=j))
                     for j in range(TOP_K)]),
        out_specs=rows,
        out_shape=jax.ShapeDtypeStruct((N, D), F32),
        compiler_params=_params(("parallel",)),
        name="combine",
    )(x1, mod, *([ys] * TOP_K))


def _moe_ffn(h2, top_i, top_g, lw):
    N, D = h2.shape
    n_as = N * TOP_K
    n_blocks = -(-n_as // EXPERT_BLOCK) + N_EXPERTS
    R = n_blocks * EXPERT_BLOCK
    ex = jnp.arange(N_EXPERTS, dtype=jnp.int32)

    def lut(onehot, table):
        return jnp.sum(jnp.where(onehot, table[None, :], 0), axis=1)

    flat_e = top_i.reshape(n_as)
    flat_g = top_g.reshape(n_as)
    oh_a = flat_e[:, None] == ex[None, :]
    counts = jnp.sum(oh_a, axis=0, dtype=jnp.int32)
    starts = jnp.cumsum(counts) - counts
    padded = (counts + EXPERT_BLOCK - 1) // EXPERT_BLOCK * EXPERT_BLOCK
    pends = jnp.cumsum(padded)
    pstarts = pends - padded
    order = jnp.argsort(flat_e).astype(jnp.int32)
    rank = jnp.argsort(order).astype(jnp.int32)
    pos = rank + lut(oh_a, pstarts - starts)
    blk_start = jnp.arange(n_blocks, dtype=jnp.int32) * EXPERT_BLOCK
    block_exp = jnp.minimum(jnp.sum(pends[None, :] <= blk_start[:, None], axis=1),
                            N_EXPERTS - 1).astype(jnp.int32)
    n_active = (pends[-1] // EXPERT_BLOCK).astype(jnp.int32).reshape(1)
    oh_p = jnp.repeat(block_exp, EXPERT_BLOCK)[:, None] == ex[None, :]
    idx = jnp.arange(R, dtype=jnp.int32) - lut(oh_p, pstarts)
    valid = (idx >= 0) & (idx < lut(oh_p, counts))
    asg = order[jnp.clip(lut(oh_p, starts) + idx, 0, n_as - 1)]
    row_tok = jnp.where(valid, asg // TOP_K, 0)
    row_gate = jnp.where(valid, flat_g[asg], 0.0)
    out = _moe_call(h2[row_tok], row_gate.reshape(R, 1), block_exp, n_active,
                    lw["layer"], lw["moe_w1"], lw["moe_b1"], lw["moe_w2"], lw["moe_b2"])
    return out[pos.reshape(N, TOP_K).T.reshape(n_as)].reshape(TOP_K, N, D)


def _seg_matrix(n, seg):
    idx = jnp.arange(n) // seg
    return (idx[:, None] == idx[None, :]).astype(BF16)


def _rope_tables(T):
    quarter = HEAD_DIM // 4
    inv_freq = ROPE_THETA ** (-jnp.arange(quarter, dtype=F32) / quarter)
    t = jnp.arange(T)
    row = (t // GRID_W).astype(F32)
    col = (t % GRID_W).astype(F32)
    ang_r = row[:, None] * inv_freq
    ang_c = col[:, None] * inv_freq
    cos = jnp.concatenate([jnp.cos(ang_r)] * 2 + [jnp.cos(ang_c)] * 2, axis=1)
    sin = jnp.concatenate([-jnp.sin(ang_r), jnp.sin(ang_r), -jnp.sin(ang_c), jnp.sin(ang_c)], axis=1)
    return jnp.concatenate([cos, cos], axis=1), jnp.concatenate([sin, sin], axis=1)


def _layer_weights(p, l):
    Wd = RWKV_WIDTH
    w_in = p["w_in"][l]
    c0 = RWKV_COLS
    aq, ak, av, gq, gk, gv, gl, gr = (c0, c0 + 512, c0 + 640, c0 + 768, c0 + 1280, c0 + 1792, c0 + 2816, c0 + 2848)
    m0 = gr + GLA_V_WIDTH
    D = D_MODEL
    w_g = jnp.concatenate([w_in[:, gq:gl], w_in[:, gr:m0], w_in[:, gl:gr],
                           jnp.zeros((D, LANES - 2 * GLA_LORA), F32)], axis=1)
    z64 = jnp.zeros((DECAY_LORA, Wd), F32)
    w2 = jnp.concatenate([jnp.concatenate([p["rwkv_w2"][l, 0], z64], axis=1),
                          jnp.concatenate([z64, p["rwkv_w2"][l, 1]], axis=1)], axis=0)
    a2 = jnp.concatenate([jnp.concatenate([p["rwkv_a2"][l, 0], z64], axis=1),
                          jnp.concatenate([z64, p["rwkv_a2"][l, 1]], axis=1)], axis=0)
    z16 = jnp.zeros((GLA_LORA, GLA_K_WIDTH), F32)
    ga2 = jnp.concatenate([jnp.concatenate([p["gla_a2"][l, 0], z16], axis=1),
                           jnp.concatenate([z16, p["gla_a2"][l, 1]], axis=1),
                           jnp.zeros((LANES - 2 * GLA_LORA, 2 * GLA_K_WIDTH), F32)], axis=0)
    return {
        "w_r": w_in[:, :c0].astype(BF16), "w_a": w_in[:, aq:gq].astype(BF16),
        "w_g": w_g.astype(BF16), "w_m": w_in[:, m0:].astype(BF16),
        "norm_mix": p["norm_mix"][l].reshape(1, D), "norm_ffn": p["norm_ffn"][l].reshape(1, D),
        "mu": p["rwkv_mu"][l].reshape(1, c0),
        "kk": p["rwkv_kk"][l].reshape(1, Wd), "ka": p["rwkv_ka"][l].reshape(1, Wd),
        "rk": p["rwkv_rk"][l].reshape(1, Wd),
        "w0": p["rwkv_w0"][l].reshape(1, 2 * Wd), "a0": p["rwkv_a0"][l].reshape(1, 2 * Wd),
        "w2": w2, "a2": a2, "g2": p["rwkv_g2"][l],
        "ln_g": p["rwkv_ln_g"][l].reshape(1, Wd), "ln_b": p["rwkv_ln_b"][l].reshape(1, Wd),
        "seg64": _seg_matrix(Wd, RWKV_HEAD_DIM),
        "qn": jnp.tile(p["attn_qn"][l], N_HEADS).reshape(1, ATT_WIDTH),
        "kn": jnp.tile(p["attn_kn"][l], KV_HEADS).reshape(1, KV_WIDTH),
        "gla_a2": ga2, "gla_ab": p["gla_ab"][l].reshape(1, 2 * GLA_K_WIDTH),
        "gla_norm": p["gla_norm"][l].reshape(1, GLA_DV),
        "p_rwkv": p["p_rwkv"][l].astype(BF16), "p_attn": p["p_attn"][l].astype(BF16),
        "p_gla": p["p_gla"][l].astype(BF16), "w_out": p["w_out"][l].astype(BF16),
        "router_w": jnp.concatenate([p["router_w"][l], jnp.zeros((D, LANES - N_EXPERTS), F32)], axis=1),
        "router_b": jnp.concatenate([p["router_b"][l], jnp.full((LANES - N_EXPERTS,), -jnp.inf, F32)]).reshape(1, LANES),
        "layer": l, "moe_w1": p["moe_w1"], "moe_b1": p["moe_b1"][l].reshape(N_EXPERTS, 1, 2 * D_EXPERT),
        "moe_w2": p["moe_w2"], "moe_b2": p["moe_b2"][l].reshape(N_EXPERTS, 1, D_MODEL),
    }


def _trunk_layer(x, mod, lw, B, T, ctx):
    N = B * T
    rpc = N if mod.shape[0] == 1 else T
    zr, za, zg, zm = (_inproj_call(x, mod, lw["norm_mix"], lw[name], rpc, lw[name].shape[1])
                      for name in ("w_r", "w_a", "w_g", "w_m"))

    fw, bw, g, bon = _rwkv_prep_call(zr, T, lw)
    if ctx is None:
        s0r = jnp.zeros((B, 2, RWKV_HEADS, RWKV_HEAD_DIM, RWKV_HEAD_DIM), F32)
        s0g = jnp.zeros((B, 2, GLA_HEADS, GLA_DV, GLA_DK), F32)
        cache, tabs = None, None
    else:
        s0r = ctx["rwkv"]
        s0g = jnp.swapaxes(ctx["gla"], -1, -2)
        cache, tabs = (ctx["k"], ctx["v"], ctx["layer"]), _rope_tables(T)
    yf, yb, s_r, gf, gb, s_g = _scan_call((fw, bw), s0r, zg, s0g, lw["gla_a2"], lw["gla_ab"], B, T)
    q, kn, kr, vext = _attn_prep_call(za, lw, T, tabs)
    oa = _attn_call(q, kr, vext, cache, B, T)
    x1, h2, ti, tg = _merge_call(yf, yb, bon, g, oa, gf, gb, zg, zm, x, mod, lw, rpc)
    y4 = _moe_ffn(h2, ti[:, :TOP_K], tg[:, :TOP_K], lw)
    x2 = _combine_call(x1, mod, y4, rpc)
    return x2, (kn, za[:, ATT_WIDTH + KV_WIDTH:], s_r, jnp.swapaxes(s_g, -1, -2))


def kernel(x_prompt, x_sample, c, cache_k, cache_v, state_rwkv, state_gla, c_ctx, ada_w, ada_b, norm_mix, norm_ffn, w_in, rwkv_mu, rwkv_w0, rwkv_w2, rwkv_a0, rwkv_a2, rwkv_g2, rwkv_kk, rwkv_ka, rwkv_rk, rwkv_ln_g, rwkv_ln_b, attn_qn, attn_kn, gla_a2, gla_ab, gla_norm, p_rwkv, p_attn, p_gla, w_out, router_w, router_b, moe_w1, moe_b1, moe_w2, moe_b2):
    p = {"w_in": w_in, "norm_mix": norm_mix, "norm_ffn": norm_ffn, "rwkv_mu": rwkv_mu, "rwkv_w0": rwkv_w0,
         "rwkv_w2": rwkv_w2, "rwkv_a0": rwkv_a0, "rwkv_a2": rwkv_a2, "rwkv_g2": rwkv_g2, "rwkv_kk": rwkv_kk,
         "rwkv_ka": rwkv_ka, "rwkv_rk": rwkv_rk, "rwkv_ln_g": rwkv_ln_g, "rwkv_ln_b": rwkv_ln_b,
         "attn_qn": attn_qn, "attn_kn": attn_kn, "gla_a2": gla_a2, "gla_ab": gla_ab, "gla_norm": gla_norm,
         "p_rwkv": p_rwkv, "p_attn": p_attn, "p_gla": p_gla, "w_out": w_out, "router_w": router_w,
         "router_b": router_b, "moe_w1": moe_w1, "moe_b1": moe_b1, "moe_w2": moe_w2, "moe_b2": moe_b2}
    Bp, Tp, D = x_prompt.shape
    Bs, Ts, _ = x_sample.shape
    L = ada_w.shape[0]
    P = cache_k.shape[2]

    cond8 = jnp.concatenate([c_ctx[None, :], c, jnp.zeros((8 - 1 - Bs, D), F32)], axis=0)
    mod = _ada_call(cond8, ada_w, ada_b).reshape(L, 8, 6, D)
    weights = [_layer_weights(p, l) for l in range(L)]
    ck = cache_k.reshape(Bs, L, P, KV_WIDTH)
    cv = cache_v.reshape(Bs, L, P, KV_WIDTH)

    yp = x_prompt.reshape(Bp * Tp, D)
    ks, vs, srs, sgs = [], [], [], []
    for l in range(L):
        yp, (kn, vh, s_r, s_g) = _trunk_layer(yp, mod[l, 0:1], weights[l], Bp, Tp, None)
        ks.append(kn.reshape(Bp, Tp, KV_HEADS, HEAD_DIM))
        vs.append(vh.reshape(Bp, Tp, KV_HEADS, HEAD_DIM))
        srs.append(s_r)
        sgs.append(s_g)

    ys = x_sample.reshape(Bs * Ts, D)
    for l in range(L):
        ctx = {"k": ck, "v": cv, "layer": l, "rwkv": state_rwkv[:, l], "gla": state_gla[:, l]}
        ys, _ = _trunk_layer(ys, mod[l, 1:1 + Bs], weights[l], Bs, Ts, ctx)

    return (yp.reshape(Bp, Tp, D), ys.reshape(Bs, Ts, D),
            jnp.stack(ks, axis=1), jnp.stack(vs, axis=1),
            jnp.stack(srs, axis=1), jnp.stack(sgs, axis=1))
```

```python
import functools

import jax
import jax.numpy as jnp
from jax import lax
from jax.experimental import pallas as pl
from jax.experimental.pallas import tpu as pltpu

D_MODEL = 1024
DEPTH = 2
GRID_W = 64
NORM_EPS = 1e-6

RWKV_HEADS = 8
RWKV_HEAD_DIM = 64
RWKV_WIDTH = RWKV_HEADS * RWKV_HEAD_DIM
DECAY_LORA = 64
ICLR_LORA = 64
GATE_LORA = 128
RWKV_GN_EPS = 6.4e-4
RWKV_COLS = 3 * RWKV_WIDTH + 2 * DECAY_LORA + 2 * ICLR_LORA + GATE_LORA
SCAN_PACK = 6 * RWKV_WIDTH

N_HEADS = 8
KV_HEADS = 2
HEAD_DIM = 64
ATT_WIDTH = N_HEADS * HEAD_DIM
KV_WIDTH = KV_HEADS * HEAD_DIM
ROPE_THETA = 10000.0

GLA_HEADS = 4
GLA_DK = 128
GLA_DV = 256
GLA_K_WIDTH = GLA_HEADS * GLA_DK
GLA_V_WIDTH = GLA_HEADS * GLA_DV
GLA_LORA = 16
GLA_GATE_NORMALIZER = 16.0

N_EXPERTS = 32
TOP_K = 4
D_EXPERT = 1024
SWIGLU_ALPHA = 1.702
SWIGLU_LIMIT = 7.0
EXPERT_BLOCK = 512
EXPERT_F_CHUNK = 256

LANES = 128
CHUNK = 64
SCAN_BATCH = 2
ROW_TILE = 256
INPROJ_TILE = 512
Q_TILE = 128
ATT_STACK = 2
KEY_CHUNK = 512
VMEM_LIMIT = 52 * 1024 * 1024
LOG2_E = 1.4426950408889634

F32 = jnp.float32
BF16 = jnp.bfloat16
HI = lax.Precision.HIGHEST


def _dot(a, b, prec=None):
    return jnp.dot(a, b, preferred_element_type=F32, precision=prec)


def _dot_nt(a, b, prec=None):
    return lax.dot_general(a, b, (((1,), (1,)), ((), ())), preferred_element_type=F32, precision=prec)


def _dot_tn(a, b, prec=None):
    return lax.dot_general(a, b, (((0,), (0,)), ((), ())), preferred_element_type=F32, precision=prec)


_NN = (((1,), (0,)), ((), ()))
_NT = (((1,), (1,)), ((), ()))
_TN = (((0,), (0,)), ((), ()))


def _dg(a, b, dn):
    return lax.dot_general(a, b, dn, preferred_element_type=F32)


def _split(x):
    h = x.astype(BF16)
    return h, (x - h.astype(F32)).astype(BF16)


def _mm3(a, b, dn=_NN):
    return _dg(a[0], b[0], dn) + _dg(a[0], b[1], dn) + _dg(a[1], b[0], dn)


def _mm3s(a, b, dn=_NN):
    m = a[0].shape[0]
    top = _dg(jnp.concatenate([a[0], a[1]], axis=0), b[0], dn)
    return top[:m] + top[m:] + _dg(a[0], b[1], dn)


def _cumsum_rows(mask, x):
    m = jnp.where(mask, 1.0, 0.0).astype(BF16)
    h = x.astype(BF16)
    r1 = x - h.astype(F32)
    mid = r1.astype(BF16)
    lo = (r1 - mid.astype(F32)).astype(BF16)
    return _dg(m, h, _NN) + _dg(m, mid, _NN) + _dg(m, lo, _NN)


def _seg_sum(x, seg_bf16):
    xh = x.astype(BF16)
    xl = (x - xh.astype(F32)).astype(BF16)
    return _dot(xh, seg_bf16) + _dot(xl, seg_bf16)


def _sigmoid(x):
    return 1.0 / (1.0 + jnp.exp(-x))


def _params(sem):
    return pltpu.CompilerParams(dimension_semantics=sem, vmem_limit_bytes=VMEM_LIMIT)


def _const_spec(shape):
    nd = len(shape)
    return pl.BlockSpec(shape, lambda *_: (0,) * nd)


def _ada_kernel(c_ref, w_ref, b_ref, o_ref):
    c = c_ref[...]
    s = c * _sigmoid(c)
    o_ref[...] = _mm3(_split(s), _split(w_ref[...])) + b_ref[...]


def _ada_call(cond8, ada_w, ada_b):
    L, D, W = ada_w.shape
    tn = 1536
    return pl.pallas_call(
        _ada_kernel,
        grid=(L, W // tn),
        in_specs=[pl.BlockSpec((8, D), lambda l, j: (0, 0)),
                  pl.BlockSpec((None, D, tn), lambda l, j: (l, 0, j)),
                  pl.BlockSpec((None, 1, tn), lambda l, j: (l, 0, j))],
        out_specs=pl.BlockSpec((None, 8, tn), lambda l, j: (l, 0, j)),
        out_shape=jax.ShapeDtypeStruct((L, 8, W), F32),
        compiler_params=_params(("parallel", "parallel")),
        name="ada",
    )(cond8, ada_w, ada_b.reshape(L, 1, W))


def _inproj_kernel(x_ref, m_ref, g_ref, w_ref, o_ref, h_scr):
    @pl.when(pl.program_id(1) == 0)
    def _():
        x = x_ref[...]
        y = x * lax.rsqrt(jnp.mean(x * x, axis=-1, keepdims=True) + NORM_EPS) * g_ref[...]
        h = y * (1.0 + m_ref[1:2, :]) + m_ref[0:1, :]
        h_scr[...] = h.astype(BF16)

    o_ref[...] = _dot(h_scr[...], w_ref[...])


def _inproj_call(x, mod, gain, w, rows_per_cond, tn):
    N, D = x.shape
    W = w.shape[1]
    tm = INPROJ_TILE
    assert N % tm == 0 and rows_per_cond % tm == 0, (N, rows_per_cond, tm)
    return pl.pallas_call(
        _inproj_kernel,
        grid=(N // tm, W // tn),
        in_specs=[pl.BlockSpec((tm, D), lambda i, j: (i, 0)),
                  pl.BlockSpec((None, 6, D), lambda i, j: ((i * tm) // rows_per_cond, 0, 0)),
                  pl.BlockSpec((1, D), lambda i, j: (0, 0)),
                  pl.BlockSpec((D, tn), lambda i, j: (0, j))],
        out_specs=pl.BlockSpec((tm, tn), lambda i, j: (i, j)),
        out_shape=jax.ShapeDtypeStruct((N, W), F32),
        scratch_shapes=[pltpu.VMEM((tm, D), BF16)],
        compiler_params=_params(("parallel", "arbitrary")),
        name="inproj",
    )(x, mod, gain, w)


def _rwkv_prep_kernel(z_ref, zp_ref, zn_ref, mu_ref, kkp_ref, ka_ref, rk_ref, w0_ref, a0_ref,
                      w2_ref, a2_ref, g2_ref, seg_ref,
                      fw_o, bw_o, g_o, bon_o, *, tm, blocks_per_seq):
    i = pl.program_id(0)
    z = z_ref[...]
    row = lax.broadcasted_iota(jnp.int32, (tm, 1), 0)
    first = (i % blocks_per_seq) == 0
    last = (i % blocks_per_seq) == blocks_per_seq - 1
    pz = jnp.where(first, 0.0, zp_ref[7:8, :])
    nz = jnp.where(last, 0.0, zn_ref[0:1, :])
    prev = jnp.where(row == 0, pz, pltpu.roll(z, 1, 0))
    nxt = jnp.where(row == tm - 1, nz, pltpu.roll(z, tm - 1, 0))
    zm = z + mu_ref[...] * (0.5 * (prev + nxt) - z)

    Wd = RWKV_WIDTH
    r = zm[:, 0:Wd]
    k = zm[:, Wd:2 * Wd]
    v = zm[:, 2 * Wd:3 * Wd]
    lw = zm[:, 3 * Wd:3 * Wd + 128]
    la = zm[:, 3 * Wd + 128:3 * Wd + 256]
    lg = zm[:, 3 * Wd + 256:3 * Wd + 384]
    seg = seg_ref[...]

    kk = k * kkp_ref[...]
    kk = kk / jnp.maximum(jnp.sqrt(_seg_sum(kk * kk, seg)), 1e-12)
    w_raw = _mm3(_split(jnp.tanh(lw)), _split(w2_ref[...])) + w0_ref[...]
    logw = -_sigmoid(w_raw) * 0.6065306597126334
    a = _sigmoid(_mm3(_split(la), _split(a2_ref[...])) + a0_ref[...])
    ka = ka_ref[...]
    rk = rk_ref[...]
    bonus = jnp.zeros_like(r)
    for d, dir_o in enumerate((fw_o, bw_o)):
        a_d = a[:, d * Wd:(d + 1) * Wd]
        k_d = k * (1.0 + (a_d - 1.0) * ka)
        dir_o[:, 0:Wd] = r
        dir_o[:, Wd:2 * Wd] = v
        dir_o[:, 2 * Wd:3 * Wd] = kk
        dir_o[:, 3 * Wd:4 * Wd] = logw[:, d * Wd:(d + 1) * Wd]
        dir_o[:, 4 * Wd:5 * Wd] = k_d
        dir_o[:, 5 * Wd:6 * Wd] = kk * a_d
        bonus = bonus + _seg_sum(r * k_d * rk, seg) * v
    bon_o[...] = bonus
    g_o[...] = _mm3(_split(_sigmoid(lg)), _split(g2_ref[...]))


def _rwkv_prep_call(zr, seq_len, lw):
    N, Wz = zr.shape
    tm = ROW_TILE
    bps = seq_len // tm
    Wd = RWKV_WIDTH
    nb8 = N // 8
    row_spec = pl.BlockSpec((tm, Wd), lambda i: (i, 0))
    out = jax.ShapeDtypeStruct((N, Wd), F32)
    return pl.pallas_call(
        functools.partial(_rwkv_prep_kernel, tm=tm, blocks_per_seq=bps),
        grid=(N // tm,),
        in_specs=[pl.BlockSpec((tm, Wz), lambda i: (i, 0)),
                  pl.BlockSpec((8, Wz), lambda i: (jnp.maximum(i * (tm // 8) - 1, 0), 0)),
                  pl.BlockSpec((8, Wz), lambda i: (jnp.minimum((i + 1) * (tm // 8), nb8 - 1), 0)),
                  _const_spec((1, Wz)), _const_spec((1, Wd)), _const_spec((1, Wd)), _const_spec((1, Wd)),
                  _const_spec((1, 2 * Wd)), _const_spec((1, 2 * Wd)),
                  _const_spec((128, 2 * Wd)), _const_spec((128, 2 * Wd)), _const_spec((128, Wd)),
                  _const_spec((Wd, Wd))],
        out_specs=[pl.BlockSpec((tm, SCAN_PACK), lambda i: (i, 0))] * 2 + [row_spec] * 2,
        out_shape=[jax.ShapeDtypeStruct((N, SCAN_PACK), F32)] * 2 + [out] * 2,
        compiler_params=_params(("parallel",)),
        name="rwkv_prep",
    )(zr, zr, zr, lw["mu"], lw["kk"], lw["ka"], lw["rk"], lw["w0"], lw["a0"],
      lw["w2"], lw["a2"], lw["g2"], lw["seg64"])


def _tri_masks(d):
    ti = lax.broadcasted_iota(jnp.int32, (CHUNK, CHUNK), 0)
    si = lax.broadcasted_iota(jnp.int32, (CHUNK, CHUNK), 1)
    if d == 0:
        return si <= ti, si < ti
    return si >= ti, si > ti


def _rwkv_chunks(seqs):
    C = CHUNK
    NP = RWKV_HEADS // 2
    Wd = RWKV_WIDTH
    sls = [slice(q * LANES, (q + 1) * LANES) for q in range(NP)]
    ti = lax.broadcasted_iota(jnp.int32, (C, LANES), 0)
    si = lax.broadcasted_iota(jnp.int32, (C, LANES), 1) % RWKV_HEAD_DIM
    low = lax.broadcasted_iota(jnp.int32, (C, LANES), 1) < RWKV_HEAD_DIM
    incl2 = (si <= ti, si >= ti)
    strict2 = (si < ti, si > ti)
    eye = (si == ti).astype(F32)

    def bd(x):
        zero = jnp.zeros_like(x)
        return jnp.concatenate([jnp.where(low, x, zero), jnp.where(low, zero, x)], axis=0)

    def bd2(x):
        return bd(x[0]), bd(x[1])

    base = []
    for x_ref, d, _, _ in seqs:
        incl, _ = _tri_masks(d)
        r, v, kk, lw, k, beta = (x_ref[:, j * Wd:(j + 1) * Wd] for j in range(6))
        b = _cumsum_rows(incl, lw)
        btot = jnp.sum(lw, axis=0, keepdims=True)
        nb = jnp.exp(-b)
        eb = jnp.exp(btot - b)
        base.append(dict(
            ar=_split(jnp.concatenate([-kk * jnp.exp(b - lw), r * jnp.exp(b)], axis=0)),
            bbar=_split(beta * nb), kbar=(k * nb).astype(BF16),
            hat=jnp.concatenate([beta * eb, k * eb], axis=0).astype(BF16),
            v16=v.astype(BF16), gtot=jnp.exp(btot)))

    units = [(n, q) for n in range(len(seqs)) for q in range(NP)]
    dirs = [seqs[n][1] for n, _ in units]

    def col(name, part=None):
        vals = [base[n][name] for n, _ in units]
        if part is not None:
            vals = [x[part] for x in vals]
        return [x[:, sls[q]] for x, (_, q) in zip(vals, units)]

    ar_h, ar_l, v16, hat = col("ar", 0), col("ar", 1), col("v16"), col("hat")
    gtot = col("gtot")
    g1 = [_mm3s((a, b), bd2((c, e)), _NT) for a, b, c, e in zip(ar_h, ar_l, col("bbar", 0), col("bbar", 1))]
    g2 = [_dg(a, bd(kb), _NT) for a, kb in zip(ar_h, col("kbar"))]
    a_ab = [jnp.where(strict2[d], g[:C], 0.0) for g, d in zip(g1, dirs)]
    a_rb = [jnp.where(incl2[d], g[C:], 0.0).astype(BF16) for g, d in zip(g1, dirs)]
    a_ak = [jnp.where(strict2[d], g[:C], 0.0).astype(BF16) for g, d in zip(g2, dirs)]
    a_rk = [jnp.where(incl2[d], g[C:], 0.0).astype(BF16) for g, d in zip(g2, dirs)]
    vbd = [bd(x) for x in v16]
    av = [_dg(a, vv, _NN) for a, vv in zip(a_ak, vbd)]
    t = [eye + a for a in a_ab]
    p = [_split(a) for a in a_ab]
    p = [_split(_mm3s(pp, bd2(pp))) for pp in p]
    for _ in range(4):
        ts = [_split(tt) for tt in t]
        res = [_mm3s((jnp.concatenate([pp[0], tt[0]], axis=0), jnp.concatenate([pp[1], tt[1]], axis=0)),
                     bd2(pp)) for pp, tt in zip(p, ts)]
        p = [_split(rr[:C]) for rr in res]
        t = [tt + rr[C:] for tt, rr in zip(t, res)]
    t = [_split(tt + _mm3s(_split(tt), bd2(pp))) for tt, pp in zip(t, p)]

    s_old = [seqs[n][3][q] for n, q in units]
    z = [_dg(a, bd(s.astype(BF16)), _NT) for a, s in zip(ar_h, s_old)]
    u = [_mm3s(tt, bd2(_split(zz[:C] + aa))).astype(BF16) for tt, zz, aa in zip(t, z, av)]
    ys = [zz[C:] + _dg(jnp.concatenate([rb, rk], axis=1), jnp.concatenate([bd(uu), vv], axis=0), _NN)
          for zz, rb, rk, uu, vv in zip(z, a_rb, a_rk, u, vbd)]
    cross = [_dg(jnp.concatenate([uu, vv], axis=0), hh, _TN)
             for uu, vv, hh in zip(u, v16, hat)]
    stores = []
    for (n, q), yy, s, g, cc in zip(units, ys, s_old, gtot, cross):
        stores.append((seqs[n][2], (slice(None), sls[q]), yy))
        stores.append((seqs[n][3], (q,), s * g + jnp.where(low, cc[:C], cc[C:])))
    return stores


def _gla_chunks(seqs, a2_ref, ab_ref):
    Kw = GLA_K_WIDTH
    lora = 2 * Kw + 2 * GLA_V_WIDTH
    base = []
    for x_ref, d, _, _ in seqs:
        incl, _ = _tri_masks(d)
        xa = (_mm3(_split(x_ref[:, lora:lora + LANES]), _split(a2_ref[:, d * Kw:(d + 1) * Kw]))
              + ab_ref[:, d * Kw:(d + 1) * Kw])
        log_a = (jnp.minimum(xa, 0.0) - jnp.log1p(jnp.exp(-jnp.abs(xa)))) * (1.0 / GLA_GATE_NORMALIZER)
        b = _cumsum_rows(incl, log_a)
        bl = jnp.sum(log_a, axis=0, keepdims=True)
        q = x_ref[:, 0:Kw] * (GLA_DK ** -0.5)
        k = x_ref[:, Kw:2 * Kw]
        base.append(((q * jnp.exp(b)).astype(BF16), (k * jnp.exp(-b)).astype(BF16),
                     (k * jnp.exp(bl - b)).astype(BF16), jnp.exp(bl), incl))

    units = [(n, h) for n in range(len(seqs)) for h in range(GLA_HEADS)]
    ks = [slice(h * GLA_DK, (h + 1) * GLA_DK) for _, h in units]
    vs = [slice(2 * Kw + h * GLA_DV, 2 * Kw + (h + 1) * GLA_DV) for _, h in units]
    q_in = [base[n][0][:, c] for (n, _), c in zip(units, ks)]
    v_h = [seqs[n][0][:, c].astype(BF16) for (n, _), c in zip(units, vs)]
    att = [jnp.where(base[n][4], _dg(qq, base[n][1][:, c], _NT), 0.0).astype(BF16)
           for (n, _), qq, c in zip(units, q_in, ks)]
    o_in = [_dg(a, vv, _NN) for a, vv in zip(att, v_h)]
    kv = [_dg(vv, base[n][2][:, c], _TN) for (n, _), vv, c in zip(units, v_h, ks)]
    s_old = [seqs[n][3][h] for n, h in units]
    outs = [oo + _dg(qq, s.astype(BF16), _NT) for oo, qq, s in zip(o_in, q_in, s_old)]
    stores = []
    for (n, h), c, s, o, x in zip(units, ks, s_old, outs, kv):
        stores.append((seqs[n][2], (slice(None), slice(h * GLA_DV, (h + 1) * GLA_DV)), o))
        stores.append((seqs[n][3], (h,), s * base[n][3][:, c] + x))
    return stores


def _scan_kernel(fw_ref, bw_ref, s0r_ref, xf, xb, a2_ref, ab_ref, s0g_ref,
                 yf_o, yb_o, str_o, of_o, ob_o, stg_o, sr_scr, sg_scr, *, nc):
    i = pl.program_id(1)

    @pl.when(i == 0)
    def _():
        sr_scr[...] = s0r_ref[...]
        sg_scr[...] = s0g_ref[...]

    both = [(bb, d) for bb in range(SCAN_BATCH) for d in range(2)]
    stores = (_rwkv_chunks([((fw_ref, bw_ref)[d].at[bb], d, (yf_o, yb_o)[d].at[bb], sr_scr.at[bb, d])
                            for bb, d in both])
              + _gla_chunks([((xf, xb)[d].at[bb], d, (of_o, ob_o)[d].at[bb], sg_scr.at[bb, d])
                             for bb, d in both], a2_ref, ab_ref))
    for ref, idx, val in stores:
        ref[idx] = val

    @pl.when(i == nc - 1)
    def _():
        str_o[...] = sr_scr[...]
        stg_o[...] = sg_scr[...]


def _scan_call(prep, s0, zg, s0t, a2, ab, B, T):
    fw, bw = prep
    N, Wd = fw.shape[0], RWKV_WIDTH
    Wg = zg.shape[1]
    nc = T // CHUNK
    BB = SCAN_BATCH
    H, Dh = RWKV_HEADS, RWKV_HEAD_DIM
    pair_shape = (2, H // 2, Dh, 2 * Dh)
    gla_shape = (2, GLA_HEADS, GLA_DV, GLA_DK)

    def rows(width, backward):
        if backward:
            return pl.BlockSpec((BB, CHUNK, width), lambda b, i: (b, nc - 1 - i, 0))
        return pl.BlockSpec((BB, CHUNK, width), lambda b, i: (b, i, 0))

    st_r = pl.BlockSpec((BB,) + pair_shape, lambda b, i: (b, 0, 0, 0, 0))
    st_g = pl.BlockSpec((BB,) + gla_shape, lambda b, i: (b, 0, 0, 0, 0))
    y = jax.ShapeDtypeStruct((B, T, Wd), F32)
    o = jax.ShapeDtypeStruct((B, T, GLA_V_WIDTH), F32)
    s0p = s0.reshape(B, 2, H // 2, 2, Dh, Dh).transpose(0, 1, 2, 4, 3, 5).reshape((B,) + pair_shape)
    zg3 = zg.reshape(B, T, Wg)
    yf, yb, stp, gf_out, gb_out, stg = pl.pallas_call(
        functools.partial(_scan_kernel, nc=nc),
        grid=(B // BB, nc),
        in_specs=[rows(SCAN_PACK, False), rows(SCAN_PACK, True), st_r,
                  rows(Wg, False), rows(Wg, True),
                  _const_spec((LANES, 2 * GLA_K_WIDTH)), _const_spec((1, 2 * GLA_K_WIDTH)), st_g],
        out_specs=[rows(Wd, False), rows(Wd, True), st_r,
                   rows(GLA_V_WIDTH, False), rows(GLA_V_WIDTH, True), st_g],
        out_shape=[y, y, jax.ShapeDtypeStruct((B,) + pair_shape, F32),
                   o, o, jax.ShapeDtypeStruct((B,) + gla_shape, F32)],
        scratch_shapes=[pltpu.VMEM((BB,) + pair_shape, F32), pltpu.VMEM((BB,) + gla_shape, F32)],
        compiler_params=_params(("parallel", "arbitrary")),
        name="scans",
    )(fw.reshape(B, T, SCAN_PACK), bw.reshape(B, T, SCAN_PACK), s0p, zg3, zg3, a2, ab, s0t)
    st_out = stp.reshape(B, 2, H // 2, Dh, 2, Dh).transpose(0, 1, 2, 4, 3, 5).reshape(B, 2, H, Dh, Dh)
    return (yf.reshape(N, Wd), yb.reshape(N, Wd), st_out,
            gf_out.reshape(N, GLA_V_WIDTH), gb_out.reshape(N, GLA_V_WIDTH), stg)


def _rope(x, cos, sin):
    n = x.shape[1]
    lane = lax.broadcasted_iota(jnp.int32, x.shape, 1)
    up = pltpu.roll(x, n - 16, 1)
    dn = pltpu.roll(x, 16, 1)
    sw = jnp.where((lane % 32) < 16, up, dn)
    reps = n // LANES
    if reps > 1:
        cos = jnp.concatenate([cos] * reps, axis=1)
        sin = jnp.concatenate([sin] * reps, axis=1)
    return x * cos + sw * sin


def _attn_prep_kernel(*refs, rope):
    if rope:
        z_ref, qn_ref, kn_ref, segq_ref, segk_ref, cos_ref, sin_ref, q_o, kn_o, kr_o, ve_o = refs
    else:
        z_ref, qn_ref, kn_ref, segq_ref, segk_ref, q_o, kn_o, kr_o, ve_o = refs
    zq = z_ref[:, 0:ATT_WIDTH]
    zk = z_ref[:, ATT_WIDTH:ATT_WIDTH + KV_WIDTH]
    qh = zq * lax.rsqrt(_seg_sum(zq * zq, segq_ref[...]) * (1.0 / HEAD_DIM) + NORM_EPS) * qn_ref[...]
    kh = zk * lax.rsqrt(_seg_sum(zk * zk, segk_ref[...]) * (1.0 / HEAD_DIM) + NORM_EPS) * kn_ref[...]
    kn_o[...] = kh
    if rope:
        qh = _rope(qh, cos_ref[...], sin_ref[...])
        kh = _rope(kh, cos_ref[...], sin_ref[...])
    q_o[...] = (qh * (LOG2_E * HEAD_DIM ** -0.5)).astype(BF16)
    kr_o[...] = kh.astype(BF16)
    v = z_ref[:, ATT_WIDTH + KV_WIDTH:ATT_WIDTH + 2 * KV_WIDTH]
    group = lax.broadcasted_iota(jnp.int32, v.shape, 1) // HEAD_DIM
    for g in range(KV_HEADS):
        ve_o[:, g * LANES:(g + 1) * LANES] = jnp.where(group == g, v, 1.0).astype(BF16)


def _attn_prep_call(za, lw, seq_len, rope_tabs):
    N, Wz = za.shape
    tm = ROW_TILE
    bps = seq_len // tm
    rope = rope_tabs is not None
    in_specs = [pl.BlockSpec((tm, Wz), lambda i: (i, 0)),
                _const_spec((1, ATT_WIDTH)), _const_spec((1, KV_WIDTH)),
                _const_spec((ATT_WIDTH, ATT_WIDTH)), _const_spec((KV_WIDTH, KV_WIDTH))]
    args = [za, lw["qn"], lw["kn"], lw["seg64"], lw["seg64"][:KV_WIDTH, :KV_WIDTH]]
    if rope:
        in_specs += [pl.BlockSpec((tm, LANES), lambda i: (i % bps, 0))] * 2
        args += list(rope_tabs)
    return pl.pallas_call(
        functools.partial(_attn_prep_kernel, rope=rope),
        grid=(N // tm,),
        in_specs=in_specs,
        out_specs=[pl.BlockSpec((tm, ATT_WIDTH), lambda i: (i, 0)),
                   pl.BlockSpec((tm, KV_WIDTH), lambda i: (i, 0)),
                   pl.BlockSpec((tm, KV_WIDTH), lambda i: (i, 0)),
                   pl.BlockSpec((tm, KV_HEADS * LANES), lambda i: (i, 0))],
        out_shape=[jax.ShapeDtypeStruct((N, ATT_WIDTH), BF16),
                   jax.ShapeDtypeStruct((N, KV_WIDTH), F32),
                   jax.ShapeDtypeStruct((N, KV_WIDTH), BF16),
                   jax.ShapeDtypeStruct((N, KV_HEADS * LANES), BF16)],
        compiler_params=_params(("parallel",)),
        name="attn_prep",
    )(*args)


def _attn_kernel(*refs, cached):
    if cached:
        q_ref, k_ref, v_ref, ck_ref, cv_ref, o_ref = refs
    else:
        q_ref, k_ref, v_ref, o_ref = refs
    G = N_HEADS // KV_HEADS
    tq = q_ref.shape[0]
    for g in range(KV_HEADS):
        gs = slice(g * HEAD_DIM, (g + 1) * HEAD_DIM)
        other = slice((1 - g) * HEAD_DIM, (2 - g) * HEAD_DIM)
        kg = k_ref[:, gs]
        vg = v_ref[:, g * LANES:(g + 1) * LANES]
        if cached:
            cv = cv_ref[...]
            own = (lax.broadcasted_iota(jnp.int32, cv.shape, 1) // HEAD_DIM) == g
            ckg = ck_ref[:, gs].astype(BF16)
            cvg = jnp.where(own, cv, 1.0).astype(BF16)
        for h0 in range(g * G, (g + 1) * G, ATT_STACK):
            heads = range(h0, h0 + ATT_STACK)
            qs = jnp.concatenate([q_ref[:, h * HEAD_DIM:(h + 1) * HEAD_DIM] for h in heads], axis=0)
            chunks = [(kg[c:c + KEY_CHUNK], vg[c:c + KEY_CHUNK]) for c in range(0, kg.shape[0], KEY_CHUNK)]
            if cached:
                chunks.append((ckg, cvg))
            m = None
            for kc, _ in chunks:
                mc = jnp.max(_dg(qs, kc, _NT), axis=-1, keepdims=True)
                m = mc if m is None else jnp.maximum(m, mc)
            oe = None
            for kc, vc in chunks:
                part = _dg(jnp.exp2(_dg(qs, kc, _NT) - m).astype(BF16), vc, _NN)
                oe = part if oe is None else oe + part
            o = oe[:, gs] / oe[:, other]
            for j, h in enumerate(heads):
                o_ref[:, h * HEAD_DIM:(h + 1) * HEAD_DIM] = o[j * tq:(j + 1) * tq]


def _attn_call(q, k, vext, cache, B, T):
    N = q.shape[0]
    tq = Q_TILE
    nq = T // tq
    in_specs = [pl.BlockSpec((tq, ATT_WIDTH), lambda b, i: (b * nq + i, 0)),
                pl.BlockSpec((T, KV_WIDTH), lambda b, i: (b, 0)),
                pl.BlockSpec((T, KV_HEADS * LANES), lambda b, i: (b, 0))]
    args = [q, k, vext]
    if cache is not None:
        ck, cv, layer = cache
        P = ck.shape[2]
        cspec = pl.BlockSpec((None, None, P, KV_WIDTH), lambda b, i: (b, layer, 0, 0))
        in_specs += [cspec, cspec]
        args += [ck, cv]
    return pl.pallas_call(
        functools.partial(_attn_kernel, cached=cache is not None),
        grid=(B, nq),
        in_specs=in_specs,
        out_specs=pl.BlockSpec((tq, ATT_WIDTH), lambda b, i: (b * nq + i, 0)),
        out_shape=jax.ShapeDtypeStruct((N, ATT_WIDTH), F32),
        compiler_params=_params(("parallel", "arbitrary")),
        name="attention",
    )(*args)


def _merge_kernel(yf, yb, bon, g, oa, gf, gb, gr, mr, ma, mg, x_ref, m_ref,
                  lng, lnb, gn, nf, seg_ref, pr, pa, pg, wo, rw, rb,
                  x1_o, h2_o, ti_o, tg_o):
    seg = seg_ref[...]
    inv = 1.0 / RWKV_HEAD_DIM
    y = yf[...] + yb[...]
    mu = _seg_sum(y, seg) * inv
    yc = y - mu
    var = _seg_sum(yc * yc, seg) * inv
    o_r = (yc * lax.rsqrt(var + RWKV_GN_EPS) * lng[...] + lnb[...] + bon[...]) * g[...]

    gate_r = gr[...]
    silu_r = gate_r * _sigmoid(gate_r)
    gnv = gn[...]
    cols = []
    for h in range(GLA_HEADS):
        vs = slice(h * GLA_DV, (h + 1) * GLA_DV)
        o = gf[:, vs] + gb[:, vs]
        ms = jnp.mean(o * o, axis=-1, keepdims=True)
        cols.append(o * lax.rsqrt(ms + NORM_EPS) * gnv * silu_r[:, vs])
    o_g = jnp.concatenate(cols, axis=1)

    merged = (_sigmoid(mr[...]) * _dot(o_r.astype(BF16), pr[...])
              + _sigmoid(ma[...]) * _dot(oa[...].astype(BF16), pa[...])
              + _sigmoid(mg[...]) * _dot(o_g.astype(BF16), pg[...]))
    x1 = x_ref[...] + m_ref[2:3, :] * _dot(merged.astype(BF16), wo[...])
    x1_o[...] = x1
    hn = x1 * lax.rsqrt(jnp.mean(x1 * x1, axis=-1, keepdims=True) + NORM_EPS) * nf[...]
    h2 = hn * (1.0 + m_ref[4:5, :]) + m_ref[3:4, :]
    h2_o[...] = h2

    logits = _mm3(_split(h2), _split(rw[...])) + rb[...]
    lane = lax.broadcasted_iota(jnp.int32, logits.shape, 1)
    vals, idxs = [], []
    for _ in range(TOP_K):
        m = jnp.max(logits, axis=-1, keepdims=True)
        idx = jnp.min(jnp.where(logits == m, lane, LANES), axis=-1, keepdims=True)
        vals.append(m)
        idxs.append(idx)
        logits = jnp.where(lane == idx, -jnp.inf, logits)
    es = [jnp.exp(vv - vals[0]) for vv in vals]
    den = es[0] + es[1] + es[2] + es[3]
    ti = jnp.zeros(logits.shape, jnp.int32)
    tg = jnp.zeros(logits.shape, F32)
    for j in range(TOP_K):
        ti = jnp.where(lane == j, idxs[j], ti)
        tg = jnp.where(lane == j, es[j] / den, tg)
    ti_o[...] = ti
    tg_o[...] = tg


def _merge_call(yf, yb, bon, g, oa, gf, gb, zg, zm, x, mod, lw, rows_per_cond):
    N, D = x.shape
    tm = ROW_TILE
    Wd = RWKV_WIDTH

    def rows(w, col=0):
        return pl.BlockSpec((tm, w), lambda i: (i, col))

    in_specs = [rows(Wd)] * 4 + [rows(ATT_WIDTH), rows(GLA_V_WIDTH), rows(GLA_V_WIDTH),
                                 rows(GLA_V_WIDTH, 2), rows(D, 0), rows(D, 1), rows(D, 2), rows(D),
                                 pl.BlockSpec((None, 6, D), lambda i: ((i * tm) // rows_per_cond, 0, 0)),
                                 _const_spec((1, Wd)), _const_spec((1, Wd)), _const_spec((1, GLA_DV)),
                                 _const_spec((1, D)), _const_spec((Wd, Wd)),
                                 _const_spec((Wd, D)), _const_spec((ATT_WIDTH, D)), _const_spec((GLA_V_WIDTH, D)),
                                 _const_spec((D, D)), _const_spec((D, LANES)), _const_spec((1, LANES))]
    return pl.pallas_call(
        _merge_kernel,
        grid=(N // tm,),
        in_specs=in_specs,
        out_specs=[rows(D), rows(D), rows(LANES), rows(LANES)],
        out_shape=[jax.ShapeDtypeStruct((N, D), F32), jax.ShapeDtypeStruct((N, D), F32),
                   jax.ShapeDtypeStruct((N, LANES), jnp.int32), jax.ShapeDtypeStruct((N, LANES), F32)],
        compiler_params=_params(("parallel",)),
        name="merge",
    )(yf, yb, bon, g, oa, gf, gb, zg, zm, zm, zm, x, mod,
      lw["ln_g"], lw["ln_b"], lw["gla_norm"], lw["norm_ffn"], lw["seg64"],
      lw["p_rwkv"], lw["p_attn"], lw["p_gla"], lw["w_out"], lw["router_w"], lw["router_b"])


def _moe_kernel(be_ref, na_ref, x_ref, gate_ref, w1_ref, b1_ref, w2_ref, b2_ref, o_ref, w1_scr, w2_scr):
    i = pl.program_id(0)

    @pl.when(jnp.logical_or(i == 0, be_ref[i] != be_ref[jnp.maximum(i - 1, 0)]))
    def _():
        w1_scr[...] = w1_ref[...].astype(BF16)
        w2_scr[...] = w2_ref[...].astype(BF16)

    @pl.when(i < na_ref[0])
    def _():
        x = x_ref[...].astype(BF16)
        acc = None
        for c in range(0, D_EXPERT, EXPERT_F_CHUNK):
            glu_c = slice(c, c + EXPERT_F_CHUNK)
            lin_c = slice(D_EXPERT + c, D_EXPERT + c + EXPERT_F_CHUNK)
            glu = jnp.minimum(_dot(x, w1_scr[:, glu_c]) + b1_ref[:, glu_c], SWIGLU_LIMIT)
            lin = jnp.clip(_dot(x, w1_scr[:, lin_c]) + b1_ref[:, lin_c], -SWIGLU_LIMIT, SWIGLU_LIMIT)
            act = glu * _sigmoid(SWIGLU_ALPHA * glu) * (lin + 1.0)
            part = _dot(act.astype(BF16), w2_scr[glu_c, :])
            acc = part if acc is None else acc + part
        o_ref[...] = (acc + b2_ref[...]) * gate_ref[...]

    @pl.when(i >= na_ref[0])
    def _():
        o_ref[...] = jnp.zeros_like(o_ref)


def _moe_call(xg, row_gate, block_exp, n_active, layer, w1, b1, w2, b2):
    R, D = xg.shape
    tb = EXPERT_BLOCK
    F2 = w1.shape[3]
    grid_spec = pltpu.PrefetchScalarGridSpec(
        num_scalar_prefetch=2,
        grid=(R // tb,),
        in_specs=[pl.BlockSpec((tb, D), lambda i, be, na: (i, 0)),
                  pl.BlockSpec((tb, 1), lambda i, be, na: (i, 0)),
                  pl.BlockSpec((None, None, D, F2), lambda i, be, na: (layer, be[i], 0, 0)),
                  pl.BlockSpec((None, 1, F2), lambda i, be, na: (be[i], 0, 0)),
                  pl.BlockSpec((None, None, F2 // 2, D), lambda i, be, na: (layer, be[i], 0, 0)),
                  pl.BlockSpec((None, 1, D), lambda i, be, na: (be[i], 0, 0))],
        out_specs=pl.BlockSpec((tb, D), lambda i, be, na: (i, 0)),
        scratch_shapes=[pltpu.VMEM((D, F2), BF16), pltpu.VMEM((F2 // 2, D), BF16)],
    )
    return pl.pallas_call(
        _moe_kernel,
        grid_spec=grid_spec,
        out_shape=jax.ShapeDtypeStruct((R, D), F32),
        compiler_params=_params(("arbitrary",)),
        name="moe_experts",
    )(block_exp, n_active, xg, row_gate, w1, b1, w2, b2)


def _combine_kernel(x1_ref, m_ref, y0_ref, y1_ref, y2_ref, y3_ref, o_ref):
    y = (y0_ref[...] + y1_ref[...]) + (y2_ref[...] + y3_ref[...])
    o_ref[...] = x1_ref[...] + m_ref[5:6, :] * y


def _combine_call(x1, mod, ys, rows_per_cond):
    N, D = x1.shape
    tm = ROW_TILE
    rows = pl.BlockSpec((tm, D), lambda i: (i, 0))
    return pl.pallas_call(
        _combine_kernel,
        grid=(N // tm,),
        in_specs=([rows, pl.BlockSpec((None, 6, D), lambda i: ((i * tm) // rows_per_cond, 0, 0))]
                  + [pl.BlockSpec((None, tm, D), functools.partial(lambda i, j: (j, i, 0), j=j))
                     for j in range(TOP_K)]),
        out_specs=rows,
        out_shape=jax.ShapeDtypeStruct((N, D), F32),
        compiler_params=_params(("parallel",)),
        name="combine",
    )(x1, mod, *([ys] * TOP_K))


def _moe_ffn(h2, top_i, top_g, lw):
    N, D = h2.shape
    n_as = N * TOP_K
    n_blocks = -(-n_as // EXPERT_BLOCK) + N_EXPERTS
    R = n_blocks * EXPERT_BLOCK
    ex = jnp.arange(N_EXPERTS, dtype=jnp.int32)

    def lut(onehot, table):
        return jnp.sum(jnp.where(onehot, table[None, :], 0), axis=1)

    flat_e = top_i.reshape(n_as)
    flat_g = top_g.reshape(n_as)
    oh_a = flat_e[:, None] == ex[None, :]
    counts = jnp.sum(oh_a, axis=0, dtype=jnp.int32)
    starts = jnp.cumsum(counts) - counts
    padded = (counts + EXPERT_BLOCK - 1) // EXPERT_BLOCK * EXPERT_BLOCK
    pends = jnp.cumsum(padded)
    pstarts = pends - padded
    _, order, gate_sorted = lax.sort((flat_e, jnp.arange(n_as, dtype=jnp.int32), flat_g), num_keys=1, is_stable=True)
    rank = jnp.argsort(order).astype(jnp.int32)
    pos = rank + lut(oh_a, pstarts - starts)
    blk_start = jnp.arange(n_blocks, dtype=jnp.int32) * EXPERT_BLOCK
    block_exp = jnp.minimum(jnp.sum(pends[None, :] <= blk_start[:, None], axis=1),
                            N_EXPERTS - 1).astype(jnp.int32)
    n_active = (pends[-1] // EXPERT_BLOCK).astype(jnp.int32).reshape(1)
    oh_p = jnp.repeat(block_exp, EXPERT_BLOCK)[:, None] == ex[None, :]
    idx = jnp.arange(R, dtype=jnp.int32) - lut(oh_p, pstarts)
    valid = (idx >= 0) & (idx < lut(oh_p, counts))
    table = jnp.stack([order, lax.bitcast_convert_type(gate_sorted, jnp.int32)], axis=1)
    picked = table[jnp.clip(lut(oh_p, starts) + idx, 0, n_as - 1)]
    row_tok = jnp.where(valid, picked[:, 0] // TOP_K, 0)
    row_gate = jnp.where(valid, lax.bitcast_convert_type(picked[:, 1], F32), 0.0)
    out = _moe_call(h2[row_tok], row_gate.reshape(R, 1), block_exp, n_active,
                    lw["layer"], lw["moe_w1"], lw["moe_b1"], lw["moe_w2"], lw["moe_b2"])
    return out[pos.reshape(N, TOP_K).T.reshape(n_as)].reshape(TOP_K, N, D)


def _seg_matrix(n, seg):
    idx = jnp.arange(n) // seg
    return (idx[:, None] == idx[None, :]).astype(BF16)


def _rope_tables(T):
    quarter = HEAD_DIM // 4
    inv_freq = ROPE_THETA ** (-jnp.arange(quarter, dtype=F32) / quarter)
    t = jnp.arange(T)
    row = (t // GRID_W).astype(F32)
    col = (t % GRID_W).astype(F32)
    ang_r = row[:, None] * inv_freq
    ang_c = col[:, None] * inv_freq
    cos = jnp.concatenate([jnp.cos(ang_r)] * 2 + [jnp.cos(ang_c)] * 2, axis=1)
    sin = jnp.concatenate([-jnp.sin(ang_r), jnp.sin(ang_r), -jnp.sin(ang_c), jnp.sin(ang_c)], axis=1)
    return jnp.concatenate([cos, cos], axis=1), jnp.concatenate([sin, sin], axis=1)


def _layer_weights(p, l):
    Wd = RWKV_WIDTH
    w_in = p["w_in"][l]
    c0 = RWKV_COLS
    aq, ak, av, gq, gk, gv, gl, gr = (c0, c0 + 512, c0 + 640, c0 + 768, c0 + 1280, c0 + 1792, c0 + 2816, c0 + 2848)
    m0 = gr + GLA_V_WIDTH
    D = D_MODEL
    w_g = jnp.concatenate([w_in[:, gq:gl], w_in[:, gr:m0], w_in[:, gl:gr],
                           jnp.zeros((D, LANES - 2 * GLA_LORA), F32)], axis=1)
    z64 = jnp.zeros((DECAY_LORA, Wd), F32)
    w2 = jnp.concatenate([jnp.concatenate([p["rwkv_w2"][l, 0], z64], axis=1),
                          jnp.concatenate([z64, p["rwkv_w2"][l, 1]], axis=1)], axis=0)
    a2 = jnp.concatenate([jnp.concatenate([p["rwkv_a2"][l, 0], z64], axis=1),
                          jnp.concatenate([z64, p["rwkv_a2"][l, 1]], axis=1)], axis=0)
    z16 = jnp.zeros((GLA_LORA, GLA_K_WIDTH), F32)
    ga2 = jnp.concatenate([jnp.concatenate([p["gla_a2"][l, 0], z16], axis=1),
                           jnp.concatenate([z16, p["gla_a2"][l, 1]], axis=1),
                           jnp.zeros((LANES - 2 * GLA_LORA, 2 * GLA_K_WIDTH), F32)], axis=0)
    return {
        "w_r": w_in[:, :c0].astype(BF16), "w_a": w_in[:, aq:gq].astype(BF16),
        "w_g": w_g.astype(BF16), "w_m": w_in[:, m0:].astype(BF16),
        "norm_mix": p["norm_mix"][l].reshape(1, D), "norm_ffn": p["norm_ffn"][l].reshape(1, D),
        "mu": p["rwkv_mu"][l].reshape(1, c0),
        "kk": p["rwkv_kk"][l].reshape(1, Wd), "ka": p["rwkv_ka"][l].reshape(1, Wd),
        "rk": p["rwkv_rk"][l].reshape(1, Wd),
        "w0": p["rwkv_w0"][l].reshape(1, 2 * Wd), "a0": p["rwkv_a0"][l].reshape(1, 2 * Wd),
        "w2": w2, "a2": a2, "g2": p["rwkv_g2"][l],
        "ln_g": p["rwkv_ln_g"][l].reshape(1, Wd), "ln_b": p["rwkv_ln_b"][l].reshape(1, Wd),
        "seg64": _seg_matrix(Wd, RWKV_HEAD_DIM),
        "qn": jnp.tile(p["attn_qn"][l], N_HEADS).reshape(1, ATT_WIDTH),
        "kn": jnp.tile(p["attn_kn"][l], KV_HEADS).reshape(1, KV_WIDTH),
        "gla_a2": ga2, "gla_ab": p["gla_ab"][l].reshape(1, 2 * GLA_K_WIDTH),
        "gla_norm": p["gla_norm"][l].reshape(1, GLA_DV),
        "p_rwkv": p["p_rwkv"][l].astype(BF16), "p_attn": p["p_attn"][l].astype(BF16),
        "p_gla": p["p_gla"][l].astype(BF16), "w_out": p["w_out"][l].astype(BF16),
        "router_w": jnp.concatenate([p["router_w"][l], jnp.zeros((D, LANES - N_EXPERTS), F32)], axis=1),
        "router_b": jnp.concatenate([p["router_b"][l], jnp.full((LANES - N_EXPERTS,), -jnp.inf, F32)]).reshape(1, LANES),
        "layer": l, "moe_w1": p["moe_w1"], "moe_b1": p["moe_b1"][l].reshape(N_EXPERTS, 1, 2 * D_EXPERT),
        "moe_w2": p["moe_w2"], "moe_b2": p["moe_b2"][l].reshape(N_EXPERTS, 1, D_MODEL),
    }


def _trunk_layer(x, mod, lw, B, T, ctx):
    N = B * T
    rpc = N if mod.shape[0] == 1 else T
    zr, za, zg, zm = (_inproj_call(x, mod, lw["norm_mix"], lw[name], rpc, lw[name].shape[1])
                      for name in ("w_r", "w_a", "w_g", "w_m"))

    fw, bw, g, bon = _rwkv_prep_call(zr, T, lw)
    if ctx is None:
        s0r = jnp.zeros((B, 2, RWKV_HEADS, RWKV_HEAD_DIM, RWKV_HEAD_DIM), F32)
        s0g = jnp.zeros((B, 2, GLA_HEADS, GLA_DV, GLA_DK), F32)
        cache, tabs = None, None
    else:
        s0r = ctx["rwkv"]
        s0g = jnp.swapaxes(ctx["gla"], -1, -2)
        cache, tabs = (ctx["k"], ctx["v"], ctx["layer"]), _rope_tables(T)
    yf, yb, s_r, gf, gb, s_g = _scan_call((fw, bw), s0r, zg, s0g, lw["gla_a2"], lw["gla_ab"], B, T)
    q, kn, kr, vext = _attn_prep_call(za, lw, T, tabs)
    oa = _attn_call(q, kr, vext, cache, B, T)
    x1, h2, ti, tg = _merge_call(yf, yb, bon, g, oa, gf, gb, zg, zm, x, mod, lw, rpc)
    y4 = _moe_ffn(h2, ti[:, :TOP_K], tg[:, :TOP_K], lw)
    x2 = _combine_call(x1, mod, y4, rpc)
    return x2, (kn, za[:, ATT_WIDTH + KV_WIDTH:], s_r, jnp.swapaxes(s_g, -1, -2))


def kernel(x_prompt, x_sample, c, cache_k, cache_v, state_rwkv, state_gla, c_ctx, ada_w, ada_b, norm_mix, norm_ffn, w_in, rwkv_mu, rwkv_w0, rwkv_w2, rwkv_a0, rwkv_a2, rwkv_g2, rwkv_kk, rwkv_ka, rwkv_rk, rwkv_ln_g, rwkv_ln_b, attn_qn, attn_kn, gla_a2, gla_ab, gla_norm, p_rwkv, p_attn, p_gla, w_out, router_w, router_b, moe_w1, moe_b1, moe_w2, moe_b2):
    p = {"w_in": w_in, "norm_mix": norm_mix, "norm_ffn": norm_ffn, "rwkv_mu": rwkv_mu, "rwkv_w0": rwkv_w0,
         "rwkv_w2": rwkv_w2, "rwkv_a0": rwkv_a0, "rwkv_a2": rwkv_a2, "rwkv_g2": rwkv_g2, "rwkv_kk": rwkv_kk,
         "rwkv_ka": rwkv_ka, "rwkv_rk": rwkv_rk, "rwkv_ln_g": rwkv_ln_g, "rwkv_ln_b": rwkv_ln_b,
         "attn_qn": attn_qn, "attn_kn": attn_kn, "gla_a2": gla_a2, "gla_ab": gla_ab, "gla_norm": gla_norm,
         "p_rwkv": p_rwkv, "p_attn": p_attn, "p_gla": p_gla, "w_out": w_out, "router_w": router_w,
         "router_b": router_b, "moe_w1": moe_w1, "moe_b1": moe_b1, "moe_w2": moe_w2, "moe_b2": moe_b2}
    Bp, Tp, D = x_prompt.shape
    Bs, Ts, _ = x_sample.shape
    L = ada_w.shape[0]
    P = cache_k.shape[2]

    cond8 = jnp.concatenate([c_ctx[None, :], c, jnp.zeros((8 - 1 - Bs, D), F32)], axis=0)
    mod = _ada_call(cond8, ada_w, ada_b).reshape(L, 8, 6, D)
    weights = [_layer_weights(p, l) for l in range(L)]
    ck = cache_k.reshape(Bs, L, P, KV_WIDTH)
    cv = cache_v.reshape(Bs, L, P, KV_WIDTH)

    yp = x_prompt.reshape(Bp * Tp, D)
    ks, vs, srs, sgs = [], [], [], []
    for l in range(L):
        yp, (kn, vh, s_r, s_g) = _trunk_layer(yp, mod[l, 0:1], weights[l], Bp, Tp, None)
        ks.append(kn.reshape(Bp, Tp, KV_HEADS, HEAD_DIM))
        vs.append(vh.reshape(Bp, Tp, KV_HEADS, HEAD_DIM))
        srs.append(s_r)
        sgs.append(s_g)

    ys = x_sample.reshape(Bs * Ts, D)
    for l in range(L):
        ctx = {"k": ck, "v": cv, "layer": l, "rwkv": state_rwkv[:, l], "gla": state_gla[:, l]}
        ys, _ = _trunk_layer(ys, mod[l, 1:1 + Bs], weights[l], Bs, Ts, ctx)

    return (yp.reshape(Bp, Tp, D), ys.reshape(Bs, Ts, D),
            jnp.stack(ks, axis=1), jnp.stack(vs, axis=1),
            jnp.stack(srs, axis=1), jnp.stack(sgs, axis=1))
```

```python
import functools

import jax
import jax.numpy as jnp
from jax import lax
from jax.experimental import pallas as pl
from jax.experimental.pallas import tpu as pltpu

D_MODEL = 1024
DEPTH = 2
GRID_W = 64
NORM_EPS = 1e-6

RWKV_HEADS = 8
RWKV_HEAD_DIM = 64
RWKV_WIDTH = RWKV_HEADS * RWKV_HEAD_DIM
DECAY_LORA = 64
ICLR_LORA = 64
GATE_LORA = 128
RWKV_GN_EPS = 6.4e-4
RWKV_COLS = 3 * RWKV_WIDTH + 2 * DECAY_LORA + 2 * ICLR_LORA + GATE_LORA
SCAN_PACK = 6 * RWKV_WIDTH

N_HEADS = 8
KV_HEADS = 2
HEAD_DIM = 64
ATT_WIDTH = N_HEADS * HEAD_DIM
KV_WIDTH = KV_HEADS * HEAD_DIM
ROPE_THETA = 10000.0

GLA_HEADS = 4
GLA_DK = 128
GLA_DV = 256
GLA_K_WIDTH = GLA_HEADS * GLA_DK
GLA_V_WIDTH = GLA_HEADS * GLA_DV
GLA_LORA = 16
GLA_GATE_NORMALIZER = 16.0

N_EXPERTS = 32
TOP_K = 4
D_EXPERT = 1024
SWIGLU_ALPHA = 1.702
SWIGLU_LIMIT = 7.0
EXPERT_BLOCK = 512

LANES = 128
CHUNK = 64
SCAN_BATCH = 2
ROW_TILE = 256
INPROJ_TILE = 512
Q_TILE = 256
ATT_STACK = 2
KEY_CHUNK = 512
VMEM_LIMIT = 52 * 1024 * 1024
LOG2_E = 1.4426950408889634
EXP_MINUS_HALF = 0.6065306597126334

F32 = jnp.float32
BF16 = jnp.bfloat16


def _dot(a, b):
    return jnp.dot(a, b, preferred_element_type=F32)


_NN = (((1,), (0,)), ((), ()))
_NT = (((1,), (1,)), ((), ()))
_TN = (((0,), (0,)), ((), ()))


def _dg(a, b, dn):
    return lax.dot_general(a, b, dn, preferred_element_type=F32)


def _split(x):
    h = x.astype(BF16)
    return h, (x - h.astype(F32)).astype(BF16)


def _mm3(a, b, dn=_NN):
    return _dg(a[0], b[0], dn) + _dg(a[0], b[1], dn) + _dg(a[1], b[0], dn)


def _mm3s(a, b, dn=_NN):
    m = a[0].shape[0]
    top = _dg(jnp.concatenate([a[0], a[1]], axis=0), b[0], dn)
    return top[:m] + top[m:] + _dg(a[0], b[1], dn)


def _cumsum_rows(mask, x):
    m = jnp.where(mask, 1.0, 0.0).astype(BF16)
    h = x.astype(BF16)
    r1 = x - h.astype(F32)
    mid = r1.astype(BF16)
    lo = (r1 - mid.astype(F32)).astype(BF16)
    return _dg(m, h, _NN) + _dg(m, mid, _NN) + _dg(m, lo, _NN)


def _seg_sum(x, seg_bf16):
    xh = x.astype(BF16)
    xl = (x - xh.astype(F32)).astype(BF16)
    return _dot(xh, seg_bf16) + _dot(xl, seg_bf16)


def _sigmoid(x):
    return 1.0 / (1.0 + jnp.exp(-x))


def _params(sem):
    return pltpu.CompilerParams(dimension_semantics=sem, vmem_limit_bytes=VMEM_LIMIT)


def _const_spec(shape):
    nd = len(shape)
    return pl.BlockSpec(shape, lambda *_: (0,) * nd)


def _ada_kernel(c_ref, w_ref, b_ref, o_ref):
    c = c_ref[...]
    s = c * _sigmoid(c)
    o_ref[...] = _mm3(_split(s), _split(w_ref[...])) + b_ref[...]


def _ada_call(cond8, ada_w, ada_b):
    L, D, W = ada_w.shape
    tn = 1536
    return pl.pallas_call(
        _ada_kernel,
        grid=(L, W // tn),
        in_specs=[pl.BlockSpec((8, D), lambda l, j: (0, 0)),
                  pl.BlockSpec((None, D, tn), lambda l, j: (l, 0, j)),
                  pl.BlockSpec((None, 1, tn), lambda l, j: (l, 0, j))],
        out_specs=pl.BlockSpec((None, 8, tn), lambda l, j: (l, 0, j)),
        out_shape=jax.ShapeDtypeStruct((L, 8, W), F32),
        compiler_params=_params(("parallel", "parallel")),
        name="ada",
    )(cond8, ada_w, ada_b.reshape(L, 1, W))


def _inproj_kernel(x_ref, m_ref, g_ref, w_ref, o_ref, h_scr):
    @pl.when(pl.program_id(1) == 0)
    def _():
        x = x_ref[...]
        y = x * lax.rsqrt(jnp.mean(x * x, axis=-1, keepdims=True) + NORM_EPS) * g_ref[...]
        h = y * (1.0 + m_ref[1:2, :]) + m_ref[0:1, :]
        h_scr[...] = h.astype(BF16)

    o_ref[...] = _dot(h_scr[...], w_ref[...])


def _inproj_call(x, mod, gain, w, rows_per_cond, tn):
    N, D = x.shape
    W = w.shape[1]
    tm = INPROJ_TILE
    assert N % tm == 0 and rows_per_cond % tm == 0, (N, rows_per_cond, tm)
    return pl.pallas_call(
        _inproj_kernel,
        grid=(N // tm, W // tn),
        in_specs=[pl.BlockSpec((tm, D), lambda i, j: (i, 0)),
                  pl.BlockSpec((None, 6, D), lambda i, j: ((i * tm) // rows_per_cond, 0, 0)),
                  pl.BlockSpec((1, D), lambda i, j: (0, 0)),
                  pl.BlockSpec((D, tn), lambda i, j: (0, j))],
        out_specs=pl.BlockSpec((tm, tn), lambda i, j: (i, j)),
        out_shape=jax.ShapeDtypeStruct((N, W), F32),
        scratch_shapes=[pltpu.VMEM((tm, D), BF16)],
        compiler_params=_params(("parallel", "arbitrary")),
        name="inproj",
    )(x, mod, gain, w)


def _rwkv_prep_kernel(z_ref, zp_ref, zn_ref, mu_ref, kkp_ref, ka_ref, rk_ref, w0_ref, a0_ref,
                      w2_ref, a2_ref, g2_ref, seg_ref,
                      fw_o, bw_o, g_o, bon_o, *, tm, blocks_per_seq):
    i = pl.program_id(0)
    z = z_ref[...]
    row = lax.broadcasted_iota(jnp.int32, (tm, 1), 0)
    first = (i % blocks_per_seq) == 0
    last = (i % blocks_per_seq) == blocks_per_seq - 1
    pz = jnp.where(first, 0.0, zp_ref[7:8, :])
    nz = jnp.where(last, 0.0, zn_ref[0:1, :])
    prev = jnp.where(row == 0, pz, pltpu.roll(z, 1, 0))
    nxt = jnp.where(row == tm - 1, nz, pltpu.roll(z, tm - 1, 0))
    zm = z + mu_ref[...] * (0.5 * (prev + nxt) - z)

    Wd = RWKV_WIDTH
    r = zm[:, 0:Wd]
    k = zm[:, Wd:2 * Wd]
    v = zm[:, 2 * Wd:3 * Wd]
    lw = zm[:, 3 * Wd:3 * Wd + 128]
    la = zm[:, 3 * Wd + 128:3 * Wd + 256]
    lg = zm[:, 3 * Wd + 256:3 * Wd + 384]
    seg = seg_ref[...]

    kk = k * kkp_ref[...]
    kk = kk / jnp.maximum(jnp.sqrt(_seg_sum(kk * kk, seg)), 1e-12)
    w_raw = _mm3(_split(jnp.tanh(lw)), _split(w2_ref[...])) + w0_ref[...]
    logw = -_sigmoid(w_raw) * EXP_MINUS_HALF
    a = _sigmoid(_mm3(_split(la), _split(a2_ref[...])) + a0_ref[...])
    ka = ka_ref[...]
    rk = rk_ref[...]
    bonus = jnp.zeros_like(r)
    for d, dir_o in enumerate((fw_o, bw_o)):
        a_d = a[:, d * Wd:(d + 1) * Wd]
        k_d = k * (1.0 + (a_d - 1.0) * ka)
        dir_o[:, 0:Wd] = r
        dir_o[:, Wd:2 * Wd] = v
        dir_o[:, 2 * Wd:3 * Wd] = kk
        dir_o[:, 3 * Wd:4 * Wd] = logw[:, d * Wd:(d + 1) * Wd]
        dir_o[:, 4 * Wd:5 * Wd] = k_d
        dir_o[:, 5 * Wd:6 * Wd] = kk * a_d
        bonus = bonus + _seg_sum(r * k_d * rk, seg) * v
    bon_o[...] = bonus
    g_o[...] = _mm3(_split(_sigmoid(lg)), _split(g2_ref[...]))


def _rwkv_prep_call(zr, seq_len, lw):
    N, Wz = zr.shape
    tm = ROW_TILE
    assert seq_len % tm == 0, (seq_len, tm)
    bps = seq_len // tm
    Wd = RWKV_WIDTH
    nb8 = N // 8
    row_spec = pl.BlockSpec((tm, Wd), lambda i: (i, 0))
    out = jax.ShapeDtypeStruct((N, Wd), F32)
    return pl.pallas_call(
        functools.partial(_rwkv_prep_kernel, tm=tm, blocks_per_seq=bps),
        grid=(N // tm,),
        in_specs=[pl.BlockSpec((tm, Wz), lambda i: (i, 0)),
                  pl.BlockSpec((8, Wz), lambda i: (jnp.maximum(i * (tm // 8) - 1, 0), 0)),
                  pl.BlockSpec((8, Wz), lambda i: (jnp.minimum((i + 1) * (tm // 8), nb8 - 1), 0)),
                  _const_spec((1, Wz)), _const_spec((1, Wd)), _const_spec((1, Wd)), _const_spec((1, Wd)),
                  _const_spec((1, 2 * Wd)), _const_spec((1, 2 * Wd)),
                  _const_spec((128, 2 * Wd)), _const_spec((128, 2 * Wd)), _const_spec((128, Wd)),
                  _const_spec((Wd, Wd))],
        out_specs=[pl.BlockSpec((tm, SCAN_PACK), lambda i: (i, 0))] * 2 + [row_spec] * 2,
        out_shape=[jax.ShapeDtypeStruct((N, SCAN_PACK), F32)] * 2 + [out] * 2,
        compiler_params=_params(("parallel",)),
        name="rwkv_prep",
    )(zr, zr, zr, lw["mu"], lw["kk"], lw["ka"], lw["rk"], lw["w0"], lw["a0"],
      lw["w2"], lw["a2"], lw["g2"], lw["seg64"])


def _tri_masks(d):
    ti = lax.broadcasted_iota(jnp.int32, (CHUNK, CHUNK), 0)
    si = lax.broadcasted_iota(jnp.int32, (CHUNK, CHUNK), 1)
    if d == 0:
        return si <= ti, si < ti
    return si >= ti, si > ti


def _rwkv_chunks(seqs):
    C = CHUNK
    NP = RWKV_HEADS // 2
    Wd = RWKV_WIDTH
    sls = [slice(q * LANES, (q + 1) * LANES) for q in range(NP)]
    ti = lax.broadcasted_iota(jnp.int32, (C, LANES), 0)
    si = lax.broadcasted_iota(jnp.int32, (C, LANES), 1) % RWKV_HEAD_DIM
    low = lax.broadcasted_iota(jnp.int32, (C, LANES), 1) < RWKV_HEAD_DIM
    incl2 = (si <= ti, si >= ti)
    strict2 = (si < ti, si > ti)
    eye = (si == ti).astype(F32)

    def bd(x):
        zero = jnp.zeros_like(x)
        return jnp.concatenate([jnp.where(low, x, zero), jnp.where(low, zero, x)], axis=0)

    def bd2(x):
        return bd(x[0]), bd(x[1])

    base = []
    for x_ref, d, _, _ in seqs:
        incl, _ = _tri_masks(d)
        r, v, kk, lw, k, beta = (x_ref[:, j * Wd:(j + 1) * Wd] for j in range(6))
        b = _cumsum_rows(incl, lw)
        btot = jnp.sum(lw, axis=0, keepdims=True)
        nb = jnp.exp(-b)
        eb = jnp.exp(btot - b)
        base.append(dict(
            ar=_split(jnp.concatenate([-kk * jnp.exp(b - lw), r * jnp.exp(b)], axis=0)),
            bbar=_split(beta * nb), kbar=(k * nb).astype(BF16),
            hat=jnp.concatenate([beta * eb, k * eb], axis=0).astype(BF16),
            v16=v.astype(BF16), gtot=jnp.exp(btot)))

    units = [(n, q) for n in range(len(seqs)) for q in range(NP)]
    dirs = [seqs[n][1] for n, _ in units]

    def col(name, part=None):
        vals = [base[n][name] for n, _ in units]
        if part is not None:
            vals = [x[part] for x in vals]
        return [x[:, sls[q]] for x, (_, q) in zip(vals, units)]

    ar_h, ar_l, v16, hat = col("ar", 0), col("ar", 1), col("v16"), col("hat")
    gtot = col("gtot")
    g1 = [_mm3s((a, b), bd2((c, e)), _NT) for a, b, c, e in zip(ar_h, ar_l, col("bbar", 0), col("bbar", 1))]
    g2 = [_dg(a, bd(kb), _NT) for a, kb in zip(ar_h, col("kbar"))]
    a_ab = [jnp.where(strict2[d], g[:C], 0.0) for g, d in zip(g1, dirs)]
    a_rb = [jnp.where(incl2[d], g[C:], 0.0).astype(BF16) for g, d in zip(g1, dirs)]
    a_ak = [jnp.where(strict2[d], g[:C], 0.0).astype(BF16) for g, d in zip(g2, dirs)]
    a_rk = [jnp.where(incl2[d], g[C:], 0.0).astype(BF16) for g, d in zip(g2, dirs)]
    vbd = [bd(x) for x in v16]
    av = [_dg(a, vv, _NN) for a, vv in zip(a_ak, vbd)]
    t = [eye + a for a in a_ab]
    p = [_split(a) for a in a_ab]
    p = [_split(_mm3s(pp, bd2(pp))) for pp in p]
    for _ in range(4):
        ts = [_split(tt) for tt in t]
        res = [_mm3s((jnp.concatenate([pp[0], tt[0]], axis=0), jnp.concatenate([pp[1], tt[1]], axis=0)),
                     bd2(pp)) for pp, tt in zip(p, ts)]
        p = [_split(rr[:C]) for rr in res]
        t = [tt + rr[C:] for tt, rr in zip(t, res)]
    t = [_split(tt + _mm3s(_split(tt), bd2(pp))) for tt, pp in zip(t, p)]

    s_old = [seqs[n][3][q] for n, q in units]
    z = [_dg(a, bd(s.astype(BF16)), _NT) for a, s in zip(ar_h, s_old)]
    u = [_mm3s(tt, bd2(_split(zz[:C] + aa))).astype(BF16) for tt, zz, aa in zip(t, z, av)]
    ys = [zz[C:] + _dg(jnp.concatenate([rb, rk], axis=1), jnp.concatenate([bd(uu), vv], axis=0), _NN)
          for zz, rb, rk, uu, vv in zip(z, a_rb, a_rk, u, vbd)]
    cross = [_dg(jnp.concatenate([uu, vv], axis=0), hh, _TN)
             for uu, vv, hh in zip(u, v16, hat)]
    stores = []
    for (n, q), yy, s, g, cc in zip(units, ys, s_old, gtot, cross):
        stores.append((seqs[n][2], (slice(None), sls[q]), yy))
        stores.append((seqs[n][3], (q,), s * g + jnp.where(low, cc[:C], cc[C:])))
    return stores


def _gla_chunks(seqs, a2_ref, ab_ref):
    Kw = GLA_K_WIDTH
    lora = 2 * Kw + 2 * GLA_V_WIDTH
    base = []
    for x_ref, d, _, _ in seqs:
        incl, _ = _tri_masks(d)
        xa = (_mm3(_split(x_ref[:, lora:lora + LANES]), _split(a2_ref[:, d * Kw:(d + 1) * Kw]))
              + ab_ref[:, d * Kw:(d + 1) * Kw])
        log_a = (jnp.minimum(xa, 0.0) - jnp.log1p(jnp.exp(-jnp.abs(xa)))) * (1.0 / GLA_GATE_NORMALIZER)
        b = _cumsum_rows(incl, log_a)
        bl = jnp.sum(log_a, axis=0, keepdims=True)
        q = x_ref[:, 0:Kw] * (GLA_DK ** -0.5)
        k = x_ref[:, Kw:2 * Kw]
        base.append(((q * jnp.exp(b)).astype(BF16), (k * jnp.exp(-b)).astype(BF16),
                     (k * jnp.exp(bl - b)).astype(BF16), jnp.exp(bl), incl))

    units = [(n, h) for n in range(len(seqs)) for h in range(GLA_HEADS)]
    ks = [slice(h * GLA_DK, (h + 1) * GLA_DK) for _, h in units]
    vs = [slice(2 * Kw + h * GLA_DV, 2 * Kw + (h + 1) * GLA_DV) for _, h in units]
    q_in = [base[n][0][:, c] for (n, _), c in zip(units, ks)]
    v_h = [seqs[n][0][:, c].astype(BF16) for (n, _), c in zip(units, vs)]
    att = [jnp.where(base[n][4], _dg(qq, base[n][1][:, c], _NT), 0.0).astype(BF16)
           for (n, _), qq, c in zip(units, q_in, ks)]
    o_in = [_dg(a, vv, _NN) for a, vv in zip(att, v_h)]
    kv = [_dg(vv, base[n][2][:, c], _TN) for (n, _), vv, c in zip(units, v_h, ks)]
    s_old = [seqs[n][3][h] for n, h in units]
    outs = [oo + _dg(qq, s.astype(BF16), _NT) for oo, qq, s in zip(o_in, q_in, s_old)]
    stores = []
    for (n, h), c, s, o, x in zip(units, ks, s_old, outs, kv):
        stores.append((seqs[n][2], (slice(None), slice(h * GLA_DV, (h + 1) * GLA_DV)), o))
        stores.append((seqs[n][3], (h,), s * base[n][3][:, c] + x))
    return stores


def _scan_kernel(fw_ref, bw_ref, s0r_ref, xf, xb, a2_ref, ab_ref, s0g_ref,
                 yf_o, yb_o, str_o, of_o, ob_o, stg_o, sr_scr, sg_scr, *, nc):
    i = pl.program_id(1)

    @pl.when(i == 0)
    def _():
        sr_scr[...] = s0r_ref[...]
        sg_scr[...] = s0g_ref[...]

    both = [(bb, d) for bb in range(SCAN_BATCH) for d in range(2)]
    stores = (_rwkv_chunks([((fw_ref, bw_ref)[d].at[bb], d, (yf_o, yb_o)[d].at[bb], sr_scr.at[bb, d])
                            for bb, d in both])
              + _gla_chunks([((xf, xb)[d].at[bb], d, (of_o, ob_o)[d].at[bb], sg_scr.at[bb, d])
                             for bb, d in both], a2_ref, ab_ref))
    for ref, idx, val in stores:
        ref[idx] = val

    @pl.when(i == nc - 1)
    def _():
        str_o[...] = sr_scr[...]
        stg_o[...] = sg_scr[...]


def _scan_call(prep, s0, zg, s0t, a2, ab, B, T):
    fw, bw = prep
    N, Wd = fw.shape[0], RWKV_WIDTH
    Wg = zg.shape[1]
    nc = T // CHUNK
    BB = SCAN_BATCH
    assert T % CHUNK == 0 and B % BB == 0, (B, T)
    H, Dh = RWKV_HEADS, RWKV_HEAD_DIM
    pair_shape = (2, H // 2, Dh, 2 * Dh)
    gla_shape = (2, GLA_HEADS, GLA_DV, GLA_DK)

    def rows(width, backward):
        if backward:
            return pl.BlockSpec((BB, CHUNK, width), lambda b, i: (b, nc - 1 - i, 0))
        return pl.BlockSpec((BB, CHUNK, width), lambda b, i: (b, i, 0))

    st_r = pl.BlockSpec((BB,) + pair_shape, lambda b, i: (b, 0, 0, 0, 0))
    st_g = pl.BlockSpec((BB,) + gla_shape, lambda b, i: (b, 0, 0, 0, 0))
    y = jax.ShapeDtypeStruct((B, T, Wd), F32)
    o = jax.ShapeDtypeStruct((B, T, GLA_V_WIDTH), F32)
    s0p = s0.reshape(B, 2, H // 2, 2, Dh, Dh).transpose(0, 1, 2, 4, 3, 5).reshape((B,) + pair_shape)
    zg3 = zg.reshape(B, T, Wg)
    yf, yb, stp, gf_out, gb_out, stg = pl.pallas_call(
        functools.partial(_scan_kernel, nc=nc),
        grid=(B // BB, nc),
        in_specs=[rows(SCAN_PACK, False), rows(SCAN_PACK, True), st_r,
                  rows(Wg, False), rows(Wg, True),
                  _const_spec((LANES, 2 * GLA_K_WIDTH)), _const_spec((1, 2 * GLA_K_WIDTH)), st_g],
        out_specs=[rows(Wd, False), rows(Wd, True), st_r,
                   rows(GLA_V_WIDTH, False), rows(GLA_V_WIDTH, True), st_g],
        out_shape=[y, y, jax.ShapeDtypeStruct((B,) + pair_shape, F32),
                   o, o, jax.ShapeDtypeStruct((B,) + gla_shape, F32)],
        scratch_shapes=[pltpu.VMEM((BB,) + pair_shape, F32), pltpu.VMEM((BB,) + gla_shape, F32)],
        compiler_params=_params(("parallel", "arbitrary")),
        name="scans",
    )(fw.reshape(B, T, SCAN_PACK), bw.reshape(B, T, SCAN_PACK), s0p, zg3, zg3, a2, ab, s0t)
    st_out = stp.reshape(B, 2, H // 2, Dh, 2, Dh).transpose(0, 1, 2, 4, 3, 5).reshape(B, 2, H, Dh, Dh)
    return (yf.reshape(N, Wd), yb.reshape(N, Wd), st_out,
            gf_out.reshape(N, GLA_V_WIDTH), gb_out.reshape(N, GLA_V_WIDTH), stg)


def _rope(x, cos, sin):
    n = x.shape[1]
    quarter = HEAD_DIM // 4
    lane = lax.broadcasted_iota(jnp.int32, x.shape, 1)
    up = pltpu.roll(x, n - quarter, 1)
    dn = pltpu.roll(x, quarter, 1)
    sw = jnp.where((lane % (2 * quarter)) < quarter, up, dn)
    reps = n // LANES
    if reps > 1:
        cos = jnp.concatenate([cos] * reps, axis=1)
        sin = jnp.concatenate([sin] * reps, axis=1)
    return x * cos + sw * sin


def _attn_prep_kernel(*refs, rope):
    if rope:
        z_ref, qn_ref, kn_ref, segq_ref, segk_ref, cos_ref, sin_ref, q_o, kn_o, kr_o, ve_o = refs
    else:
        z_ref, qn_ref, kn_ref, segq_ref, segk_ref, q_o, kn_o, kr_o, ve_o = refs
    zq = z_ref[:, 0:ATT_WIDTH]
    zk = z_ref[:, ATT_WIDTH:ATT_WIDTH + KV_WIDTH]
    qh = zq * lax.rsqrt(_seg_sum(zq * zq, segq_ref[...]) * (1.0 / HEAD_DIM) + NORM_EPS) * qn_ref[...]
    kh = zk * lax.rsqrt(_seg_sum(zk * zk, segk_ref[...]) * (1.0 / HEAD_DIM) + NORM_EPS) * kn_ref[...]
    kn_o[...] = kh
    if rope:
        qh = _rope(qh, cos_ref[...], sin_ref[...])
        kh = _rope(kh, cos_ref[...], sin_ref[...])
    q_o[...] = (qh * (LOG2_E * HEAD_DIM ** -0.5)).astype(BF16)
    kr_o[...] = kh.astype(BF16)
    v = z_ref[:, ATT_WIDTH + KV_WIDTH:ATT_WIDTH + 2 * KV_WIDTH]
    group = lax.broadcasted_iota(jnp.int32, v.shape, 1) // HEAD_DIM
    for g in range(KV_HEADS):
        ve_o[:, g * LANES:(g + 1) * LANES] = jnp.where(group == g, v, 1.0).astype(BF16)


def _attn_prep_call(za, lw, seq_len, rope_tabs):
    N, Wz = za.shape
    tm = ROW_TILE
    assert seq_len % tm == 0, (seq_len, tm)
    bps = seq_len // tm
    rope = rope_tabs is not None
    in_specs = [pl.BlockSpec((tm, Wz), lambda i: (i, 0)),
                _const_spec((1, ATT_WIDTH)), _const_spec((1, KV_WIDTH)),
                _const_spec((ATT_WIDTH, ATT_WIDTH)), _const_spec((KV_WIDTH, KV_WIDTH))]
    args = [za, lw["qn"], lw["kn"], lw["seg64"], lw["seg64"][:KV_WIDTH, :KV_WIDTH]]
    if rope:
        in_specs += [pl.BlockSpec((tm, LANES), lambda i: (i % bps, 0))] * 2
        args += list(rope_tabs)
    return pl.pallas_call(
        functools.partial(_attn_prep_kernel, rope=rope),
        grid=(N // tm,),
        in_specs=in_specs,
        out_specs=[pl.BlockSpec((tm, ATT_WIDTH), lambda i: (i, 0)),
                   pl.BlockSpec((tm, KV_WIDTH), lambda i: (i, 0)),
                   pl.BlockSpec((tm, KV_WIDTH), lambda i: (i, 0)),
                   pl.BlockSpec((tm, KV_HEADS * LANES), lambda i: (i, 0))],
        out_shape=[jax.ShapeDtypeStruct((N, ATT_WIDTH), BF16),
                   jax.ShapeDtypeStruct((N, KV_WIDTH), F32),
                   jax.ShapeDtypeStruct((N, KV_WIDTH), BF16),
                   jax.ShapeDtypeStruct((N, KV_HEADS * LANES), BF16)],
        compiler_params=_params(("parallel",)),
        name="attn_prep",
    )(*args)


def _attn_kernel(*refs, cached):
    if cached:
        q_ref, k_ref, v_ref, ck_ref, cv_ref, o_ref = refs
    else:
        q_ref, k_ref, v_ref, o_ref = refs
    G = N_HEADS // KV_HEADS
    tq = q_ref.shape[0]
    for g in range(KV_HEADS):
        gs = slice(g * HEAD_DIM, (g + 1) * HEAD_DIM)
        other = slice((1 - g) * HEAD_DIM, (2 - g) * HEAD_DIM)
        kg = k_ref[:, gs]
        vg = v_ref[:, g * LANES:(g + 1) * LANES]
        if cached:
            cv = cv_ref[...]
            own = (lax.broadcasted_iota(jnp.int32, cv.shape, 1) // HEAD_DIM) == g
            ckg = ck_ref[:, gs].astype(BF16)
            cvg = jnp.where(own, cv, 1.0).astype(BF16)
        for h0 in range(g * G, (g + 1) * G, ATT_STACK):
            heads = range(h0, h0 + ATT_STACK)
            qs = jnp.concatenate([q_ref[:, h * HEAD_DIM:(h + 1) * HEAD_DIM] for h in heads], axis=0)
            chunks = [(kg[c:c + KEY_CHUNK], vg[c:c + KEY_CHUNK]) for c in range(0, kg.shape[0], KEY_CHUNK)]
            if cached:
                chunks.append((ckg, cvg))
            m = None
            for kc, _ in chunks:
                mc = jnp.max(_dg(qs, kc, _NT), axis=-1, keepdims=True)
                m = mc if m is None else jnp.maximum(m, mc)
            oe = None
            for kc, vc in chunks:
                part = _dg(jnp.exp2(_dg(qs, kc, _NT) - m).astype(BF16), vc, _NN)
                oe = part if oe is None else oe + part
            o = oe[:, gs] / oe[:, other]
            for j, h in enumerate(heads):
                o_ref[:, h * HEAD_DIM:(h + 1) * HEAD_DIM] = o[j * tq:(j + 1) * tq]


def _attn_call(q, k, vext, cache, B, T):
    N = q.shape[0]
    tq = Q_TILE
    assert T % tq == 0, (T, tq)
    nq = T // tq
    in_specs = [pl.BlockSpec((tq, ATT_WIDTH), lambda b, i: (b * nq + i, 0)),
                pl.BlockSpec((T, KV_WIDTH), lambda b, i: (b, 0)),
                pl.BlockSpec((T, KV_HEADS * LANES), lambda b, i: (b, 0))]
    args = [q, k, vext]
    if cache is not None:
        ck, cv, layer = cache
        P = ck.shape[2]
        cspec = pl.BlockSpec((None, None, P, KV_WIDTH), lambda b, i: (b, layer, 0, 0))
        in_specs += [cspec, cspec]
        args += [ck, cv]
    return pl.pallas_call(
        functools.partial(_attn_kernel, cached=cache is not None),
        grid=(B, nq),
        in_specs=in_specs,
        out_specs=pl.BlockSpec((tq, ATT_WIDTH), lambda b, i: (b * nq + i, 0)),
        out_shape=jax.ShapeDtypeStruct((N, ATT_WIDTH), F32),
        compiler_params=_params(("parallel", "arbitrary")),
        name="attention",
    )(*args)


def _merge_kernel(yf, yb, bon, g, oa, gf, gb, gr, mr, ma, mg, x_ref, m_ref,
                  lng, lnb, gn, nf, seg_ref, pr, pa, pg, wo, rw, rb,
                  x1_o, h2_o, ti_o, tg_o):
    seg = seg_ref[...]
    inv = 1.0 / RWKV_HEAD_DIM
    y = yf[...] + yb[...]
    mu = _seg_sum(y, seg) * inv
    yc = y - mu
    var = _seg_sum(yc * yc, seg) * inv
    o_r = (yc * lax.rsqrt(var + RWKV_GN_EPS) * lng[...] + lnb[...] + bon[...]) * g[...]

    gate_r = gr[...]
    silu_r = gate_r * _sigmoid(gate_r)
    gnv = gn[...]
    cols = []
    for h in range(GLA_HEADS):
        vs = slice(h * GLA_DV, (h + 1) * GLA_DV)
        o = gf[:, vs] + gb[:, vs]
        ms = jnp.mean(o * o, axis=-1, keepdims=True)
        cols.append(o * lax.rsqrt(ms + NORM_EPS) * gnv * silu_r[:, vs])
    o_g = jnp.concatenate(cols, axis=1)

    merged = (_sigmoid(mr[...]) * _dot(o_r.astype(BF16), pr[...])
              + _sigmoid(ma[...]) * _dot(oa[...].astype(BF16), pa[...])
              + _sigmoid(mg[...]) * _dot(o_g.astype(BF16), pg[...]))
    x1 = x_ref[...] + m_ref[2:3, :] * _dot(merged.astype(BF16), wo[...])
    x1_o[...] = x1
    hn = x1 * lax.rsqrt(jnp.mean(x1 * x1, axis=-1, keepdims=True) + NORM_EPS) * nf[...]
    h2 = hn * (1.0 + m_ref[4:5, :]) + m_ref[3:4, :]
    h2_o[...] = h2

    logits = _mm3(_split(h2), _split(rw[...])) + rb[...]
    lane = lax.broadcasted_iota(jnp.int32, logits.shape, 1)
    vals, idxs = [], []
    for _ in range(TOP_K):
        m = jnp.max(logits, axis=-1, keepdims=True)
        idx = jnp.min(jnp.where(logits == m, lane, LANES), axis=-1, keepdims=True)
        vals.append(m)
        idxs.append(idx)
        logits = jnp.where(lane == idx, -jnp.inf, logits)
    es = [jnp.exp(vv - vals[0]) for vv in vals]
    den = es[0] + es[1] + es[2] + es[3]
    ti = jnp.zeros(logits.shape, jnp.int32)
    tg = jnp.zeros(logits.shape, F32)
    for j in range(TOP_K):
        ti = jnp.where(lane == j, idxs[j], ti)
        tg = jnp.where(lane == j, es[j] / den, tg)
    ti_o[...] = ti
    tg_o[...] = tg


def _merge_call(yf, yb, bon, g, oa, gf, gb, zg, zm, x, mod, lw, rows_per_cond):
    N, D = x.shape
    tm = ROW_TILE
    Wd = RWKV_WIDTH

    def rows(w, col=0):
        return pl.BlockSpec((tm, w), lambda i: (i, col))

    in_specs = [rows(Wd)] * 4 + [rows(ATT_WIDTH), rows(GLA_V_WIDTH), rows(GLA_V_WIDTH),
                                 rows(GLA_V_WIDTH, 2), rows(D, 0), rows(D, 1), rows(D, 2), rows(D),
                                 pl.BlockSpec((None, 6, D), lambda i: ((i * tm) // rows_per_cond, 0, 0)),
                                 _const_spec((1, Wd)), _const_spec((1, Wd)), _const_spec((1, GLA_DV)),
                                 _const_spec((1, D)), _const_spec((Wd, Wd)),
                                 _const_spec((Wd, D)), _const_spec((ATT_WIDTH, D)), _const_spec((GLA_V_WIDTH, D)),
                                 _const_spec((D, D)), _const_spec((D, LANES)), _const_spec((1, LANES))]
    return pl.pallas_call(
        _merge_kernel,
        grid=(N // tm,),
        in_specs=in_specs,
        out_specs=[rows(D), rows(D), rows(LANES), rows(LANES)],
        out_shape=[jax.ShapeDtypeStruct((N, D), F32), jax.ShapeDtypeStruct((N, D), F32),
                   jax.ShapeDtypeStruct((N, LANES), jnp.int32), jax.ShapeDtypeStruct((N, LANES), F32)],
        compiler_params=_params(("parallel",)),
        name="merge",
    )(yf, yb, bon, g, oa, gf, gb, zg, zm, zm, zm, x, mod,
      lw["ln_g"], lw["ln_b"], lw["gla_norm"], lw["norm_ffn"], lw["seg64"],
      lw["p_rwkv"], lw["p_attn"], lw["p_gla"], lw["w_out"], lw["router_w"], lw["router_b"])


def _moe_kernel(be_ref, na_ref, x_ref, gate_ref, w1_ref, b1_ref, w2_ref, b2_ref, o_ref, w1_scr, w2_scr):
    i = pl.program_id(0)

    @pl.when(jnp.logical_or(i == 0, be_ref[i] != be_ref[jnp.maximum(i - 1, 0)]))
    def _():
        w1_scr[...] = w1_ref[...].astype(BF16)
        w2_scr[...] = w2_ref[...].astype(BF16)

    @pl.when(i < na_ref[0])
    def _():
        z = _dot(x_ref[...].astype(BF16), w1_scr[...]) + b1_ref[...]
        glu = jnp.minimum(z[:, :D_EXPERT], SWIGLU_LIMIT)
        lin = jnp.clip(z[:, D_EXPERT:], -SWIGLU_LIMIT, SWIGLU_LIMIT)
        act = glu * _sigmoid(SWIGLU_ALPHA * glu) * (lin + 1.0)
        o_ref[...] = (_dot(act.astype(BF16), w2_scr[...]) + b2_ref[...]) * gate_ref[...]

    @pl.when(i >= na_ref[0])
    def _():
        o_ref[...] = jnp.zeros_like(o_ref)


def _moe_call(xg, row_gate, block_exp, n_active, layer, w1, b1, w2, b2):
    R, D = xg.shape
    tb = EXPERT_BLOCK
    F2 = w1.shape[3]
    grid_spec = pltpu.PrefetchScalarGridSpec(
        num_scalar_prefetch=2,
        grid=(R // tb,),
        in_specs=[pl.BlockSpec((tb, D), lambda i, be, na: (i, 0)),
                  pl.BlockSpec((tb, 1), lambda i, be, na: (i, 0)),
                  pl.BlockSpec((None, None, D, F2), lambda i, be, na: (layer, be[i], 0, 0)),
                  pl.BlockSpec((None, 1, F2), lambda i, be, na: (be[i], 0, 0)),
                  pl.BlockSpec((None, None, F2 // 2, D), lambda i, be, na: (layer, be[i], 0, 0)),
                  pl.BlockSpec((None, 1, D), lambda i, be, na: (be[i], 0, 0))],
        out_specs=pl.BlockSpec((tb, D), lambda i, be, na: (i, 0)),
        scratch_shapes=[pltpu.VMEM((D, F2), BF16), pltpu.VMEM((F2 // 2, D), BF16)],
    )
    return pl.pallas_call(
        _moe_kernel,
        grid_spec=grid_spec,
        out_shape=jax.ShapeDtypeStruct((R, D), F32),
        compiler_params=_params(("arbitrary",)),
        name="moe_experts",
    )(block_exp, n_active, xg, row_gate, w1, b1, w2, b2)


def _combine_kernel(x1_ref, m_ref, y0_ref, y1_ref, y2_ref, y3_ref, o_ref):
    y = (y0_ref[...] + y1_ref[...]) + (y2_ref[...] + y3_ref[...])
    o_ref[...] = x1_ref[...] + m_ref[5:6, :] * y


def _combine_call(x1, mod, ys, rows_per_cond):
    N, D = x1.shape
    tm = ROW_TILE
    rows = pl.BlockSpec((tm, D), lambda i: (i, 0))
    return pl.pallas_call(
        _combine_kernel,
        grid=(N // tm,),
        in_specs=([rows, pl.BlockSpec((None, 6, D), lambda i: ((i * tm) // rows_per_cond, 0, 0))]
                  + [pl.BlockSpec((None, tm, D), functools.partial(lambda i, j: (j, i, 0), j=j))
                     for j in range(TOP_K)]),
        out_specs=rows,
        out_shape=jax.ShapeDtypeStruct((N, D), F32),
        compiler_params=_params(("parallel",)),
        name="combine",
    )(x1, mod, *([ys] * TOP_K))


def _moe_ffn(h2, top_i, top_g, lw):
    N, D = h2.shape
    n_as = N * TOP_K
    n_blocks = -(-n_as // EXPERT_BLOCK) + N_EXPERTS
    R = n_blocks * EXPERT_BLOCK
    ex = jnp.arange(N_EXPERTS, dtype=jnp.int32)

    def lut(onehot, table):
        return jnp.sum(jnp.where(onehot, table[None, :], 0), axis=1)

    flat_e = top_i.reshape(n_as)
    flat_g = top_g.reshape(n_as)
    oh_a = flat_e[:, None] == ex[None, :]
    counts = jnp.sum(oh_a, axis=0, dtype=jnp.int32)
    starts = jnp.cumsum(counts) - counts
    padded = (counts + EXPERT_BLOCK - 1) // EXPERT_BLOCK * EXPERT_BLOCK
    pends = jnp.cumsum(padded)
    pstarts = pends - padded
    _, order, gate_sorted = lax.sort((flat_e, jnp.arange(n_as, dtype=jnp.int32), flat_g), num_keys=1, is_stable=True)
    rank = jnp.argsort(order).astype(jnp.int32)
    pos = rank + lut(oh_a, pstarts - starts)
    blk_start = jnp.arange(n_blocks, dtype=jnp.int32) * EXPERT_BLOCK
    block_exp = jnp.minimum(jnp.sum(pends[None, :] <= blk_start[:, None], axis=1),
                            N_EXPERTS - 1).astype(jnp.int32)
    n_active = (pends[-1] // EXPERT_BLOCK).astype(jnp.int32).reshape(1)
    oh_p = jnp.repeat(block_exp, EXPERT_BLOCK)[:, None] == ex[None, :]
    idx = jnp.arange(R, dtype=jnp.int32) - lut(oh_p, pstarts)
    valid = (idx >= 0) & (idx < lut(oh_p, counts))
    table = jnp.stack([order, lax.bitcast_convert_type(gate_sorted, jnp.int32)], axis=1)
    picked = table[jnp.clip(lut(oh_p, starts) + idx, 0, n_as - 1)]
    row_tok = jnp.where(valid, picked[:, 0] // TOP_K, 0)
    row_gate = jnp.where(valid, lax.bitcast_convert_type(picked[:, 1], F32), 0.0)
    out = _moe_call(h2[row_tok], row_gate.reshape(R, 1), block_exp, n_active,
                    lw["layer"], lw["moe_w1"], lw["moe_b1"], lw["moe_w2"], lw["moe_b2"])
    return out[pos.reshape(N, TOP_K).T.reshape(n_as)].reshape(TOP_K, N, D)


def _seg_matrix(n, seg):
    idx = jnp.arange(n) // seg
    return (idx[:, None] == idx[None, :]).astype(BF16)


def _rope_tables(T):
    quarter = HEAD_DIM // 4
    inv_freq = ROPE_THETA ** (-jnp.arange(quarter, dtype=F32) / quarter)
    t = jnp.arange(T)
    row = (t // GRID_W).astype(F32)
    col = (t % GRID_W).astype(F32)
    ang_r = row[:, None] * inv_freq
    ang_c = col[:, None] * inv_freq
    cos = jnp.concatenate([jnp.cos(ang_r)] * 2 + [jnp.cos(ang_c)] * 2, axis=1)
    sin = jnp.concatenate([-jnp.sin(ang_r), jnp.sin(ang_r), -jnp.sin(ang_c), jnp.sin(ang_c)], axis=1)
    return jnp.concatenate([cos, cos], axis=1), jnp.concatenate([sin, sin], axis=1)


def _layer_weights(p, l):
    Wd = RWKV_WIDTH
    w_in = p["w_in"][l]
    c0 = RWKV_COLS
    aq, ak, av, gq, gk, gv, gl, gr = (c0, c0 + 512, c0 + 640, c0 + 768, c0 + 1280, c0 + 1792, c0 + 2816, c0 + 2848)
    m0 = gr + GLA_V_WIDTH
    D = D_MODEL
    w_g = jnp.concatenate([w_in[:, gq:gl], w_in[:, gr:m0], w_in[:, gl:gr],
                           jnp.zeros((D, LANES - 2 * GLA_LORA), F32)], axis=1)
    z64 = jnp.zeros((DECAY_LORA, Wd), F32)
    w2 = jnp.concatenate([jnp.concatenate([p["rwkv_w2"][l, 0], z64], axis=1),
                          jnp.concatenate([z64, p["rwkv_w2"][l, 1]], axis=1)], axis=0)
    a2 = jnp.concatenate([jnp.concatenate([p["rwkv_a2"][l, 0], z64], axis=1),
                          jnp.concatenate([z64, p["rwkv_a2"][l, 1]], axis=1)], axis=0)
    z16 = jnp.zeros((GLA_LORA, GLA_K_WIDTH), F32)
    ga2 = jnp.concatenate([jnp.concatenate([p["gla_a2"][l, 0], z16], axis=1),
                           jnp.concatenate([z16, p["gla_a2"][l, 1]], axis=1),
                           jnp.zeros((LANES - 2 * GLA_LORA, 2 * GLA_K_WIDTH), F32)], axis=0)
    return {
        "w_r": w_in[:, :c0].astype(BF16), "w_a": w_in[:, aq:gq].astype(BF16),
        "w_g": w_g.astype(BF16), "w_m": w_in[:, m0:].astype(BF16),
        "norm_mix": p["norm_mix"][l].reshape(1, D), "norm_ffn": p["norm_ffn"][l].reshape(1, D),
        "mu": p["rwkv_mu"][l].reshape(1, c0),
        "kk": p["rwkv_kk"][l].reshape(1, Wd), "ka": p["rwkv_ka"][l].reshape(1, Wd),
        "rk": p["rwkv_rk"][l].reshape(1, Wd),
        "w0": p["rwkv_w0"][l].reshape(1, 2 * Wd), "a0": p["rwkv_a0"][l].reshape(1, 2 * Wd),
        "w2": w2, "a2": a2, "g2": p["rwkv_g2"][l],
        "ln_g": p["rwkv_ln_g"][l].reshape(1, Wd), "ln_b": p["rwkv_ln_b"][l].reshape(1, Wd),
        "seg64": _seg_matrix(Wd, RWKV_HEAD_DIM),
        "qn": jnp.tile(p["attn_qn"][l], N_HEADS).reshape(1, ATT_WIDTH),
        "kn": jnp.tile(p["attn_kn"][l], KV_HEADS).reshape(1, KV_WIDTH),
        "gla_a2": ga2, "gla_ab": p["gla_ab"][l].reshape(1, 2 * GLA_K_WIDTH),
        "gla_norm": p["gla_norm"][l].reshape(1, GLA_DV),
        "p_rwkv": p["p_rwkv"][l].astype(BF16), "p_attn": p["p_attn"][l].astype(BF16),
        "p_gla": p["p_gla"][l].astype(BF16), "w_out": p["w_out"][l].astype(BF16),
        "router_w": jnp.concatenate([p["router_w"][l], jnp.zeros((D, LANES - N_EXPERTS), F32)], axis=1),
        "router_b": jnp.concatenate([p["router_b"][l], jnp.full((LANES - N_EXPERTS,), -jnp.inf, F32)]).reshape(1, LANES),
        "layer": l, "moe_w1": p["moe_w1"], "moe_b1": p["moe_b1"][l].reshape(N_EXPERTS, 1, 2 * D_EXPERT),
        "moe_w2": p["moe_w2"], "moe_b2": p["moe_b2"][l].reshape(N_EXPERTS, 1, D_MODEL),
    }


def _trunk_layer(x, mod, lw, B, T, ctx):
    N = B * T
    rpc = N if mod.shape[0] == 1 else T
    zr, za, zg, zm = (_inproj_call(x, mod, lw["norm_mix"], lw[name], rpc, lw[name].shape[1])
                      for name in ("w_r", "w_a", "w_g", "w_m"))

    fw, bw, g, bon = _rwkv_prep_call(zr, T, lw)
    if ctx is None:
        s0r = jnp.zeros((B, 2, RWKV_HEADS, RWKV_HEAD_DIM, RWKV_HEAD_DIM), F32)
        s0g = jnp.zeros((B, 2, GLA_HEADS, GLA_DV, GLA_DK), F32)
        cache, tabs = None, None
    else:
        s0r = ctx["rwkv"]
        s0g = jnp.swapaxes(ctx["gla"], -1, -2)
        cache, tabs = (ctx["k"], ctx["v"], ctx["layer"]), _rope_tables(T)
    yf, yb, s_r, gf, gb, s_g = _scan_call((fw, bw), s0r, zg, s0g, lw["gla_a2"], lw["gla_ab"], B, T)
    q, kn, kr, vext = _attn_prep_call(za, lw, T, tabs)
    oa = _attn_call(q, kr, vext, cache, B, T)
    x1, h2, ti, tg = _merge_call(yf, yb, bon, g, oa, gf, gb, zg, zm, x, mod, lw, rpc)
    y4 = _moe_ffn(h2, ti[:, :TOP_K], tg[:, :TOP_K], lw)
    x2 = _combine_call(x1, mod, y4, rpc)
    return x2, (kn, za[:, ATT_WIDTH + KV_WIDTH:], s_r, jnp.swapaxes(s_g, -1, -2))


def kernel(x_prompt, x_sample, c, cache_k, cache_v, state_rwkv, state_gla, c_ctx, ada_w, ada_b, norm_mix, norm_ffn, w_in, rwkv_mu, rwkv_w0, rwkv_w2, rwkv_a0, rwkv_a2, rwkv_g2, rwkv_kk, rwkv_ka, rwkv_rk, rwkv_ln_g, rwkv_ln_b, attn_qn, attn_kn, gla_a2, gla_ab, gla_norm, p_rwkv, p_attn, p_gla, w_out, router_w, router_b, moe_w1, moe_b1, moe_w2, moe_b2):
    p = {"w_in": w_in, "norm_mix": norm_mix, "norm_ffn": norm_ffn, "rwkv_mu": rwkv_mu, "rwkv_w0": rwkv_w0,
         "rwkv_w2": rwkv_w2, "rwkv_a0": rwkv_a0, "rwkv_a2": rwkv_a2, "rwkv_g2": rwkv_g2, "rwkv_kk": rwkv_kk,
         "rwkv_ka": rwkv_ka, "rwkv_rk": rwkv_rk, "rwkv_ln_g": rwkv_ln_g, "rwkv_ln_b": rwkv_ln_b,
         "attn_qn": attn_qn, "attn_kn": attn_kn, "gla_a2": gla_a2, "gla_ab": gla_ab, "gla_norm": gla_norm,
         "p_rwkv": p_rwkv, "p_attn": p_attn, "p_gla": p_gla, "w_out": w_out, "router_w": router_w,
         "router_b": router_b, "moe_w1": moe_w1, "moe_b1": moe_b1, "moe_w2": moe_w2, "moe_b2": moe_b2}
    Bp, Tp, D = x_prompt.shape
    Bs, Ts, _ = x_sample.shape
    L = ada_w.shape[0]
    P = cache_k.shape[2]

    cond8 = jnp.concatenate([c_ctx[None, :], c, jnp.zeros((8 - 1 - Bs, D), F32)], axis=0)
    mod = _ada_call(cond8, ada_w, ada_b).reshape(L, 8, 6, D)
    weights = [_layer_weights(p, l) for l in range(L)]
    ck = cache_k.reshape(Bs, L, P, KV_WIDTH)
    cv = cache_v.reshape(Bs, L, P, KV_WIDTH)

    yp = x_prompt.reshape(Bp * Tp, D)
    ks, vs, srs, sgs = [], [], [], []
    for l in range(L):
        yp, (kn, vh, s_r, s_g) = _trunk_layer(yp, mod[l, 0:1], weights[l], Bp, Tp, None)
        ks.append(kn.reshape(Bp, Tp, KV_HEADS, HEAD_DIM))
        vs.append(vh.reshape(Bp, Tp, KV_HEADS, HEAD_DIM))
        srs.append(s_r)
        sgs.append(s_g)

    ys = x_sample.reshape(Bs * Ts, D)
    for l in range(L):
        ctx = {"k": ck, "v": cv, "layer": l, "rwkv": state_rwkv[:, l], "gla": state_gla[:, l]}
        ys, _ = _trunk_layer(ys, mod[l, 1:1 + Bs], weights[l], Bs, Ts, ctx)

    return (yp.reshape(Bp, Tp, D), ys.reshape(Bs, Ts, D),
            jnp.stack(ks, axis=1), jnp.stack(vs, axis=1),
            jnp.stack(srs, axis=1), jnp.stack(sgs, axis=1))
```
